```python
import math, functools
import jax, jax.numpy as jnp
from jax import lax
import numpy as np

D_MODEL = 1024
BATCH = 16
SEQ = 256
DEPTH = 2
DEC_BATCH = 8
DEC_SEQ = 4096
PAST_LEN = 256

GRID_W = 64
N_MIXERS = 2
N_LRU_LAYERS = (DEPTH + 1) // 2
N_S5_LAYERS = DEPTH // 2
N_DIR = 2
D_RNN = D_MODEL
LRU_HEADS = 16
LRU_BLOCK = D_RNN // LRU_HEADS
LRU_C = 8.0
LRU_CONV = 4
S5_GROUP = 16
S5_GROUPS = D_MODEL // S5_GROUP
S5_STATE = 64
D_FF = 2816
FFN_CONV = 3
N_MOD = 6
EPS = 1e-6

kernel_name = "hybrid_rglru_s5_diffusion_step"

F32 = jnp.float32


def rms_norm(x, g):
    x32 = x.astype(F32)
    y = x32 * lax.rsqrt(jnp.mean(x32 * x32, axis=-1, keepdims=True) + EPS)
    return (y * g.astype(F32)).astype(x.dtype)


def modulate(x, g, shift, scale):
    return rms_norm(x, g) * (1 + scale) + shift


def depthwise_conv_seq(x, w, b, pad_left, pad_right):
    L = x.shape[1]
    xp = jnp.pad(x, ((0, 0), (pad_left, pad_right), (0, 0)))
    y = b
    for k in range(w.shape[0]):
        y = y + xp[:, k:k + L] * w[k]
    return y


def _combine(e1, e2):
    a1, b1 = e1
    a2, b2 = e2
    return a1 * a2, a2 * b1 + b2


def linear_scan(a, b, h0, reverse):
    edge = -1 if reverse else 0
    b = b.at[:, edge].add(a[:, edge] * h0)
    _, h = lax.associative_scan(_combine, (a, b), reverse=reverse, axis=1)
    return h, (h[:, 0] if reverse else h[:, -1])


def rglru_mixer(h, h0, w_in, conv_w, conv_b, w_a, b_a, w_i, b_i, lam, w_out):
    B, L, _ = h.shape
    gate, xr = jnp.split(h @ w_in, 2, axis=-1)
    xr = depthwise_conv_seq(xr, conv_w, conv_b, 2, 1).astype(F32)
    xb = xr.reshape(B, L, LRU_HEADS, LRU_BLOCK)
    y = 0.0
    finals = []
    for d in range(N_DIR):
        r = jax.nn.sigmoid(jnp.einsum('blhi,hij->blhj', xb, w_a[d].astype(F32)).reshape(B, L, D_RNN) + b_a[d].astype(F32))
        i = jax.nn.sigmoid(jnp.einsum('blhi,hij->blhj', xb, w_i[d].astype(F32)).reshape(B, L, D_RNN) + b_i[d].astype(F32))
        log_a = -LRU_C * r * jax.nn.softplus(-lam[d].astype(F32))
        a = jnp.exp(log_a)
        bx = jnp.sqrt(-jnp.expm1(2.0 * log_a)) * (i * xr)
        hs, hf = linear_scan(a, bx, h0[:, d], reverse=(d == 1))
        y = y + hs
        finals.append(hf)
    out = (y * jax.nn.gelu(gate.astype(F32))).astype(h.dtype) @ w_out
    return out, jnp.stack(finals, axis=1)


def s5_mixer(h, h0, a_re, a_im, log_dt, b_re, b_im, c_re, c_im, d_skip, w_glu):
    B, L, _ = h.shape
    u = h.astype(F32).reshape(B, L, S5_GROUPS, S5_GROUP)
    uc = u.astype(jnp.complex64)
    y = d_skip.astype(F32).reshape(S5_GROUPS, S5_GROUP) * u
    finals = []
    for d in range(N_DIR):
        lam = lax.complex(a_re[d].astype(F32), a_im[d].astype(F32))
        dt = jnp.exp(log_dt[d].astype(F32))[:, None]
        a_bar = jnp.exp(lam * dt)
        b_bar = lax.complex(b_re[d].astype(F32), b_im[d].astype(F32)) * ((a_bar - 1.0) / lam)[..., None]
        bu = jnp.einsum('blgc,gpc->blgp', uc, b_bar)
        a_seq = jnp.broadcast_to(a_bar, (1, L) + a_bar.shape)
        hs, hf = linear_scan(a_seq, bu, h0[:, d], reverse=(d == 1))
        cc = lax.complex(c_re[d].astype(F32), c_im[d].astype(F32))
        y = y + jnp.real(jnp.einsum('blgp,gcp->blgc', hs, cc))
        finals.append(hf)
    z = jax.nn.gelu(y.reshape(B, L, D_MODEL)).astype(h.dtype)
    v, g = jnp.split(z @ w_glu, 2, axis=-1)
    return v * jax.nn.sigmoid(g), jnp.stack(finals, axis=1)


def conv_ffn(h, on_grid, w_up, conv_w, conv_b, w_down):
    B, L, _ = h.shape
    up = h @ w_up
    if on_grid:
        rows = L // GRID_W
        up = depthwise_conv_seq(up.reshape(B * rows, GRID_W, 2 * D_FF), conv_w, conv_b, 1, 1).reshape(B, L, 2 * D_FF)
    else:
        up = depthwise_conv_seq(up, conv_w, conv_b, 1, 1)
    v, g = jnp.split(up, 2, axis=-1)
    return (v * jax.nn.silu(g)) @ w_down


def apply_layer(x, mod, h0, on_grid, mixer, g_mix, g_ffn, w_up, conv_w, conv_b, w_down):
    sh1, sc1, gt1, sh2, sc2, gt2 = jnp.split(mod, N_MOD, axis=-1)
    out, final = mixer(modulate(x, g_mix, sh1, sc1), h0)
    x = x + gt1 * out
    x = x + gt2 * conv_ffn(modulate(x, g_ffn, sh2, sc2), on_grid, w_up, conv_w, conv_b, w_down)
    return x, final


def setup_inputs(seed: int = 0) -> dict:
    key = jax.random.key(seed)
    ks = iter(jax.random.split(key, 40))

    def nrm(shape, scale):
        return jax.random.normal(next(ks), shape, F32) * scale

    u = jax.random.uniform(next(ks), (N_LRU_LAYERS, N_DIR, D_RNN), F32, minval=0.9, maxval=0.999)
    lru_lambda = jnp.log(u) - jnp.log1p(-u)
    a_im = jnp.pi * jnp.arange(S5_STATE, dtype=F32)
    return {
        "x_prompt": nrm((BATCH, SEQ, D_MODEL), 1.0),
        "x_sample": nrm((DEC_BATCH, DEC_SEQ, D_MODEL), 1.0),
        "state_lru": nrm((DEC_BATCH, N_LRU_LAYERS, N_DIR, D_RNN), 0.5),
        "state_s5_re": nrm((DEC_BATCH, N_S5_LAYERS, N_DIR, S5_GROUPS, S5_STATE), 0.5),
        "state_s5_im": nrm((DEC_BATCH, N_S5_LAYERS, N_DIR, S5_GROUPS, S5_STATE), 0.5),
        "c": nrm((DEC_BATCH, D_MODEL), 1.0),
        "c_ctx": nrm((D_MODEL,), 1.0),
        "ada_w": nrm((DEPTH, D_MODEL, N_MOD * D_MODEL), 0.5 * D_MODEL ** -0.5),
        "ada_b": nrm((DEPTH, N_MOD * D_MODEL), 0.02),
        "norm_mix": 1.0 + nrm((DEPTH, D_MODEL), 0.02),
        "norm_ffn": 1.0 + nrm((DEPTH, D_MODEL), 0.02),
        "norm_final": 1.0 + nrm((D_MODEL,), 0.02),
        "lru_w_in": nrm((N_LRU_LAYERS, D_MODEL, 2 * D_RNN), D_MODEL ** -0.5),
        "lru_conv_w": nrm((N_LRU_LAYERS, LRU_CONV, D_RNN), 0.5),
        "lru_conv_b": nrm((N_LRU_LAYERS, D_RNN), 0.02),
        "lru_w_a": nrm((N_LRU_LAYERS, N_DIR, LRU_HEADS, LRU_BLOCK, LRU_BLOCK), LRU_BLOCK ** -0.5),
        "lru_b_a": nrm((N_LRU_LAYERS, N_DIR, D_RNN), 0.02),
        "lru_w_i": nrm((N_LRU_LAYERS, N_DIR, LRU_HEADS, LRU_BLOCK, LRU_BLOCK), LRU_BLOCK ** -0.5),
        "lru_b_i": nrm((N_LRU_LAYERS, N_DIR, D_RNN), 0.02),
        "lru_lambda": lru_lambda,
        "lru_w_out": nrm((N_LRU_LAYERS, D_RNN, D_MODEL), D_RNN ** -0.5),
        "s5_a_re": -0.5 + nrm((N_S5_LAYERS, N_DIR, S5_GROUPS, S5_STATE), 0.01),
        "s5_a_im": a_im + nrm((N_S5_LAYERS, N_DIR, S5_GROUPS, S5_STATE), 0.01),
        "s5_log_dt": jax.random.uniform(next(ks), (N_S5_LAYERS, N_DIR, S5_GROUPS), F32, minval=math.log(0.001), maxval=math.log(0.1)),
        "s5_b_re": nrm((N_S5_LAYERS, N_DIR, S5_GROUPS, S5_STATE, S5_GROUP), (2 * S5_GROUP) ** -0.5),
        "s5_b_im": nrm((N_S5_LAYERS, N_DIR, S5_GROUPS, S5_STATE, S5_GROUP), (2 * S5_GROUP) ** -0.5),
        "s5_c_re": nrm((N_S5_LAYERS, N_DIR, S5_GROUPS, S5_GROUP, S5_STATE), (2 * S5_STATE) ** -0.5),
        "s5_c_im": nrm((N_S5_LAYERS, N_DIR, S5_GROUPS, S5_GROUP, S5_STATE), (2 * S5_STATE) ** -0.5),
        "s5_d": nrm((N_S5_LAYERS, D_MODEL), 0.5),
        "s5_w_glu": nrm((N_S5_LAYERS, D_MODEL, 2 * D_MODEL), D_MODEL ** -0.5),
        "ffn_w_up": nrm((DEPTH, D_MODEL, 2 * D_FF), D_MODEL ** -0.5),
        "ffn_conv_w": nrm((DEPTH, FFN_CONV, 2 * D_FF), 0.5),
        "ffn_conv_b": nrm((DEPTH, 2 * D_FF), 0.02),
        "ffn_w_down": nrm((DEPTH, D_FF, D_MODEL), D_FF ** -0.5),
    }


def reference(x_prompt, x_sample, state_lru, state_s5_re, state_s5_im, c, c_ctx,
              ada_w, ada_b, norm_mix, norm_ffn, norm_final,
              lru_w_in, lru_conv_w, lru_conv_b, lru_w_a, lru_b_a, lru_w_i, lru_b_i, lru_lambda, lru_w_out,
              s5_a_re, s5_a_im, s5_log_dt, s5_b_re, s5_b_im, s5_c_re, s5_c_im, s5_d, s5_w_glu,
              ffn_w_up, ffn_conv_w, ffn_conv_b, ffn_w_down):
    x_p, x_s = x_prompt, x_sample
    bp = x_p.shape[0]
    new_lru, new_s5 = [], []
    for l in range(DEPTH):
        j = l // N_MIXERS
        mod_p = (jax.nn.silu(c_ctx) @ ada_w[l] + ada_b[l])[None, None, :]
        mod_s = (jax.nn.silu(c) @ ada_w[l] + ada_b[l])[:, None, :]
        if l % N_MIXERS == 0:
            mixer = functools.partial(rglru_mixer, w_in=lru_w_in[j], conv_w=lru_conv_w[j], conv_b=lru_conv_b[j],
                                      w_a=lru_w_a[j], b_a=lru_b_a[j], w_i=lru_w_i[j], b_i=lru_b_i[j],
                                      lam=lru_lambda[j], w_out=lru_w_out[j])
            h0_p = jnp.zeros((bp, N_DIR, D_RNN), F32)
            h0_s = state_lru[:, j].astype(F32)
        else:
            mixer = functools.partial(s5_mixer, a_re=s5_a_re[j], a_im=s5_a_im[j], log_dt=s5_log_dt[j],
                                      b_re=s5_b_re[j], b_im=s5_b_im[j], c_re=s5_c_re[j], c_im=s5_c_im[j],
                                      d_skip=s5_d[j], w_glu=s5_w_glu[j])
            h0_p = jnp.zeros((bp, N_DIR, S5_GROUPS, S5_STATE), jnp.complex64)
            h0_s = lax.complex(state_s5_re[:, j].astype(F32), state_s5_im[:, j].astype(F32))
        x_p, fin_p = apply_layer(x_p, mod_p, h0_p, False, mixer, norm_mix[l], norm_ffn[l],
                                 ffn_w_up[l], ffn_conv_w[l], ffn_conv_b[l], ffn_w_down[l])
        x_s, _ = apply_layer(x_s, mod_s, h0_s, True, mixer, norm_mix[l], norm_ffn[l],
                             ffn_w_up[l], ffn_conv_w[l], ffn_conv_b[l], ffn_w_down[l])
        if l % N_MIXERS == 0:
            new_lru.append(fin_p)
        else:
            new_s5.append(fin_p)
    y_prompt = rms_norm(x_p, norm_final)
    y_sample = rms_norm(x_s, norm_final)
    new_state_lru = jnp.stack(new_lru, axis=1)
    s5_state = jnp.stack(new_s5, axis=1)
    return (y_prompt, y_sample, new_state_lru, jnp.real(s5_state), jnp.imag(s5_state))
```

```python
import functools

import jax
import jax.numpy as jnp
from jax import lax
from jax.experimental import pallas as pl
from jax.experimental.pallas import tpu as pltpu

F32 = jnp.float32
BF16 = jnp.bfloat16

D = 1024
D_FF = 2816
N_MOD = 6
EPS = 1e-6
LRU_C = 8.0
NB = 8
STEPS = 64
TM = STEPS * NB
HALO = 16
P_LEN = 256
S_LEN = 4096
P_TILES_PER_SEQ = P_LEN // STEPS
N_P_TILES = 2 * P_TILES_PER_SEQ
N_S_TILES = S_LEN // STEPS
NT = N_P_TILES + N_S_TILES
N_ROWS = NT * TM
N_SEQ = 3
S5_KT = 4
SLAB = 2048
VMEM_LIMIT = 56 * 1024 * 1024


def _seq_id(j):
    return jnp.where(j >= N_P_TILES, 2, jnp.where(j >= P_TILES_PER_SEQ, 1, 0))


def _is_seq_first(j):
    return (j == 0) | (j == P_TILES_PER_SEQ) | (j == N_P_TILES)


def _is_seq_last(j):
    return (j == P_TILES_PER_SEQ - 1) | (j == N_P_TILES - 1) | (j == NT - 1)


def _params(sem):
    return pltpu.CompilerParams(dimension_semantics=sem, vmem_limit_bytes=VMEM_LIMIT)


def _const_spec(shape):
    n = len(shape)
    return pl.BlockSpec(shape, lambda *_: (0,) * n, pipeline_mode=pl.Buffered(1))


def _mod_spec(tile_of):
    return pl.BlockSpec((None, NB, N_MOD * D),
                        lambda i: (jnp.where(tile_of(i) >= N_P_TILES, 1, 0), 0, 0))


def _norm_mod(xv, g, scale1, shift):
    r = xv.shape[0]
    ms = jnp.mean(xv * xv, axis=-1, keepdims=True)
    y = xv * lax.rsqrt(ms + EPS) * g
    y = y.reshape(r // NB, NB, D) * scale1[None] + shift[None]
    return y.reshape(r, D)


def _fill_hbuf(hbuf, x_ref, xp_ref, xn_ref, g, scale1, shift):
    hbuf[0:HALO] = _norm_mod(xp_ref[...], g, scale1, shift).astype(BF16)
    hbuf[HALO + TM:HALO + TM + HALO] = _norm_mod(xn_ref[...], g, scale1, shift).astype(BF16)

    def body(k, c):
        r0 = pl.multiple_of(k * 16, 16)
        hbuf[pl.ds(HALO + r0, 16)] = _norm_mod(x_ref[pl.ds(r0, 16)], g, scale1, shift).astype(BF16)
        return c

    lax.fori_loop(0, TM // 16, body, 0)


def _to_rows_kernel(xp_ref, xs_ref, o_ref):
    tb = pl.program_id(1)

    @pl.when(tb < 2)
    def _():
        o_ref[...] = xp_ref[...]

    @pl.when(tb >= 2)
    def _():
        o_ref[...] = xs_ref[...]


def _to_rows(xp4, xs):
    nblk = 2 + S_LEN // P_LEN
    out = pl.pallas_call(
        _to_rows_kernel,
        out_shape=jax.ShapeDtypeStruct((nblk * P_LEN, NB * D), F32),
        grid=(NB, nblk),
        in_specs=[
            pl.BlockSpec((None, None, P_LEN, D), lambda b, tb: (jnp.minimum(tb, 1), b, 0, 0)),
            pl.BlockSpec((None, P_LEN, D), lambda b, tb: (b, jnp.maximum(tb - 2, 0), 0)),
        ],
        out_specs=pl.BlockSpec((P_LEN, D), lambda b, tb: (tb, b)),
        compiler_params=_params(("arbitrary", "arbitrary")),
        name="to_rows",
    )(xp4, xs)
    return out.reshape(N_ROWS, D)


def _final_kernel(x_ref, g_ref, yp_ref, ys_ref):
    tb = pl.program_id(1)
    xv = x_ref[...]
    ms = jnp.mean(xv * xv, axis=-1, keepdims=True)
    y = xv * lax.rsqrt(ms + EPS) * g_ref[...]

    @pl.when(tb < 2)
    def _():
        yp_ref[...] = y

    @pl.when(tb >= 2)
    def _():
        ys_ref[...] = y


def _final_norm(x_rows, g):
    nblk = 2 + S_LEN // P_LEN
    xv = x_rows.reshape(nblk * P_LEN, NB * D)
    return pl.pallas_call(
        _final_kernel,
        out_shape=(jax.ShapeDtypeStruct((2, NB, P_LEN, D), F32),
                   jax.ShapeDtypeStruct((NB, S_LEN, D), F32)),
        grid=(NB, nblk),
        in_specs=[
            pl.BlockSpec((P_LEN, D), lambda b, tb: (tb, b)),
            pl.BlockSpec((1, D), lambda b, tb: (0, 0)),
        ],
        out_specs=(
            pl.BlockSpec((None, None, P_LEN, D), lambda b, tb: (jnp.minimum(tb, 1), b, 0, 0)),
            pl.BlockSpec((None, P_LEN, D), lambda b, tb: (b, jnp.maximum(tb - 2, 0), 0)),
        ),
        compiler_params=_params(("arbitrary", "arbitrary")),
        name="final_norm",
    )(xv, g)


def _mod_kernel(c_ref, w_ref, b_ref, o_ref):
    cv = c_ref[...]
    s = (cv * jax.nn.sigmoid(cv)).astype(BF16)
    o_ref[...] = jnp.dot(s, w_ref[...], preferred_element_type=F32) + b_ref[...]


def _mod_vectors(c_all, ada_w, ada_b):
    depth = ada_w.shape[0]
    return pl.pallas_call(
        _mod_kernel,
        out_shape=jax.ShapeDtypeStruct((depth, 16, N_MOD * D), F32),
        grid=(depth, N_MOD),
        in_specs=[
            pl.BlockSpec((16, D), lambda l, n: (0, 0)),
            pl.BlockSpec((None, D, D), lambda l, n: (l, 0, n)),
            pl.BlockSpec((None, 1, D), lambda l, n: (l, 0, n)),
        ],
        out_specs=pl.BlockSpec((None, 16, D), lambda l, n: (l, 0, n)),
        compiler_params=_params(("arbitrary", "arbitrary")),
        name="mod_vectors",
    )(c_all, ada_w, ada_b)


def _halo_specs():
    prev = pl.BlockSpec((HALO, D), lambda i: (jnp.maximum(i * (TM // HALO) - 1, 0), 0))
    nxt = pl.BlockSpec((HALO, D), lambda i: (jnp.minimum((i + 1) * (TM // HALO), N_ROWS // HALO - 1), 0))
    return prev, nxt


def _lru_in_kernel(x_ref, xp_ref, xn_ref, mod_ref, g_ref, wg_ref, wx_ref, cw_ref, cb_ref,
                   gate_ref, xr_ref, hbuf, xpre):
    i = pl.program_id(0)
    shift = mod_ref[:, 0:D]
    scale1 = 1.0 + mod_ref[:, D:2 * D]
    _fill_hbuf(hbuf, x_ref, xp_ref, xn_ref, g_ref[...], scale1, shift)

    gate_ref[...] = jnp.dot(hbuf[HALO:HALO + TM], wg_ref[...], preferred_element_type=F32)
    xpre[...] = jnp.dot(hbuf[...], wx_ref[...], preferred_element_type=F32)

    @pl.when(_is_seq_first(i))
    def _():
        xpre[0:HALO] = jnp.zeros((HALO, D), F32)

    @pl.when(_is_seq_last(i))
    def _():
        xpre[HALO + TM:HALO + TM + HALO] = jnp.zeros((HALO, D), F32)

    cb = cb_ref[...]
    w0 = cw_ref[0:1]
    w1 = cw_ref[1:2]
    w2 = cw_ref[2:3]
    w3 = cw_ref[3:4]

    def body(k, c):
        r0 = pl.multiple_of(k * 16, 16)
        acc = cb + xpre[pl.ds(r0, 16)] * w0
        acc = acc + xpre[pl.ds(r0 + 8, 16)] * w1
        acc = acc + xpre[pl.ds(r0 + 16, 16)] * w2
        acc = acc + xpre[pl.ds(r0 + 24, 16)] * w3
        xr_ref[pl.ds(r0, 16)] = acc
        return c

    lax.fori_loop(0, TM // 16, body, 0)


def _lru_in(x, mod, g, wg, wx, cw, cb):
    prev, nxt = _halo_specs()
    tile = pl.BlockSpec((TM, D), lambda i: (i, 0))
    return pl.pallas_call(
        _lru_in_kernel,
        out_shape=(jax.ShapeDtypeStruct((N_ROWS, D), F32), jax.ShapeDtypeStruct((N_ROWS, D), F32)),
        grid=(NT,),
        in_specs=[tile, prev, nxt, _mod_spec(lambda i: i), _const_spec((1, D)),
                  _const_spec((D, D)), _const_spec((D, D)), _const_spec((4, D)), _const_spec((1, D))],
        out_specs=(tile, tile),
        scratch_shapes=[pltpu.VMEM((TM + 2 * HALO, D), BF16), pltpu.VMEM((TM + 2 * HALO, D), F32)],
        compiler_params=_params(("arbitrary",)),
        name="lru_in",
    )(x, x, x, mod, g, wg, wx, cw, cb)


def _lru_gates(xr_ref, wp_ref, ba_ref, bi_ref, sp_ref, a_s, bx_s, xb_s, z_s):
    xb_s[...] = xr_ref[...].astype(BF16)
    for p in range(8):
        z_s[:, 256 * p:256 * (p + 1)] = jnp.dot(xb_s[:, 128 * p:128 * (p + 1)], wp_ref[p],
                                                preferred_element_type=F32)

    def body(k, c):
        r0 = pl.multiple_of(k * 8, 8)
        rows = pl.ds(r0, 8)
        for p in range(8):
            cols = slice(128 * p, 128 * (p + 1))
            r = jax.nn.sigmoid(z_s[rows, 256 * p:256 * p + 128] + ba_ref[:, cols])
            ig = jax.nn.sigmoid(z_s[rows, 256 * p + 128:256 * (p + 1)] + bi_ref[:, cols])
            log_a = (-LRU_C) * r * sp_ref[:, cols]
            a = jnp.exp(log_a)
            a_s[rows, cols] = a
            bx_s[rows, cols] = jnp.sqrt(1.0 - a * a) * (ig * xr_ref[rows, cols])
        return c

    lax.fori_loop(0, TM // 8, body, 0)


def _lru_scan(a_s, bx_s, h_s, reverse):
    def body(s, h):
        t = (STEPS - 1 - s) if reverse else s
        rows = pl.ds(pl.multiple_of(t * 8, 8), 8)
        h = a_s[rows] * h + bx_s[rows]
        bx_s[rows] = h
        return h

    h = lax.fori_loop(0, STEPS, body, h_s[...], unroll=4)
    h_s[...] = h


def _lru_bwd_kernel(xr_ref, wp_ref, ba_ref, bi_ref, sp_ref, h0_ref,
                    hs_ref, fin_ref, a_s, bx_s, xb_s, z_s, h_s):
    j = NT - 1 - pl.program_id(0)

    @pl.when(_is_seq_last(j))
    def _():
        h_s[...] = h0_ref[...]

    _lru_gates(xr_ref, wp_ref, ba_ref, bi_ref, sp_ref, a_s, bx_s, xb_s, z_s)
    _lru_scan(a_s, bx_s, h_s, True)
    hs_ref[...] = bx_s[...]
    fin_ref[...] = h_s[...]


def _lru_bwd(xr, wp, ba, bi, sp, h0):
    tile = pl.BlockSpec((TM, D), lambda i: (NT - 1 - i, 0))
    seq = pl.BlockSpec((None, NB, D), lambda i: (_seq_id(NT - 1 - i), 0, 0))
    return pl.pallas_call(
        _lru_bwd_kernel,
        out_shape=(jax.ShapeDtypeStruct((N_ROWS, D), F32), jax.ShapeDtypeStruct((N_SEQ, NB, D), F32)),
        grid=(NT,),
        in_specs=[tile, _const_spec((8, 128, 256)), _const_spec((1, D)), _const_spec((1, D)),
                  _const_spec((1, D)), seq],
        out_specs=(tile, seq),
        scratch_shapes=[pltpu.VMEM((TM, D), F32), pltpu.VMEM((TM, D), F32), pltpu.VMEM((TM, D), BF16),
                        pltpu.VMEM((TM, 2 * D), F32), pltpu.VMEM((NB, D), F32)],
        compiler_params=_params(("arbitrary",)),
        name="lru_bwd",
    )(xr, wp, ba, bi, sp, h0)


def _lru_fwd_kernel(xr_ref, gate_ref, hsb_ref, x_ref, mod_ref, wp_ref, ba_ref, bi_ref, sp_ref, h0_ref,
                    wo_ref, x1_ref, fin_ref, a_s, bx_s, xb_s, z_s, h_s):
    j = pl.program_id(0)

    @pl.when(_is_seq_first(j))
    def _():
        h_s[...] = h0_ref[...]

    _lru_gates(xr_ref, wp_ref, ba_ref, bi_ref, sp_ref, a_s, bx_s, xb_s, z_s)
    _lru_scan(a_s, bx_s, h_s, False)
    fin_ref[...] = h_s[...]

    def body(k, c):
        rows = pl.ds(pl.multiple_of(k * 16, 16), 16)
        y = (bx_s[rows] + hsb_ref[rows]) * jax.nn.gelu(gate_ref[rows])
        xb_s[rows] = y.astype(BF16)
        return c

    lax.fori_loop(0, TM // 16, body, 0)
    out = jnp.dot(xb_s[...], wo_ref[...], preferred_element_type=F32)
    gt1 = mod_ref[:, 2 * D:3 * D]
    x1_ref[...] = x_ref[...] + (out.reshape(STEPS, NB, D) * gt1[None]).reshape(TM, D)


def _lru_fwd(xr, gate, hsb, x, mod, wp, ba, bi, sp, h0, wo):
    tile = pl.BlockSpec((TM, D), lambda i: (i, 0))
    seq = pl.BlockSpec((None, NB, D), lambda i: (_seq_id(i), 0, 0))
    return pl.pallas_call(
        _lru_fwd_kernel,
        out_shape=(jax.ShapeDtypeStruct((N_ROWS, D), F32), jax.ShapeDtypeStruct((N_SEQ, NB, D), F32)),
        grid=(NT,),
        in_specs=[tile, tile, tile, tile, _mod_spec(lambda i: i), _const_spec((8, 128, 256)),
                  _const_spec((1, D)), _const_spec((1, D)), _const_spec((1, D)), seq, _const_spec((D, D))],
        out_specs=(tile, seq),
        scratch_shapes=[pltpu.VMEM((TM, D), F32), pltpu.VMEM((TM, D), F32), pltpu.VMEM((TM, D), BF16),
                        pltpu.VMEM((TM, 2 * D), F32), pltpu.VMEM((NB, D), F32)],
        compiler_params=_params(("arbitrary",)),
        name="lru_fwd",
    )(xr, gate, hsb, x, mod, wp, ba, bi, sp, h0, wo)


def _ffn_kernel(x_ref, xp_ref, xn_ref, mod_ref, g_ref, wu_ref, cw_ref, cb_ref, wd_ref,
                o_ref, hbuf, up, act):
    i = pl.program_id(0)
    shift = mod_ref[:, 3 * D:4 * D]
    scale1 = 1.0 + mod_ref[:, 4 * D:5 * D]
    _fill_hbuf(hbuf, x_ref, xp_ref, xn_ref, g_ref[...], scale1, shift)

    up[...] = jnp.dot(hbuf[...], wu_ref[...], preferred_element_type=F32)

    is_prompt = i < N_P_TILES
    seg_first = jnp.logical_not(is_prompt & ((i % P_TILES_PER_SEQ) != 0))
    seg_last = jnp.logical_not(is_prompt & ((i % P_TILES_PER_SEQ) != P_TILES_PER_SEQ - 1))

    @pl.when(seg_first)
    def _():
        up[0:HALO] = jnp.zeros((HALO, 2 * D_FF), F32)

    @pl.when(seg_last)
    def _():
        up[HALO + TM:HALO + TM + HALO] = jnp.zeros((HALO, 2 * D_FF), F32)

    def body(k, c):
        r0 = pl.multiple_of(k * 16, 16)
        for q in range(D_FF // 128):
            cv = slice(128 * q, 128 * (q + 1))
            cg = slice(D_FF + 128 * q, D_FF + 128 * (q + 1))
            v = cb_ref[:, cv] + up[pl.ds(r0 + 8, 16), cv] * cw_ref[0:1, cv]
            v = v + up[pl.ds(r0 + 16, 16), cv] * cw_ref[1:2, cv]
            v = v + up[pl.ds(r0 + 24, 16), cv] * cw_ref[2:3, cv]
            gg = cb_ref[:, cg] + up[pl.ds(r0 + 8, 16), cg] * cw_ref[0:1, cg]
            gg = gg + up[pl.ds(r0 + 16, 16), cg] * cw_ref[1:2, cg]
            gg = gg + up[pl.ds(r0 + 24, 16), cg] * cw_ref[2:3, cg]
            act[pl.ds(r0, 16), cv] = (v * (gg * jax.nn.sigmoid(gg))).astype(BF16)
        return c

    lax.fori_loop(0, TM // 16, body, 0)

    out = jnp.dot(act[...], wd_ref[...], preferred_element_type=F32)
    gt2 = mod_ref[:, 5 * D:6 * D]
    o_ref[...] = x_ref[...] + (out.reshape(STEPS, NB, D) * gt2[None]).reshape(TM, D)


def _ffn(x, mod, g, wu, cw, cb, wd):
    prev, nxt = _halo_specs()
    tile = pl.BlockSpec((TM, D), lambda i: (i, 0))
    return pl.pallas_call(
        _ffn_kernel,
        out_shape=jax.ShapeDtypeStruct((N_ROWS, D), F32),
        grid=(NT,),
        in_specs=[tile, prev, nxt, _mod_spec(lambda i: i), _const_spec((1, D)),
                  _const_spec((D, 2 * D_FF)), _const_spec((3, 2 * D_FF)), _const_spec((1, 2 * D_FF)),
                  _const_spec((D_FF, D))],
        out_specs=tile,
        scratch_shapes=[pltpu.VMEM((TM + 2 * HALO, D), BF16), pltpu.VMEM((TM + 2 * HALO, 2 * D_FF), F32),
                        pltpu.VMEM((TM, D_FF), BF16)],
        compiler_params=_params(("arbitrary",)),
        name="conv_ffn",
    )(x, x, x, mod, g, wu, cw, cb, wd)


def _s5_core(ubuf, bb_ref, cc_ref, ab_ref, slab, hst, y_write, reverse):
    for kt in range(S5_KT):
        slab[...] = jnp.dot(ubuf[:, 256 * kt:256 * (kt + 1)], bb_ref[kt], preferred_element_type=F32)
        a_re = jnp.broadcast_to(ab_ref[kt, 0:1, :], (NB, D))
        a_im = jnp.broadcast_to(ab_ref[kt, 1:2, :], (NB, D))

        def body(s, carry):
            h_re, h_im = carry
            t = (STEPS - 1 - s) if reverse else s
            rows = pl.ds(pl.multiple_of(t * 8, 8), 8)
            n_re = a_re * h_re - a_im * h_im + slab[rows, 0:D]
            n_im = a_re * h_im + a_im * h_re + slab[rows, D:2 * D]
            slab[rows, 0:D] = n_re
            slab[rows, D:2 * D] = n_im
            return n_re, n_im

        h_re, h_im = lax.fori_loop(0, STEPS, body, (hst[kt, :, 0:D], hst[kt, :, D:2 * D]), unroll=2)
        hst[kt, :, 0:D] = h_re
        hst[kt, :, D:2 * D] = h_im
        y_write(kt, jnp.dot(slab[...].astype(BF16), cc_ref[kt], preferred_element_type=F32))


def _s5_fill_u(ubuf, x_ref, mod_ref, g_ref, u_s=None):
    shift = mod_ref[:, 0:D]
    scale1 = 1.0 + mod_ref[:, D:2 * D]
    g = g_ref[...]

    def body(k, c):
        rows = pl.ds(pl.multiple_of(k * 16, 16), 16)
        u = _norm_mod(x_ref[rows], g, scale1, shift)
        ubuf[rows] = u.astype(BF16)
        if u_s is not None:
            u_s[rows] = u
        return c

    lax.fori_loop(0, TM // 16, body, 0)


def _s5_bwd_kernel(x_ref, mod_ref, g_ref, bb_ref, cc_ref, ab_ref, h0_ref,
                   yb_ref, fin_ref, ubuf, slab, hst):
    j = NT - 1 - pl.program_id(0)

    @pl.when(_is_seq_last(j))
    def _():
        hst[...] = h0_ref[...]

    _s5_fill_u(ubuf, x_ref, mod_ref, g_ref)

    def y_write(kt, val):
        yb_ref[:, 256 * kt:256 * (kt + 1)] = val

    _s5_core(ubuf, bb_ref, cc_ref, ab_ref, slab, hst, y_write, True)
    fin_ref[...] = hst[...]


def _s5_bwd(x, mod, g, bb, cc, ab, h0):
    tile = pl.BlockSpec((TM, D), lambda i: (NT - 1 - i, 0))
    seq = pl.BlockSpec((None, S5_KT, NB, SLAB), lambda i: (_seq_id(NT - 1 - i), 0, 0, 0))
    return pl.pallas_call(
        _s5_bwd_kernel,
        out_shape=(jax.ShapeDtypeStruct((N_ROWS, D), F32),
                   jax.ShapeDtypeStruct((N_SEQ, S5_KT, NB, SLAB), F32)),
        grid=(NT,),
        in_specs=[tile, _mod_spec(lambda i: NT - 1 - i), _const_spec((1, D)),
                  _const_spec((S5_KT, 256, SLAB)), _const_spec((S5_KT, SLAB, 256)),
                  _const_spec((S5_KT, 2, D)), seq],
        out_specs=(tile, seq),
        scratch_shapes=[pltpu.VMEM((TM, D), BF16), pltpu.VMEM((TM, SLAB), F32),
                        pltpu.VMEM((S5_KT, NB, SLAB), F32)],
        compiler_params=_params(("arbitrary",)),
        name="s5_bwd",
    )(x, mod, g, bb, cc, ab, h0)


def _s5_fwd_kernel(x_ref, yb_ref, mod_ref, g_ref, bb_ref, cc_ref, ab_ref, h0_ref, dsk_ref, wglu_ref,
                   x1_ref, fin_ref, ubuf, u_s, y_s, slab, hst):
    j = pl.program_id(0)

    @pl.when(_is_seq_first(j))
    def _():
        hst[...] = h0_ref[...]

    _s5_fill_u(ubuf, x_ref, mod_ref, g_ref, u_s)

    def y_write(kt, val):
        y_s[:, 256 * kt:256 * (kt + 1)] = val

    _s5_core(ubuf, bb_ref, cc_ref, ab_ref, slab, hst, y_write, False)
    fin_ref[...] = hst[...]

    dsk = dsk_ref[...]

    def body(k, c):
        rows = pl.ds(pl.multiple_of(k * 16, 16), 16)
        y = dsk * u_s[rows] + yb_ref[rows] + y_s[rows]
        ubuf[rows] = jax.nn.gelu(y).astype(BF16)
        return c

    lax.fori_loop(0, TM // 16, body, 0)
    slab[...] = jnp.dot(ubuf[...], wglu_ref[...], preferred_element_type=F32)
    gt1 = mod_ref[:, 2 * D:3 * D]

    def body2(k, c):
        rows = pl.ds(pl.multiple_of(k * 8, 8), 8)
        out = slab[rows, 0:D] * jax.nn.sigmoid(slab[rows, D:2 * D])
        x1_ref[rows] = x_ref[rows] + out * gt1
        return c

    lax.fori_loop(0, TM // 8, body2, 0)


def _s5_fwd(x, yb, mod, g, bb, cc, ab, h0, dsk, wglu):
    tile = pl.BlockSpec((TM, D), lambda i: (i, 0))
    seq = pl.BlockSpec((None, S5_KT, NB, SLAB), lambda i: (_seq_id(i), 0, 0, 0))
    return pl.pallas_call(
        _s5_fwd_kernel,
        out_shape=(jax.ShapeDtypeStruct((N_ROWS, D), F32),
                   jax.ShapeDtypeStruct((N_SEQ, S5_KT, NB, SLAB), F32)),
        grid=(NT,),
        in_specs=[tile, tile, _mod_spec(lambda i: i), _const_spec((1, D)),
                  _const_spec((S5_KT, 256, SLAB)), _const_spec((S5_KT, SLAB, 256)),
                  _const_spec((S5_KT, 2, D)), seq, _const_spec((1, D)), _const_spec((D, 2 * D))],
        out_specs=(tile, seq),
        scratch_shapes=[pltpu.VMEM((TM, D), BF16), pltpu.VMEM((TM, D), F32), pltpu.VMEM((TM, D), F32),
                        pltpu.VMEM((TM, SLAB), F32), pltpu.VMEM((S5_KT, NB, SLAB), F32)],
        compiler_params=_params(("arbitrary",)),
        name="s5_fwd",
    )(x, yb, mod, g, bb, cc, ab, h0, dsk, wglu)


def _pair_blockdiag(w_a, w_i):
    eye2 = jnp.eye(2, dtype=F32)

    def bd(w):
        w4 = w.astype(F32).reshape(8, 2, 64, 64)
        return jnp.einsum('phij,hk->phikj', w4, eye2).reshape(8, 128, 128)

    return jnp.concatenate([bd(w_a), bd(w_i)], axis=-1).astype(BF16)


def _s5_dir_params(a_re, a_im, log_dt, b_re, b_im, c_re, c_im):
    lam = lax.complex(a_re.astype(F32), a_im.astype(F32))
    dt = jnp.exp(log_dt.astype(F32))[:, None]
    a_bar = jnp.exp(lam * dt)
    b_bar = lax.complex(b_re.astype(F32), b_im.astype(F32)) * ((a_bar - 1.0) / lam)[..., None]
    eye = jnp.eye(16, dtype=F32)

    def b_blk(m):
        m4 = m.reshape(S5_KT, 16, 64, 16)
        return jnp.einsum('kgpc,gh->kgchp', m4, eye).reshape(S5_KT, 256, 1024)

    def c_blk(m):
        m4 = m.reshape(S5_KT, 16, 16, 64)
        return jnp.einsum('kgcp,gh->kgphc', m4, eye).reshape(S5_KT, 1024, 256)

    bb = jnp.concatenate([b_blk(jnp.real(b_bar)), b_blk(jnp.imag(b_bar))], axis=-1).astype(BF16)
    cc = jnp.concatenate([c_blk(c_re.astype(F32)), c_blk(-c_im.astype(F32))], axis=1).astype(BF16)
    ab = jnp.stack([jnp.real(a_bar).reshape(S5_KT, D), jnp.imag(a_bar).reshape(S5_KT, D)], axis=1)
    return bb, cc, ab


def _s5_state_to_slab(s_re, s_im):
    re = s_re.astype(F32).reshape(NB, S5_KT, D).transpose(1, 0, 2)
    im = s_im.astype(F32).reshape(NB, S5_KT, D).transpose(1, 0, 2)
    return jnp.concatenate([re, im], axis=-1)


def _s5_slab_to_state(fin):
    re = fin[..., 0:D].transpose(0, 2, 1, 3).reshape(2 * NB, 64, 64)
    im = fin[..., D:2 * D].transpose(0, 2, 1, 3).reshape(2 * NB, 64, 64)
    return re, im


def kernel(x_prompt, x_sample, state_lru, state_s5_re, state_s5_im, c, c_ctx, ada_w, ada_b, norm_mix, norm_ffn, norm_final, lru_w_in, lru_conv_w, lru_conv_b, lru_w_a, lru_b_a, lru_w_i, lru_b_i, lru_lambda, lru_w_out, s5_a_re, s5_a_im, s5_log_dt, s5_b_re, s5_b_im, s5_c_re, s5_c_im, s5_d, s5_w_glu, ffn_w_up, ffn_conv_w, ffn_conv_b, ffn_w_down):
    c_all = jnp.concatenate([c.astype(F32), c_ctx.astype(F32)[None], jnp.zeros((7, D), F32)], axis=0)
    mods = _mod_vectors(c_all, ada_w.astype(BF16), ada_b.astype(F32)[:, None, :])

    def mod_of(l):
        return jnp.stack([jnp.broadcast_to(mods[l, 8][None], (NB, N_MOD * D)), mods[l, 0:NB]], axis=0)

    x = _to_rows(x_prompt.astype(F32).reshape(2, NB, P_LEN, D), x_sample.astype(F32))

    mod0 = mod_of(0)
    w_in = lru_w_in[0].astype(BF16)
    gate, xr = _lru_in(x, mod0, norm_mix[0][None].astype(F32), w_in[:, 0:D], w_in[:, D:2 * D],
                       lru_conv_w[0].astype(F32), lru_conv_b[0][None].astype(F32))
    zeros_h = jnp.zeros((NB, D), F32)
    sp = jax.nn.softplus(-lru_lambda[0].astype(F32))
    h0_f = jnp.stack([zeros_h, zeros_h, state_lru[:, 0, 0].astype(F32)], axis=0)
    h0_b = jnp.stack([zeros_h, zeros_h, state_lru[:, 0, 1].astype(F32)], axis=0)
    hsb, fin_b = _lru_bwd(xr, _pair_blockdiag(lru_w_a[0, 1], lru_w_i[0, 1]),
                          lru_b_a[0, 1][None].astype(F32), lru_b_i[0, 1][None].astype(F32), sp[1][None], h0_b)
    x, fin_f = _lru_fwd(xr, gate, hsb, x, mod0, _pair_blockdiag(lru_w_a[0, 0], lru_w_i[0, 0]),
                        lru_b_a[0, 0][None].astype(F32), lru_b_i[0, 0][None].astype(F32), sp[0][None], h0_f,
                        lru_w_out[0].astype(BF16))
    x = _ffn(x, mod0, norm_ffn[0][None].astype(F32), ffn_w_up[0].astype(BF16), ffn_conv_w[0].astype(F32),
             ffn_conv_b[0][None].astype(F32), ffn_w_down[0].astype(BF16))
    new_lru = jnp.stack([fin_f[0:2].reshape(2 * NB, D), fin_b[0:2].reshape(2 * NB, D)], axis=1)[:, None]

    mod1 = mod_of(1)
    g1 = norm_mix[1][None].astype(F32)
    zeros_s = jnp.zeros((S5_KT, NB, SLAB), F32)
    dirs = []
    for d in range(2):
        bb, cc, ab = _s5_dir_params(s5_a_re[0, d], s5_a_im[0, d], s5_log_dt[0, d], s5_b_re[0, d], s5_b_im[0, d],
                                    s5_c_re[0, d], s5_c_im[0, d])
        h0 = jnp.stack([zeros_s, zeros_s, _s5_state_to_slab(state_s5_re[:, 0, d], state_s5_im[:, 0, d])], axis=0)
        dirs.append((bb, cc, ab, h0))
    yb, fin_sb = _s5_bwd(x, mod1, g1, *dirs[1])
    x, fin_sf = _s5_fwd(x, yb, mod1, g1, *dirs[0], s5_d[0][None].astype(F32), s5_w_glu[0].astype(BF16))
    x = _ffn(x, mod1, norm_ffn[1][None].astype(F32), ffn_w_up[1].astype(BF16), ffn_conv_w[1].astype(F32),
             ffn_conv_b[1][None].astype(F32), ffn_w_down[1].astype(BF16))
    f_re, f_im = _s5_slab_to_state(fin_sf[0:2])
    b_re, b_im = _s5_slab_to_state(fin_sb[0:2])
    new_s5_re = jnp.stack([f_re, b_re], axis=1)[:, None]
    new_s5_im = jnp.stack([f_im, b_im], axis=1)[:, None]

    y_p4, y_sample = _final_norm(x, norm_final[None].astype(F32))
    return (y_p4.reshape(2 * NB, P_LEN, D), y_sample, new_lru, new_s5_re, new_s5_im)
```

```python
import functools

import jax
import jax.numpy as jnp
from jax import lax
from jax.experimental import pallas as pl
from jax.experimental.pallas import tpu as pltpu

F32 = jnp.float32
BF16 = jnp.bfloat16

D = 1024
D_FF = 2816
N_MOD = 6
EPS = 1e-6
LRU_C = 8.0
NB = 8
STEPS = 64
TM = STEPS * NB
HALO = 16
NORM_ROWS = 64
FF_CHUNK = 256
COL_CHUNK = 256
P_LEN = 256
S_LEN = 4096
P_TILES_PER_SEQ = P_LEN // STEPS
N_P_TILES = 2 * P_TILES_PER_SEQ
N_S_TILES = S_LEN // STEPS
NT = N_P_TILES + N_S_TILES
N_ROWS = NT * TM
N_SEQ = 3
S5_KT = 4
SLAB = 2048
VMEM_LIMIT = 56 * 1024 * 1024


def _seq_id(j):
    return jnp.where(j >= N_P_TILES, 2, jnp.where(j >= P_TILES_PER_SEQ, 1, 0))


def _is_seq_first(j):
    return (j == 0) | (j == P_TILES_PER_SEQ) | (j == N_P_TILES)


def _is_seq_last(j):
    return (j == P_TILES_PER_SEQ - 1) | (j == N_P_TILES - 1) | (j == NT - 1)


def _params(sem):
    return pltpu.CompilerParams(dimension_semantics=sem, vmem_limit_bytes=VMEM_LIMIT)


def _const_spec(shape):
    n = len(shape)
    return pl.BlockSpec(shape, lambda *_: (0,) * n, pipeline_mode=pl.Buffered(1))


def _mod_spec(tile_of):
    return pl.BlockSpec((None, NB, N_MOD * D),
                        lambda i: (jnp.where(tile_of(i) >= N_P_TILES, 1, 0), 0, 0))


def _rms(xv, g):
    ms = jnp.mean(xv * xv, axis=-1, keepdims=True)
    return xv * lax.rsqrt(ms + EPS) * g


def _per_batch(y, vec):
    r, n = y.shape
    return (y.reshape(r // NB, NB, n) * vec[None]).reshape(r, n)


def _norm_mod(xv, g, scale1, shift):
    r = xv.shape[0]
    y = _rms(xv, g).reshape(r // NB, NB, D) * scale1[None] + shift[None]
    return y.reshape(r, D)


def _fill_norm(dst, dst_off, x_ref, g, scale1, shift, f32_dst=None):
    def body(k, c):
        r0 = pl.multiple_of(k * NORM_ROWS, NORM_ROWS)
        y = _norm_mod(x_ref[pl.ds(r0, NORM_ROWS)], g, scale1, shift)
        dst[pl.ds(dst_off + r0, NORM_ROWS)] = y.astype(BF16)
        if f32_dst is not None:
            f32_dst[pl.ds(r0, NORM_ROWS)] = y
        return c

    lax.fori_loop(0, TM // NORM_ROWS, body, 0, unroll=2)


def _fill_hbuf(hbuf, x_ref, xp_ref, xn_ref, g, scale1, shift):
    hbuf[0:HALO] = _norm_mod(xp_ref[...], g, scale1, shift).astype(BF16)
    hbuf[HALO + TM:HALO + TM + HALO] = _norm_mod(xn_ref[...], g, scale1, shift).astype(BF16)
    _fill_norm(hbuf, HALO, x_ref, g, scale1, shift)


def _mod_kernel(c_ref, w_ref, b_ref, o_ref):
    cv = c_ref[...]
    s = (cv * jax.nn.sigmoid(cv)).astype(BF16)
    o_ref[...] = jnp.dot(s, w_ref[...], preferred_element_type=F32) + b_ref[...]


def _mod_vectors(c_all, ada_w, ada_b):
    depth = ada_w.shape[0]
    return pl.pallas_call(
        _mod_kernel,
        out_shape=jax.ShapeDtypeStruct((depth, 16, N_MOD * D), F32),
        grid=(depth, N_MOD),
        in_specs=[
            pl.BlockSpec((16, D), lambda l, n: (0, 0)),
            pl.BlockSpec((None, D, D), lambda l, n: (l, 0, n)),
            pl.BlockSpec((None, 1, D), lambda l, n: (l, 0, n)),
        ],
        out_specs=pl.BlockSpec((None, 16, D), lambda l, n: (l, 0, n)),
        compiler_params=_params(("arbitrary", "arbitrary")),
        name="mod_vectors",
    )(c_all, ada_w, ada_b)


def _halo_specs():
    prev = pl.BlockSpec((HALO, D), lambda i: (jnp.maximum(i * (TM // HALO) - 1, 0), 0))
    nxt = pl.BlockSpec((HALO, D), lambda i: (jnp.minimum((i + 1) * (TM // HALO), N_ROWS // HALO - 1), 0))
    return prev, nxt


def _lru_in_kernel(x_ref, xp_ref, xn_ref, mod_ref, g_ref, wg_ref, wx_ref, cw_ref, cb_ref,
                   gate_ref, xr_ref, hbuf):
    i = pl.program_id(0)
    shift = mod_ref[:, 0:D]
    scale1 = 1.0 + mod_ref[:, D:2 * D]
    _fill_hbuf(hbuf, x_ref, xp_ref, xn_ref, g_ref[...], scale1, shift)
    first = _is_seq_first(i)
    last = _is_seq_last(i)

    for c in range(D // COL_CHUNK):
        cs = slice(COL_CHUNK * c, COL_CHUNK * (c + 1))
        gate_ref[:, cs] = jnp.dot(hbuf[HALO:HALO + TM], wg_ref[:, cs], preferred_element_type=F32)
        u = jnp.dot(hbuf[...], wx_ref[:, cs], preferred_element_type=F32)
        head = jnp.where(first, 0.0, u[0:16])
        tail = jnp.where(last, 0.0, u[TM + 16:TM + 24])
        m2 = jnp.concatenate([head, u[16:TM]], axis=0)
        m1 = jnp.concatenate([head[8:16], u[16:TM + 8]], axis=0)
        p1 = jnp.concatenate([u[24:TM + 16], tail], axis=0)
        xr_ref[:, cs] = (cb_ref[:, cs] + m2 * cw_ref[0:1, cs] + m1 * cw_ref[1:2, cs]
                         + u[16:TM + 16] * cw_ref[2:3, cs] + p1 * cw_ref[3:4, cs])


def _lru_in(x, mod, g, wg, wx, cw, cb):
    prev, nxt = _halo_specs()
    tile = pl.BlockSpec((TM, D), lambda i: (i, 0))
    return pl.pallas_call(
        _lru_in_kernel,
        out_shape=(jax.ShapeDtypeStruct((N_ROWS, D), F32), jax.ShapeDtypeStruct((N_ROWS, D), F32)),
        grid=(NT,),
        in_specs=[tile, prev, nxt, _mod_spec(lambda i: i), _const_spec((1, D)),
                  _const_spec((D, D)), _const_spec((D, D)), _const_spec((4, D)), _const_spec((1, D))],
        out_specs=(tile, tile),
        scratch_shapes=[pltpu.VMEM((TM + 2 * HALO, D), BF16)],
        compiler_params=_params(("arbitrary",)),
        name="lru_in",
    )(x, x, x, mod, g, wg, wx, cw, cb)


def _lru_gates(xr_ref, wp_ref, ba_ref, bi_ref, sp_ref, a_s, bx_s, xb_s):
    xb_s[...] = xr_ref[...].astype(BF16)
    for p in range(8):
        cols = slice(128 * p, 128 * (p + 1))
        z = jnp.dot(xb_s[:, cols], wp_ref[p], preferred_element_type=F32)
        r = jax.nn.sigmoid(z[:, 0:128] + ba_ref[:, cols])
        ig = jax.nn.sigmoid(z[:, 128:256] + bi_ref[:, cols])
        a = jnp.exp((-LRU_C) * r * sp_ref[:, cols])
        a_s[:, cols] = a
        bx_s[:, cols] = jnp.sqrt(1.0 - a * a) * (ig * xr_ref[:, cols])


def _lru_scan(a_s, bx_s, h_s, out_ref, reverse):
    h = h_s[...]
    for s in range(STEPS):
        t = (STEPS - 1 - s) if reverse else s
        rows = slice(8 * t, 8 * (t + 1))
        h = a_s[rows] * h + bx_s[rows]
        out_ref[rows] = h
    h_s[...] = h


def _lru_bwd_kernel(xr_ref, wp_ref, ba_ref, bi_ref, sp_ref, h0_ref,
                    hs_ref, fin_ref, a_s, bx_s, xb_s, h_s):
    j = NT - 1 - pl.program_id(0)

    @pl.when(_is_seq_last(j))
    def _():
        h_s[...] = h0_ref[...]

    _lru_gates(xr_ref, wp_ref, ba_ref, bi_ref, sp_ref, a_s, bx_s, xb_s)
    _lru_scan(a_s, bx_s, h_s, hs_ref, True)
    fin_ref[...] = h_s[...]


def _lru_scratch():
    return [pltpu.VMEM((TM, D), F32), pltpu.VMEM((TM, D), F32), pltpu.VMEM((TM, D), BF16),
            pltpu.VMEM((NB, D), F32)]


def _lru_bwd(xr, wp, ba, bi, sp, h0):
    tile = pl.BlockSpec((TM, D), lambda i: (NT - 1 - i, 0))
    seq = pl.BlockSpec((None, NB, D), lambda i: (_seq_id(NT - 1 - i), 0, 0))
    return pl.pallas_call(
        _lru_bwd_kernel,
        out_shape=(jax.ShapeDtypeStruct((N_ROWS, D), F32), jax.ShapeDtypeStruct((N_SEQ, NB, D), F32)),
        grid=(NT,),
        in_specs=[tile, _const_spec((8, 128, 256)), _const_spec((1, D)), _const_spec((1, D)),
                  _const_spec((1, D)), seq],
        out_specs=(tile, seq),
        scratch_shapes=_lru_scratch(),
        compiler_params=_params(("arbitrary",)),
        name="lru_bwd",
    )(xr, wp, ba, bi, sp, h0)


def _lru_fwd_kernel(xr_ref, gate_ref, hsb_ref, x_ref, mod_ref, wp_ref, ba_ref, bi_ref, sp_ref, h0_ref,
                    wo_ref, x1_ref, fin_ref, a_s, bx_s, xb_s, h_s):
    j = pl.program_id(0)

    @pl.when(_is_seq_first(j))
    def _():
        h_s[...] = h0_ref[...]

    _lru_gates(xr_ref, wp_ref, ba_ref, bi_ref, sp_ref, a_s, bx_s, xb_s)
    _lru_scan(a_s, bx_s, h_s, bx_s, False)
    fin_ref[...] = h_s[...]

    xb_s[...] = ((bx_s[...] + hsb_ref[...]) * jax.nn.gelu(gate_ref[...])).astype(BF16)
    for c in range(D // COL_CHUNK):
        cs = slice(COL_CHUNK * c, COL_CHUNK * (c + 1))
        out = jnp.dot(xb_s[...], wo_ref[:, cs], preferred_element_type=F32)
        x1_ref[:, cs] = x_ref[:, cs] + _per_batch(out, mod_ref[:, 2 * D + COL_CHUNK * c:2 * D + COL_CHUNK * (c + 1)])


def _lru_fwd(xr, gate, hsb, x, mod, wp, ba, bi, sp, h0, wo):
    tile = pl.BlockSpec((TM, D), lambda i: (i, 0))
    seq = pl.BlockSpec((None, NB, D), lambda i: (_seq_id(i), 0, 0))
    return pl.pallas_call(
        _lru_fwd_kernel,
        out_shape=(jax.ShapeDtypeStruct((N_ROWS, D), F32), jax.ShapeDtypeStruct((N_SEQ, NB, D), F32)),
        grid=(NT,),
        in_specs=[tile, tile, tile, tile, _mod_spec(lambda i: i), _const_spec((8, 128, 256)),
                  _const_spec((1, D)), _const_spec((1, D)), _const_spec((1, D)), seq, _const_spec((D, D))],
        out_specs=(tile, seq),
        scratch_shapes=_lru_scratch(),
        compiler_params=_params(("arbitrary",)),
        name="lru_fwd",
    )(xr, gate, hsb, x, mod, wp, ba, bi, sp, h0, wo)


def _ffn_kernel(x_ref, xp_ref, xn_ref, mod_ref, g_ref, wu_ref, cw_ref, cb_ref, wd_ref, gf_ref,
                o_ref, hbuf, act, *, final_norm):
    i = pl.program_id(0)
    shift = mod_ref[:, 3 * D:4 * D]
    scale1 = 1.0 + mod_ref[:, 4 * D:5 * D]
    _fill_hbuf(hbuf, x_ref, xp_ref, xn_ref, g_ref[...], scale1, shift)

    is_prompt = i < N_P_TILES
    seg_first = jnp.logical_not(is_prompt & ((i % P_TILES_PER_SEQ) != 0))
    seg_last = jnp.logical_not(is_prompt & ((i % P_TILES_PER_SEQ) != P_TILES_PER_SEQ - 1))

    def conv3(cs):
        u = jnp.dot(hbuf[...], wu_ref[:, cs], preferred_element_type=F32)
        p0 = jnp.where(seg_first, 0.0, u[8:16])
        n0 = jnp.where(seg_last, 0.0, u[TM + 16:TM + 24])
        prev = jnp.concatenate([p0, u[16:TM + 8]], axis=0)
        nxt = jnp.concatenate([u[24:TM + 16], n0], axis=0)
        return (cb_ref[:, cs] + prev * cw_ref[0:1, cs] + u[16:TM + 16] * cw_ref[1:2, cs]
                + nxt * cw_ref[2:3, cs])

    for c in range(D_FF // FF_CHUNK):
        v = conv3(slice(FF_CHUNK * c, FF_CHUNK * (c + 1)))
        gg = conv3(slice(D_FF + FF_CHUNK * c, D_FF + FF_CHUNK * (c + 1)))
        act[:, FF_CHUNK * c:FF_CHUNK * (c + 1)] = (v * (gg * jax.nn.sigmoid(gg))).astype(BF16)

    out = jnp.dot(act[...], wd_ref[...], preferred_element_type=F32)
    y = x_ref[...] + _per_batch(out, mod_ref[:, 5 * D:6 * D])
    if final_norm:
        y = _rms(y, gf_ref[...])
    o_ref[...] = y


def _ffn(x, mod, g, wu, cw, cb, wd, gf, final_norm):
    prev, nxt = _halo_specs()
    tile = pl.BlockSpec((TM, D), lambda i: (i, 0))
    return pl.pallas_call(
        functools.partial(_ffn_kernel, final_norm=final_norm),
        out_shape=jax.ShapeDtypeStruct((N_ROWS, D), F32),
        grid=(NT,),
        in_specs=[tile, prev, nxt, _mod_spec(lambda i: i), _const_spec((1, D)),
                  _const_spec((D, 2 * D_FF)), _const_spec((3, 2 * D_FF)), _const_spec((1, 2 * D_FF)),
                  _const_spec((D_FF, D)), _const_spec((1, D))],
        out_specs=tile,
        scratch_shapes=[pltpu.VMEM((TM + 2 * HALO, D), BF16), pltpu.VMEM((TM, D_FF), BF16)],
        compiler_params=_params(("arbitrary",)),
        name="conv_ffn",
    )(x, x, x, mod, g, wu, cw, cb, wd, gf)


def _s5_core(ubuf, bb_ref, cc_ref, ab_ref, slabs, hst, y_write, reverse):
    for kt in range(S5_KT):
        slab = slabs.at[kt % 2]
        slab[...] = jnp.dot(ubuf[:, 256 * kt:256 * (kt + 1)], bb_ref[kt], preferred_element_type=F32)
        a_re = jnp.broadcast_to(ab_ref[kt, 0:1, :], (NB, D))
        a_im = jnp.broadcast_to(ab_ref[kt, 1:2, :], (NB, D))
        h_re = hst[kt, :, 0:D]
        h_im = hst[kt, :, D:2 * D]
        for s in range(STEPS):
            t = (STEPS - 1 - s) if reverse else s
            rows = slice(8 * t, 8 * (t + 1))
            n_re = a_re * h_re - a_im * h_im + slab[rows, 0:D]
            n_im = a_re * h_im + a_im * h_re + slab[rows, D:2 * D]
            slab[rows, 0:D] = n_re
            slab[rows, D:2 * D] = n_im
            h_re, h_im = n_re, n_im
        hst[kt, :, 0:D] = h_re
        hst[kt, :, D:2 * D] = h_im
        y_write(kt, jnp.dot(slab[...].astype(BF16), cc_ref[kt], preferred_element_type=F32))


def _s5_bwd_kernel(x_ref, mod_ref, g_ref, bb_ref, cc_ref, ab_ref, h0_ref,
                   yb_ref, fin_ref, ubuf, slabs, hst):
    j = NT - 1 - pl.program_id(0)

    @pl.when(_is_seq_last(j))
    def _():
        hst[...] = h0_ref[...]

    _fill_norm(ubuf, 0, x_ref, g_ref[...], 1.0 + mod_ref[:, D:2 * D], mod_ref[:, 0:D])

    def y_write(kt, val):
        yb_ref[:, 256 * kt:256 * (kt + 1)] = val

    _s5_core(ubuf, bb_ref, cc_ref, ab_ref, slabs, hst, y_write, True)
    fin_ref[...] = hst[...]


def _s5_bwd(x, mod, g, bb, cc, ab, h0):
    tile = pl.BlockSpec((TM, D), lambda i: (NT - 1 - i, 0))
    seq = pl.BlockSpec((None, S5_KT, NB, SLAB), lambda i: (_seq_id(NT - 1 - i), 0, 0, 0))
    return pl.pallas_call(
        _s5_bwd_kernel,
        out_shape=(jax.ShapeDtypeStruct((N_ROWS, D), F32),
                   jax.ShapeDtypeStruct((N_SEQ, S5_KT, NB, SLAB), F32)),
        grid=(NT,),
        in_specs=[tile, _mod_spec(lambda i: NT - 1 - i), _const_spec((1, D)),
                  _const_spec((S5_KT, 256, SLAB)), _const_spec((S5_KT, SLAB, 256)),
                  _const_spec((S5_KT, 2, D)), seq],
        out_specs=(tile, seq),
        scratch_shapes=[pltpu.VMEM((TM, D), BF16), pltpu.VMEM((2, TM, SLAB), F32),
                        pltpu.VMEM((S5_KT, NB, SLAB), F32)],
        compiler_params=_params(("arbitrary",)),
        name="s5_bwd",
    )(x, mod, g, bb, cc, ab, h0)


def _s5_fwd_kernel(x_ref, yb_ref, mod_ref, g_ref, bb_ref, cc_ref, ab_ref, h0_ref, dsk_ref, wglu_ref,
                   x1_ref, fin_ref, ubuf, zbuf, u_s, slabs, hst):
    j = pl.program_id(0)

    @pl.when(_is_seq_first(j))
    def _():
        hst[...] = h0_ref[...]

    _fill_norm(ubuf, 0, x_ref, g_ref[...], 1.0 + mod_ref[:, D:2 * D], mod_ref[:, 0:D], u_s)

    def y_write(kt, val):
        cs = slice(256 * kt, 256 * (kt + 1))
        y = dsk_ref[:, cs] * u_s[:, cs] + yb_ref[:, cs] + val
        zbuf[:, cs] = jax.nn.gelu(y).astype(BF16)

    _s5_core(ubuf, bb_ref, cc_ref, ab_ref, slabs, hst, y_write, False)
    fin_ref[...] = hst[...]

    for c in range(D // COL_CHUNK):
        cs = slice(COL_CHUNK * c, COL_CHUNK * (c + 1))
        v = jnp.dot(zbuf[...], wglu_ref[:, cs], preferred_element_type=F32)
        gg = jnp.dot(zbuf[...], wglu_ref[:, D + COL_CHUNK * c:D + COL_CHUNK * (c + 1)], preferred_element_type=F32)
        out = v * jax.nn.sigmoid(gg)
        x1_ref[:, cs] = x_ref[:, cs] + _per_batch(out, mod_ref[:, 2 * D + COL_CHUNK * c:2 * D + COL_CHUNK * (c + 1)])


def _s5_fwd(x, yb, mod, g, bb, cc, ab, h0, dsk, wglu):
    tile = pl.BlockSpec((TM, D), lambda i: (i, 0))
    seq = pl.BlockSpec((None, S5_KT, NB, SLAB), lambda i: (_seq_id(i), 0, 0, 0))
    return pl.pallas_call(
        _s5_fwd_kernel,
        out_shape=(jax.ShapeDtypeStruct((N_ROWS, D), F32),
                   jax.ShapeDtypeStruct((N_SEQ, S5_KT, NB, SLAB), F32)),
        grid=(NT,),
        in_specs=[tile, tile, _mod_spec(lambda i: i), _const_spec((1, D)),
                  _const_spec((S5_KT, 256, SLAB)), _const_spec((S5_KT, SLAB, 256)),
                  _const_spec((S5_KT, 2, D)), seq, _const_spec((1, D)), _const_spec((D, 2 * D))],
        out_specs=(tile, seq),
        scratch_shapes=[pltpu.VMEM((TM, D), BF16), pltpu.VMEM((TM, D), BF16), pltpu.VMEM((TM, D), F32),
                        pltpu.VMEM((2, TM, SLAB), F32), pltpu.VMEM((S5_KT, NB, SLAB), F32)],
        compiler_params=_params(("arbitrary",)),
        name="s5_fwd",
    )(x, yb, mod, g, bb, cc, ab, h0, dsk, wglu)


def _pair_blockdiag(w_a, w_i):
    eye2 = jnp.eye(2, dtype=F32)

    def bd(w):
        w4 = w.astype(F32).reshape(8, 2, 64, 64)
        return jnp.einsum('phij,hk->phikj', w4, eye2).reshape(8, 128, 128)

    return jnp.concatenate([bd(w_a), bd(w_i)], axis=-1).astype(BF16)


def _s5_dir_params(a_re, a_im, log_dt, b_re, b_im, c_re, c_im):
    l_re = a_re.astype(F32)
    l_im = a_im.astype(F32)
    dt = jnp.exp(log_dt.astype(F32))[:, None]
    mag = jnp.exp(l_re * dt)
    ab_re = mag * jnp.cos(l_im * dt)
    ab_im = mag * jnp.sin(l_im * dt)
    den = l_re * l_re + l_im * l_im
    k_re = ((ab_re - 1.0) * l_re + ab_im * l_im) / den
    k_im = (ab_im * l_re - (ab_re - 1.0) * l_im) / den
    br = b_re.astype(F32)
    bi = b_im.astype(F32)
    bb_re = br * k_re[..., None] - bi * k_im[..., None]
    bb_im = br * k_im[..., None] + bi * k_re[..., None]
    eye = jnp.eye(16, dtype=F32)

    def b_blk(m):
        m4 = m.reshape(S5_KT, 16, 64, 16)
        return jnp.einsum('kgpc,gh->kgchp', m4, eye).reshape(S5_KT, 256, 1024)

    def c_blk(m):
        m4 = m.reshape(S5_KT, 16, 16, 64)
        return jnp.einsum('kgcp,gh->kgphc', m4, eye).reshape(S5_KT, 1024, 256)

    bb = jnp.concatenate([b_blk(bb_re), b_blk(bb_im)], axis=-1).astype(BF16)
    cc = jnp.concatenate([c_blk(c_re.astype(F32)), c_blk(-c_im.astype(F32))], axis=1).astype(BF16)
    ab = jnp.stack([ab_re.reshape(S5_KT, D), ab_im.reshape(S5_KT, D)], axis=1)
    return bb, cc, ab


def _s5_state_to_slab(s_re, s_im):
    re = s_re.astype(F32).reshape(NB, S5_KT, D).transpose(1, 0, 2)
    im = s_im.astype(F32).reshape(NB, S5_KT, D).transpose(1, 0, 2)
    return jnp.concatenate([re, im], axis=-1)


def _s5_slab_to_state(fin):
    re = fin[..., 0:D].transpose(0, 2, 1, 3).reshape(2 * NB, 64, 64)
    im = fin[..., D:2 * D].transpose(0, 2, 1, 3).reshape(2 * NB, 64, 64)
    return re, im


def _to_rows(x_prompt, x_sample):
    xp = x_prompt.astype(F32).reshape(2, NB, P_LEN, D).transpose(0, 2, 1, 3).reshape(2 * P_LEN * NB, D)
    xs = x_sample.astype(F32).transpose(1, 0, 2).reshape(S_LEN * NB, D)
    return jnp.concatenate([xp, xs], axis=0)


def _from_rows(y):
    yp = y[0:2 * P_LEN * NB].reshape(2, P_LEN, NB, D).transpose(0, 2, 1, 3).reshape(2 * NB, P_LEN, D)
    ys = y[2 * P_LEN * NB:].reshape(S_LEN, NB, D).transpose(1, 0, 2)
    return yp, ys


def kernel(x_prompt, x_sample, state_lru, state_s5_re, state_s5_im, c, c_ctx, ada_w, ada_b, norm_mix, norm_ffn, norm_final, lru_w_in, lru_conv_w, lru_conv_b, lru_w_a, lru_b_a, lru_w_i, lru_b_i, lru_lambda, lru_w_out, s5_a_re, s5_a_im, s5_log_dt, s5_b_re, s5_b_im, s5_c_re, s5_c_im, s5_d, s5_w_glu, ffn_w_up, ffn_conv_w, ffn_conv_b, ffn_w_down):
    c_all = jnp.concatenate([c.astype(F32), c_ctx.astype(F32)[None], jnp.zeros((7, D), F32)], axis=0)
    mods = _mod_vectors(c_all, ada_w.astype(BF16), ada_b.astype(F32)[:, None, :])

    def mod_of(l):
        return jnp.stack([jnp.broadcast_to(mods[l, 8][None], (NB, N_MOD * D)), mods[l, 0:NB]], axis=0)

    x = _to_rows(x_prompt, x_sample)
    gf = norm_final[None].astype(F32)

    mod0 = mod_of(0)
    w_in = lru_w_in[0].astype(BF16)
    gate, xr = _lru_in(x, mod0, norm_mix[0][None].astype(F32), w_in[:, 0:D], w_in[:, D:2 * D],
                       lru_conv_w[0].astype(F32), lru_conv_b[0][None].astype(F32))
    zeros_h = jnp.zeros((NB, D), F32)
    sp = jax.nn.softplus(-lru_lambda[0].astype(F32))
    h0_f = jnp.stack([zeros_h, zeros_h, state_lru[:, 0, 0].astype(F32)], axis=0)
    h0_b = jnp.stack([zeros_h, zeros_h, state_lru[:, 0, 1].astype(F32)], axis=0)
    hsb, fin_b = _lru_bwd(xr, _pair_blockdiag(lru_w_a[0, 1], lru_w_i[0, 1]),
                          lru_b_a[0, 1][None].astype(F32), lru_b_i[0, 1][None].astype(F32), sp[1][None], h0_b)
    x, fin_f = _lru_fwd(xr, gate, hsb, x, mod0, _pair_blockdiag(lru_w_a[0, 0], lru_w_i[0, 0]),
                        lru_b_a[0, 0][None].astype(F32), lru_b_i[0, 0][None].astype(F32), sp[0][None], h0_f,
                        lru_w_out[0].astype(BF16))
    x = _ffn(x, mod0, norm_ffn[0][None].astype(F32), ffn_w_up[0].astype(BF16), ffn_conv_w[0].astype(F32),
             ffn_conv_b[0][None].astype(F32), ffn_w_down[0].astype(BF16), gf, False)
    new_lru = jnp.stack([fin_f[0:2].reshape(2 * NB, D), fin_b[0:2].reshape(2 * NB, D)], axis=1)[:, None]

    mod1 = mod_of(1)
    g1 = norm_mix[1][None].astype(F32)
    zeros_s = jnp.zeros((S5_KT, NB, SLAB), F32)
    dirs = []
    for d in range(2):
        bb, cc, ab = _s5_dir_params(s5_a_re[0, d], s5_a_im[0, d], s5_log_dt[0, d], s5_b_re[0, d], s5_b_im[0, d],
                                    s5_c_re[0, d], s5_c_im[0, d])
        h0 = jnp.stack([zeros_s, zeros_s, _s5_state_to_slab(state_s5_re[:, 0, d], state_s5_im[:, 0, d])], axis=0)
        dirs.append((bb, cc, ab, h0))
    yb, fin_sb = _s5_bwd(x, mod1, g1, *dirs[1])
    x, fin_sf = _s5_fwd(x, yb, mod1, g1, *dirs[0], s5_d[0][None].astype(F32), s5_w_glu[0].astype(BF16))
    y = _ffn(x, mod1, norm_ffn[1][None].astype(F32), ffn_w_up[1].astype(BF16), ffn_conv_w[1].astype(F32),
             ffn_conv_b[1][None].astype(F32), ffn_w_down[1].astype(BF16), gf, True)
    f_re, f_im = _s5_slab_to_state(fin_sf[0:2])
    b_re, b_im = _s5_slab_to_state(fin_sb[0:2])
    new_s5_re = jnp.stack([f_re, b_re], axis=1)[:, None]
    new_s5_im = jnp.stack([f_im, b_im], axis=1)[:, None]

    y_prompt, y_sample = _from_rows(y)
    return (y_prompt, y_sample, new_lru, new_s5_re, new_s5_im)
```

```python
import functools

import jax
import jax.numpy as jnp
from jax import lax
from jax.experimental import pallas as pl
from jax.experimental.pallas import tpu as pltpu

F32 = jnp.float32
BF16 = jnp.bfloat16

D = 1024
D_FF = 2816
N_MOD = 6
EPS = 1e-6
LRU_C = 8.0
LOG2_E = 1.4426950408889634
NB = 8
STEPS = 64
TM = STEPS * NB
HALO = 16
NORM_ROWS = 64
FF_CHUNK = 256
COL_CHUNK = 256
P_LEN = 256
S_LEN = 4096
P_TILES_PER_SEQ = P_LEN // STEPS
N_P_TILES = 2 * P_TILES_PER_SEQ
N_S_TILES = S_LEN // STEPS
NT = N_P_TILES + N_S_TILES
N_ROWS = NT * TM
N_SEQ = 3
S5_KT = 4
SLAB = 2048
VMEM_LIMIT = 56 * 1024 * 1024


def _seq_id(j):
    return jnp.where(j >= N_P_TILES, 2, jnp.where(j >= P_TILES_PER_SEQ, 1, 0))


def _is_seq_first(j):
    return (j == 0) | (j == P_TILES_PER_SEQ) | (j == N_P_TILES)


def _is_seq_last(j):
    return (j == P_TILES_PER_SEQ - 1) | (j == N_P_TILES - 1) | (j == NT - 1)


def _params(sem):
    return pltpu.CompilerParams(dimension_semantics=sem, vmem_limit_bytes=VMEM_LIMIT)


def _const_spec(shape):
    n = len(shape)
    return pl.BlockSpec(shape, lambda *_: (0,) * n, pipeline_mode=pl.Buffered(1))


def _mod_spec(tile_of):
    return pl.BlockSpec((None, NB, N_MOD * D),
                        lambda i: (jnp.where(tile_of(i) >= N_P_TILES, 1, 0), 0, 0))


def _rms(xv, g):
    ms = jnp.mean(xv * xv, axis=-1, keepdims=True)
    return xv * lax.rsqrt(ms + EPS) * g


def _per_batch(y, vec):
    r, n = y.shape
    return (y.reshape(r // NB, NB, n) * vec[None]).reshape(r, n)


def _norm_mod(xv, g, scale1, shift):
    r = xv.shape[0]
    y = _rms(xv, g).reshape(r // NB, NB, D) * scale1[None] + shift[None]
    return y.reshape(r, D)


def _fill_norm(dst, dst_off, x_ref, g, scale1, shift, f32_dst=None):
    for k in range(TM // NORM_ROWS):
        r0 = k * NORM_ROWS
        y = _norm_mod(x_ref[r0:r0 + NORM_ROWS], g, scale1, shift)
        dst[dst_off + r0:dst_off + r0 + NORM_ROWS] = y.astype(BF16)
        if f32_dst is not None:
            f32_dst[r0:r0 + NORM_ROWS] = y


def _fill_hbuf(hbuf, x_ref, x_prev, x_next, g, scale1, shift):
    hbuf[0:HALO] = _norm_mod(x_prev, g, scale1, shift).astype(BF16)
    hbuf[HALO + TM:HALO + TM + HALO] = _norm_mod(x_next, g, scale1, shift).astype(BF16)
    _fill_norm(hbuf, HALO, x_ref, g, scale1, shift)


def _mod_kernel(c_ref, w_ref, b_ref, o_ref):
    cv = c_ref[...]
    s = (cv * jax.nn.sigmoid(cv)).astype(BF16)
    o_ref[...] = jnp.dot(s, w_ref[...].astype(BF16), preferred_element_type=F32) + b_ref[...]


def _mod_vectors(c_all, ada_w, ada_b):
    depth = ada_w.shape[0]
    return pl.pallas_call(
        _mod_kernel,
        out_shape=jax.ShapeDtypeStruct((depth, 16, N_MOD * D), F32),
        grid=(depth, N_MOD),
        in_specs=[
            pl.BlockSpec((16, D), lambda l, n: (0, 0)),
            pl.BlockSpec((None, D, D), lambda l, n: (l, 0, n)),
            pl.BlockSpec((None, 1, D), lambda l, n: (l, 0, n)),
        ],
        out_specs=pl.BlockSpec((None, 16, D), lambda l, n: (l, 0, n)),
        compiler_params=_params(("arbitrary", "arbitrary")),
        name="mod_vectors",
    )(c_all, ada_w, ada_b)


def _halo_specs():
    prev = pl.BlockSpec((HALO, D), lambda i: (jnp.maximum(i * (TM // HALO) - 1, 0), 0))
    nxt = pl.BlockSpec((HALO, D), lambda i: (jnp.minimum((i + 1) * (TM // HALO), N_ROWS // HALO - 1), 0))
    return prev, nxt


def _bt_specs(steps, offset):
    per_tile = STEPS // steps

    def prompt_idx(i):
        ip = jnp.minimum(i, N_P_TILES - 1)
        blk = jnp.clip((ip % P_TILES_PER_SEQ) * per_tile + offset, 0, P_LEN // steps - 1)
        return (ip // P_TILES_PER_SEQ, 0, blk, 0)

    def sample_idx(i):
        blk = jnp.clip((i - N_P_TILES) * per_tile + offset, 0, S_LEN // steps - 1)
        return (0, blk, 0)

    return (pl.BlockSpec((None, NB, steps, D), prompt_idx), pl.BlockSpec((NB, steps, D), sample_idx))


def _to_rows(x_bt):
    n = x_bt.shape[1]
    return pltpu.einshape("btd->tbd", x_bt).reshape(n * NB, D)


def _lru_in_kernel(xa_ref, xap_ref, xan_ref, xb_ref, xbp_ref, xbn_ref, mod_ref, g_ref, wi_ref, cw_ref, cb_ref,
                   gate_ref, xr_ref, xc_ref, hbuf):
    i = pl.program_id(0)
    shift = mod_ref[:, 0:D]
    scale1 = 1.0 + mod_ref[:, D:2 * D]

    is_p = i < N_P_TILES
    xc_ref[...] = _to_rows(jnp.where(is_p, xa_ref[...], xb_ref[...]))
    x_prev = _to_rows(jnp.where(is_p, xap_ref[...], xbp_ref[...]))[NB * NB - HALO:NB * NB]
    x_next = _to_rows(jnp.where(is_p, xan_ref[...], xbn_ref[...]))[0:HALO]
    _fill_hbuf(hbuf, xc_ref, x_prev, x_next, g_ref[...], scale1, shift)

    first = _is_seq_first(i)
    last = _is_seq_last(i)

    for c in range(D // COL_CHUNK):
        cs = slice(COL_CHUNK * c, COL_CHUNK * (c + 1))
        cx = slice(D + COL_CHUNK * c, D + COL_CHUNK * (c + 1))
        gate_ref[:, cs] = jax.nn.gelu(jnp.dot(hbuf[HALO:HALO + TM], wi_ref[:, cs], preferred_element_type=F32))
        u = jnp.dot(hbuf[...], wi_ref[:, cx], preferred_element_type=F32)
        head = jnp.where(first, 0.0, u[0:16])
        tail = jnp.where(last, 0.0, u[TM + 16:TM + 24])
        m2 = jnp.concatenate([head, u[16:TM]], axis=0)
        m1 = jnp.concatenate([head[8:16], u[16:TM + 8]], axis=0)
        p1 = jnp.concatenate([u[24:TM + 16], tail], axis=0)
        xr_ref[:, cs] = (cb_ref[:, cs] + m2 * cw_ref[0:1, cs] + m1 * cw_ref[1:2, cs]
                         + u[16:TM + 16] * cw_ref[2:3, cs] + p1 * cw_ref[3:4, cs])


def _lru_in(xp4, xs, mod, g, wi, cw, cb):
    p_tile, s_tile = _bt_specs(STEPS, 0)
    p_prev, s_prev = _bt_specs(NB, -1)
    p_next, s_next = _bt_specs(NB, STEPS // NB)
    tile = pl.BlockSpec((TM, D), lambda i: (i, 0))
    rows = jax.ShapeDtypeStruct((N_ROWS, D), F32)
    return pl.pallas_call(
        _lru_in_kernel,
        out_shape=(rows, rows, rows),
        grid=(NT,),
        in_specs=[p_tile, p_prev, p_next, s_tile, s_prev, s_next,
                  _mod_spec(lambda i: i), _const_spec((1, D)),
                  _const_spec((D, 2 * D)), _const_spec((4, D)), _const_spec((1, D))],
        out_specs=(tile, tile, tile),
        scratch_shapes=[pltpu.VMEM((TM + 2 * HALO, D), BF16)],
        compiler_params=_params(("arbitrary",)),
        name="lru_in",
    )(xp4, xp4, xp4, xs, xs, xs, mod, g, wi, cw, cb)


def _lru_gates(xr_ref, wp_ref, ba_ref, bi_ref, sp_ref, a_s, bx_s, xb_s):
    xb_s[...] = xr_ref[...].astype(BF16)
    for p in range(8):
        cols = slice(128 * p, 128 * (p + 1))
        z = jnp.dot(xb_s[:, cols], wp_ref[p], preferred_element_type=F32)
        r = jax.nn.sigmoid(z[:, 0:128] + ba_ref[:, cols])
        ig = jax.nn.sigmoid(z[:, 128:256] + bi_ref[:, cols])
        a = jnp.exp2(r * sp_ref[:, cols])
        a_s[:, cols] = a
        bx_s[:, cols] = jnp.sqrt(1.0 - a * a) * (ig * xr_ref[:, cols])


def _lru_scan(a_s, bx_s, h_s, out_ref, reverse):
    h = h_s[...]
    for s in range(STEPS):
        t = (STEPS - 1 - s) if reverse else s
        rows = slice(8 * t, 8 * (t + 1))
        h = a_s[rows] * h + bx_s[rows]
        out_ref[rows] = h
    h_s[...] = h


def _lru_bwd_kernel(xr_ref, wp_ref, ba_ref, bi_ref, sp_ref, h0_ref,
                    hs_ref, fin_ref, a_s, bx_s, xb_s, h_s):
    j = NT - 1 - pl.program_id(0)

    @pl.when(_is_seq_last(j))
    def _():
        h_s[...] = h0_ref[...]

    _lru_gates(xr_ref, wp_ref, ba_ref, bi_ref, sp_ref, a_s, bx_s, xb_s)
    _lru_scan(a_s, bx_s, h_s, hs_ref, True)
    fin_ref[...] = h_s[...]


def _lru_scratch():
    return [pltpu.VMEM((TM, D), F32), pltpu.VMEM((TM, D), F32), pltpu.VMEM((TM, D), BF16),
            pltpu.VMEM((NB, D), F32)]


def _lru_bwd(xr, wp, ba, bi, sp, h0):
    tile = pl.BlockSpec((TM, D), lambda i: (NT - 1 - i, 0))
    seq = pl.BlockSpec((None, NB, D), lambda i: (_seq_id(NT - 1 - i), 0, 0))
    return pl.pallas_call(
        _lru_bwd_kernel,
        out_shape=(jax.ShapeDtypeStruct((N_ROWS, D), F32), jax.ShapeDtypeStruct((N_SEQ, NB, D), F32)),
        grid=(NT,),
        in_specs=[tile, _const_spec((8, 128, 256)), _const_spec((1, D)), _const_spec((1, D)),
                  _const_spec((1, D)), seq],
        out_specs=(tile, seq),
        scratch_shapes=_lru_scratch(),
        compiler_params=_params(("arbitrary",)),
        name="lru_bwd",
    )(xr, wp, ba, bi, sp, h0)


def _lru_fwd_kernel(xr_ref, gate_ref, hsb_ref, x_ref, mod_ref, wp_ref, ba_ref, bi_ref, sp_ref, h0_ref,
                    wo_ref, x1_ref, fin_ref, a_s, bx_s, xb_s, h_s):
    j = pl.program_id(0)

    @pl.when(_is_seq_first(j))
    def _():
        h_s[...] = h0_ref[...]

    _lru_gates(xr_ref, wp_ref, ba_ref, bi_ref, sp_ref, a_s, bx_s, xb_s)
    _lru_scan(a_s, bx_s, h_s, bx_s, False)
    fin_ref[...] = h_s[...]

    xb_s[...] = ((bx_s[...] + hsb_ref[...]) * gate_ref[...]).astype(BF16)
    for c in range(D // COL_CHUNK):
        cs = slice(COL_CHUNK * c, COL_CHUNK * (c + 1))
        out = jnp.dot(xb_s[...], wo_ref[:, cs], preferred_element_type=F32)
        x1_ref[:, cs] = x_ref[:, cs] + _per_batch(out, mod_ref[:, 2 * D + COL_CHUNK * c:2 * D + COL_CHUNK * (c + 1)])


def _lru_fwd(xr, gate, hsb, x, mod, wp, ba, bi, sp, h0, wo):
    tile = pl.BlockSpec((TM, D), lambda i: (i, 0))
    seq = pl.BlockSpec((None, NB, D), lambda i: (_seq_id(i), 0, 0))
    return pl.pallas_call(
        _lru_fwd_kernel,
        out_shape=(jax.ShapeDtypeStruct((N_ROWS, D), F32), jax.ShapeDtypeStruct((N_SEQ, NB, D), F32)),
        grid=(NT,),
        in_specs=[tile, tile, tile, tile, _mod_spec(lambda i: i), _const_spec((8, 128, 256)),
                  _const_spec((1, D)), _const_spec((1, D)), _const_spec((1, D)), seq, _const_spec((D, D))],
        out_specs=(tile, seq),
        scratch_shapes=_lru_scratch(),
        compiler_params=_params(("arbitrary",)),
        name="lru_fwd",
    )(xr, gate, hsb, x, mod, wp, ba, bi, sp, h0, wo)


def _ffn_kernel(x_ref, xp_ref, xn_ref, mod_ref, g_ref, wu_ref, cw_ref, cb_ref, wd_ref, gf_ref,
                *rest, final_norm):
    if final_norm:
        op_ref, os_ref, hbuf, act = rest
    else:
        o_ref, hbuf, act = rest
    i = pl.program_id(0)
    shift = mod_ref[:, 3 * D:4 * D]
    scale1 = 1.0 + mod_ref[:, 4 * D:5 * D]
    _fill_hbuf(hbuf, x_ref, xp_ref[...], xn_ref[...], g_ref[...], scale1, shift)

    is_prompt = i < N_P_TILES
    seg_first = jnp.logical_not(is_prompt & ((i % P_TILES_PER_SEQ) != 0))
    seg_last = jnp.logical_not(is_prompt & ((i % P_TILES_PER_SEQ) != P_TILES_PER_SEQ - 1))

    def conv3(cs):
        u = jnp.dot(hbuf[...], wu_ref[:, cs], preferred_element_type=F32)
        p0 = jnp.where(seg_first, 0.0, u[8:16])
        n0 = jnp.where(seg_last, 0.0, u[TM + 16:TM + 24])
        prev = jnp.concatenate([p0, u[16:TM + 8]], axis=0)
        nxt = jnp.concatenate([u[24:TM + 16], n0], axis=0)
        return (cb_ref[:, cs] + prev * cw_ref[0:1, cs] + u[16:TM + 16] * cw_ref[1:2, cs]
                + nxt * cw_ref[2:3, cs])

    for c in range(D_FF // FF_CHUNK):
        v = conv3(slice(FF_CHUNK * c, FF_CHUNK * (c + 1)))
        gg = conv3(slice(D_FF + FF_CHUNK * c, D_FF + FF_CHUNK * (c + 1)))
        act[:, FF_CHUNK * c:FF_CHUNK * (c + 1)] = (v * (gg * jax.nn.sigmoid(gg))).astype(BF16)

    out = jnp.dot(act[...], wd_ref[...], preferred_element_type=F32)
    y = x_ref[...] + _per_batch(out, mod_ref[:, 5 * D:6 * D])
    if final_norm:
        y = pltpu.einshape("tbd->btd", _rms(y, gf_ref[...]).reshape(STEPS, NB, D))

        @pl.when(i < N_P_TILES)
        def _():
            op_ref[...] = y

        @pl.when(i >= N_P_TILES)
        def _():
            os_ref[...] = y
    else:
        o_ref[...] = y


def _ffn(x, mod, g, wu, cw, cb, wd, gf, final_norm):
    prev, nxt = _halo_specs()
    tile = pl.BlockSpec((TM, D), lambda i: (i, 0))
    if final_norm:
        out_shape = (jax.ShapeDtypeStruct((2, NB, P_LEN, D), F32), jax.ShapeDtypeStruct((NB, S_LEN, D), F32))
        out_specs = _bt_specs(STEPS, 0)
    else:
        out_shape = jax.ShapeDtypeStruct((N_ROWS, D), F32)
        out_specs = tile
    return pl.pallas_call(
        functools.partial(_ffn_kernel, final_norm=final_norm),
        out_shape=out_shape,
        grid=(NT,),
        in_specs=[tile, prev, nxt, _mod_spec(lambda i: i), _const_spec((1, D)),
                  _const_spec((D, 2 * D_FF)), _const_spec((3, 2 * D_FF)), _const_spec((1, 2 * D_FF)),
                  _const_spec((D_FF, D)), _const_spec((1, D))],
        out_specs=out_specs,
        scratch_shapes=[pltpu.VMEM((TM + 2 * HALO, D), BF16), pltpu.VMEM((TM, D_FF), BF16)],
        compiler_params=_params(("arbitrary",)),
        name="conv_ffn",
    )(x, x, x, mod, g, wu, cw, cb, wd, gf)


def _s5_core(ubuf, bb_ref, cc_ref, ab_ref, slabs, hst, y_write, reverse):
    for kt in range(S5_KT):
        slab = slabs.at[kt % 2]
        slab[...] = jnp.dot(ubuf[:, 256 * kt:256 * (kt + 1)], bb_ref[kt], preferred_element_type=F32)
        a_re = jnp.broadcast_to(ab_ref[kt, 0:1, :], (NB, D))
        a_im = jnp.broadcast_to(ab_ref[kt, 1:2, :], (NB, D))
        h_re = hst[kt, :, 0:D]
        h_im = hst[kt, :, D:2 * D]
        for s in range(STEPS):
            t = (STEPS - 1 - s) if reverse else s
            rows = slice(8 * t, 8 * (t + 1))
            n_re = a_re * h_re - a_im * h_im + slab[rows, 0:D]
            n_im = a_re * h_im + a_im * h_re + slab[rows, D:2 * D]
            slab[rows, 0:D] = n_re
            slab[rows, D:2 * D] = n_im
            h_re, h_im = n_re, n_im
        hst[kt, :, 0:D] = h_re
        hst[kt, :, D:2 * D] = h_im
        y_write(kt, jnp.dot(slab[...].astype(BF16), cc_ref[kt], preferred_element_type=F32))


def _s5_bwd_kernel(x_ref, mod_ref, g_ref, bb_ref, cc_ref, ab_ref, h0_ref,
                   yb_ref, fin_ref, ubuf, slabs, hst):
    j = NT - 1 - pl.program_id(0)

    @pl.when(_is_seq_last(j))
    def _():
        hst[...] = h0_ref[...]

    _fill_norm(ubuf, 0, x_ref, g_ref[...], 1.0 + mod_ref[:, D:2 * D], mod_ref[:, 0:D])

    def y_write(kt, val):
        yb_ref[:, 256 * kt:256 * (kt + 1)] = val

    _s5_core(ubuf, bb_ref, cc_ref, ab_ref, slabs, hst, y_write, True)
    fin_ref[...] = hst[...]


def _s5_bwd(x, mod, g, bb, cc, ab, h0):
    tile = pl.BlockSpec((TM, D), lambda i: (NT - 1 - i, 0))
    seq = pl.BlockSpec((None, S5_KT, NB, SLAB), lambda i: (_seq_id(NT - 1 - i), 0, 0, 0))
    return pl.pallas_call(
        _s5_bwd_kernel,
        out_shape=(jax.ShapeDtypeStruct((N_ROWS, D), F32),
                   jax.ShapeDtypeStruct((N_SEQ, S5_KT, NB, SLAB), F32)),
        grid=(NT,),
        in_specs=[tile, _mod_spec(lambda i: NT - 1 - i), _const_spec((1, D)),
                  _const_spec((S5_KT, 256, SLAB)), _const_spec((S5_KT, SLAB, 256)),
                  _const_spec((S5_KT, 2, D)), seq],
        out_specs=(tile, seq),
        scratch_shapes=[pltpu.VMEM((TM, D), BF16), pltpu.VMEM((2, TM, SLAB), F32),
                        pltpu.VMEM((S5_KT, NB, SLAB), F32)],
        compiler_params=_params(("arbitrary",)),
        name="s5_bwd",
    )(x, mod, g, bb, cc, ab, h0)


def _s5_fwd_kernel(x_ref, yb_ref, mod_ref, g_ref, bb_ref, cc_ref, ab_ref, h0_ref, dsk_ref, wglu_ref,
                   x1_ref, fin_ref, ubuf, zbuf, u_s, slabs, hst):
    j = pl.program_id(0)

    @pl.when(_is_seq_first(j))
    def _():
        hst[...] = h0_ref[...]

    _fill_norm(ubuf, 0, x_ref, g_ref[...], 1.0 + mod_ref[:, D:2 * D], mod_ref[:, 0:D], u_s)

    def y_write(kt, val):
        cs = slice(256 * kt, 256 * (kt + 1))
        y = dsk_ref[:, cs] * u_s[:, cs] + yb_ref[:, cs] + val
        zbuf[:, cs] = jax.nn.gelu(y).astype(BF16)

    _s5_core(ubuf, bb_ref, cc_ref, ab_ref, slabs, hst, y_write, False)
    fin_ref[...] = hst[...]

    for c in range(D // COL_CHUNK):
        cs = slice(COL_CHUNK * c, COL_CHUNK * (c + 1))
        v = jnp.dot(zbuf[...], wglu_ref[:, cs], preferred_element_type=F32)
        gg = jnp.dot(zbuf[...], wglu_ref[:, D + COL_CHUNK * c:D + COL_CHUNK * (c + 1)], preferred_element_type=F32)
        out = v * jax.nn.sigmoid(gg)
        x1_ref[:, cs] = x_ref[:, cs] + _per_batch(out, mod_ref[:, 2 * D + COL_CHUNK * c:2 * D + COL_CHUNK * (c + 1)])


def _s5_fwd(x, yb, mod, g, bb, cc, ab, h0, dsk, wglu):
    tile = pl.BlockSpec((TM, D), lambda i: (i, 0))
    seq = pl.BlockSpec((None, S5_KT, NB, SLAB), lambda i: (_seq_id(i), 0, 0, 0))
    return pl.pallas_call(
        _s5_fwd_kernel,
        out_shape=(jax.ShapeDtypeStruct((N_ROWS, D), F32),
                   jax.ShapeDtypeStruct((N_SEQ, S5_KT, NB, SLAB), F32)),
        grid=(NT,),
        in_specs=[tile, tile, _mod_spec(lambda i: i), _const_spec((1, D)),
                  _const_spec((S5_KT, 256, SLAB)), _const_spec((S5_KT, SLAB, 256)),
                  _const_spec((S5_KT, 2, D)), seq, _const_spec((1, D)), _const_spec((D, 2 * D))],
        out_specs=(tile, seq),
        scratch_shapes=[pltpu.VMEM((TM, D), BF16), pltpu.VMEM((TM, D), BF16), pltpu.VMEM((TM, D), F32),
                        pltpu.VMEM((2, TM, SLAB), F32), pltpu.VMEM((S5_KT, NB, SLAB), F32)],
        compiler_params=_params(("arbitrary",)),
        name="s5_fwd",
    )(x, yb, mod, g, bb, cc, ab, h0, dsk, wglu)


def _pair_blockdiag(w_a, w_i):
    eye2 = jnp.eye(2, dtype=F32)

    def bd(w):
        w4 = w.astype(F32).reshape(8, 2, 64, 64)
        return jnp.einsum('phij,hk->phikj', w4, eye2).reshape(8, 128, 128)

    return jnp.concatenate([bd(w_a), bd(w_i)], axis=-1).astype(BF16)


def _s5_dir_params(a_re, a_im, log_dt, b_re, b_im, c_re, c_im):
    l_re = a_re.astype(F32)
    l_im = a_im.astype(F32)
    dt = jnp.exp(log_dt.astype(F32))[:, None]
    mag = jnp.exp(l_re * dt)
    ab_re = mag * jnp.cos(l_im * dt)
    ab_im = mag * jnp.sin(l_im * dt)
    den = l_re * l_re + l_im * l_im
    k_re = ((ab_re - 1.0) * l_re + ab_im * l_im) / den
    k_im = (ab_im * l_re - (ab_re - 1.0) * l_im) / den
    br = b_re.astype(F32)
    bi = b_im.astype(F32)
    bb_re = br * k_re[..., None] - bi * k_im[..., None]
    bb_im = br * k_im[..., None] + bi * k_re[..., None]
    eye = jnp.eye(16, dtype=F32)

    def b_blk(m):
        m4 = m.reshape(S5_KT, 16, 64, 16)
        return jnp.einsum('kgpc,gh->kgchp', m4, eye).reshape(S5_KT, 256, 1024)

    def c_blk(m):
        m4 = m.reshape(S5_KT, 16, 16, 64)
        return jnp.einsum('kgcp,gh->kgphc', m4, eye).reshape(S5_KT, 1024, 256)

    bb = jnp.concatenate([b_blk(bb_re), b_blk(bb_im)], axis=-1).astype(BF16)
    cc = jnp.concatenate([c_blk(c_re.astype(F32)), c_blk(-c_im.astype(F32))], axis=1).astype(BF16)
    ab = jnp.stack([ab_re.reshape(S5_KT, D), ab_im.reshape(S5_KT, D)], axis=1)
    return bb, cc, ab


def _s5_state_to_slab(s_re, s_im):
    re = s_re.astype(F32).reshape(NB, S5_KT, D).transpose(1, 0, 2)
    im = s_im.astype(F32).reshape(NB, S5_KT, D).transpose(1, 0, 2)
    return jnp.concatenate([re, im], axis=-1)


def _s5_slab_to_state(fin):
    re = fin[..., 0:D].transpose(0, 2, 1, 3).reshape(2 * NB, 64, 64)
    im = fin[..., D:2 * D].transpose(0, 2, 1, 3).reshape(2 * NB, 64, 64)
    return re, im


def kernel(x_prompt, x_sample, state_lru, state_s5_re, state_s5_im, c, c_ctx, ada_w, ada_b, norm_mix, norm_ffn, norm_final, lru_w_in, lru_conv_w, lru_conv_b, lru_w_a, lru_b_a, lru_w_i, lru_b_i, lru_lambda, lru_w_out, s5_a_re, s5_a_im, s5_log_dt, s5_b_re, s5_b_im, s5_c_re, s5_c_im, s5_d, s5_w_glu, ffn_w_up, ffn_conv_w, ffn_conv_b, ffn_w_down):
    c_all = jnp.concatenate([c.astype(F32), c_ctx.astype(F32)[None], jnp.zeros((7, D), F32)], axis=0)
    mods = _mod_vectors(c_all, ada_w.astype(F32), ada_b.astype(F32)[:, None, :])

    def mod_of(l):
        return jnp.stack([jnp.broadcast_to(mods[l, 8][None], (NB, N_MOD * D)), mods[l, 0:NB]], axis=0)

    xp4 = x_prompt.astype(F32).reshape(2, NB, P_LEN, D)
    gf = norm_final[None].astype(F32)

    mod0 = mod_of(0)
    gate, xr, x = _lru_in(xp4, x_sample.astype(F32), mod0, norm_mix[0][None].astype(F32), lru_w_in[0].astype(BF16),
                          lru_conv_w[0].astype(F32), lru_conv_b[0][None].astype(F32))
    zeros_h = jnp.zeros((NB, D), F32)
    sp = jax.nn.softplus(-lru_lambda[0].astype(F32)) * (-LRU_C * LOG2_E)
    h0_f = jnp.stack([zeros_h, zeros_h, state_lru[:, 0, 0].astype(F32)], axis=0)
    h0_b = jnp.stack([zeros_h, zeros_h, state_lru[:, 0, 1].astype(F32)], axis=0)
    hsb, fin_b = _lru_bwd(xr, _pair_blockdiag(lru_w_a[0, 1], lru_w_i[0, 1]),
                          lru_b_a[0, 1][None].astype(F32), lru_b_i[0, 1][None].astype(F32), sp[1][None], h0_b)
    x, fin_f = _lru_fwd(xr, gate, hsb, x, mod0, _pair_blockdiag(lru_w_a[0, 0], lru_w_i[0, 0]),
                        lru_b_a[0, 0][None].astype(F32), lru_b_i[0, 0][None].astype(F32), sp[0][None], h0_f,
                        lru_w_out[0].astype(BF16))
    x = _ffn(x, mod0, norm_ffn[0][None].astype(F32), ffn_w_up[0].astype(BF16), ffn_conv_w[0].astype(F32),
             ffn_conv_b[0][None].astype(F32), ffn_w_down[0].astype(BF16), gf, False)
    new_lru = jnp.stack([fin_f[0:2].reshape(2 * NB, D), fin_b[0:2].reshape(2 * NB, D)], axis=1)[:, None]

    mod1 = mod_of(1)
    g1 = norm_mix[1][None].astype(F32)
    zeros_s = jnp.zeros((S5_KT, NB, SLAB), F32)
    dirs = []
    for d in range(2):
        bb, cc, ab = _s5_dir_params(s5_a_re[0, d], s5_a_im[0, d], s5_log_dt[0, d], s5_b_re[0, d], s5_b_im[0, d],
                                    s5_c_re[0, d], s5_c_im[0, d])
        h0 = jnp.stack([zeros_s, zeros_s, _s5_state_to_slab(state_s5_re[:, 0, d], state_s5_im[:, 0, d])], axis=0)
        dirs.append((bb, cc, ab, h0))
    yb, fin_sb = _s5_bwd(x, mod1, g1, *dirs[1])
    x, fin_sf = _s5_fwd(x, yb, mod1, g1, *dirs[0], s5_d[0][None].astype(F32), s5_w_glu[0].astype(BF16))
    y_p4, y_sample = _ffn(x, mod1, norm_ffn[1][None].astype(F32), ffn_w_up[1].astype(BF16), ffn_conv_w[1].astype(F32),
             ffn_conv_b[1][None].astype(F32), ffn_w_down[1].astype(BF16), gf, True)
    f_re, f_im = _s5_slab_to_state(fin_sf[0:2])
    b_re, b_im = _s5_slab_to_state(fin_sb[0:2])
    new_s5_re = jnp.stack([f_re, b_re], axis=1)[:, None]
    new_s5_im = jnp.stack([f_im, b_im], axis=1)[:, None]

    return (y_p4.reshape(2 * NB, P_LEN, D), y_sample, new_lru, new_s5_re, new_s5_im)
```

```python
import functools

import jax
import jax.numpy as jnp
from jax import lax
from jax.experimental import pallas as pl
from jax.experimental.pallas import tpu as pltpu

F32 = jnp.float32
BF16 = jnp.bfloat16

D = 1024
D_FF = 2816
N_MOD = 6
EPS = 1e-6
LRU_C = 8.0
LOG2_E = 1.4426950408889634
NB = 8
STEPS = 64
TM = STEPS * NB
HALO = 16
NORM_ROWS = 64
FF_CHUNK = 256
COL_CHUNK = 256
P_LEN = 256
S_LEN = 4096
P_TILES_PER_SEQ = P_LEN // STEPS
N_P_TILES = 2 * P_TILES_PER_SEQ
N_S_TILES = S_LEN // STEPS
NT = N_P_TILES + N_S_TILES
N_ROWS = NT * TM
N_SEQ = 3
S5_KT = 4
SLAB = 2048
S5_BN = 256
VMEM_LIMIT = 56 * 1024 * 1024


def _seq_id(j):
    return jnp.where(j >= N_P_TILES, 2, jnp.where(j >= P_TILES_PER_SEQ, 1, 0))


def _is_seq_first(j):
    return (j == 0) | (j == P_TILES_PER_SEQ) | (j == N_P_TILES)


def _is_seq_last(j):
    return (j == P_TILES_PER_SEQ - 1) | (j == N_P_TILES - 1) | (j == NT - 1)


def _params(sem):
    return pltpu.CompilerParams(dimension_semantics=sem, vmem_limit_bytes=VMEM_LIMIT)


def _const_spec(shape):
    n = len(shape)
    return pl.BlockSpec(shape, lambda *_: (0,) * n, pipeline_mode=pl.Buffered(1))


def _mod_spec(tile_of):
    return pl.BlockSpec((None, NB, N_MOD * D),
                        lambda i: (jnp.where(tile_of(i) >= N_P_TILES, 1, 0), 0, 0))


def _rms(xv, g):
    ms = jnp.mean(xv * xv, axis=-1, keepdims=True)
    return xv * lax.rsqrt(ms + EPS) * g


def _per_batch(y, vec):
    r, n = y.shape
    return (y.reshape(r // NB, NB, n) * vec[None]).reshape(r, n)


def _norm_mod(xv, g, scale1, shift):
    r = xv.shape[0]
    y = _rms(xv, g).reshape(r // NB, NB, D) * scale1[None] + shift[None]
    return y.reshape(r, D)


def _fill_norm(dst, dst_off, x_ref, g, scale1, shift, f32_dst=None):
    for k in range(TM // NORM_ROWS):
        r0 = k * NORM_ROWS
        y = _norm_mod(x_ref[r0:r0 + NORM_ROWS], g, scale1, shift)
        dst[dst_off + r0:dst_off + r0 + NORM_ROWS] = y.astype(BF16)
        if f32_dst is not None:
            f32_dst[r0:r0 + NORM_ROWS] = y


def _fill_hbuf(hbuf, x_ref, x_prev, x_next, g, scale1, shift):
    hbuf[0:HALO] = _norm_mod(x_prev, g, scale1, shift).astype(BF16)
    hbuf[HALO + TM:HALO + TM + HALO] = _norm_mod(x_next, g, scale1, shift).astype(BF16)
    _fill_norm(hbuf, HALO, x_ref, g, scale1, shift)


def _mod_kernel(c_ref, w_ref, b_ref, o_ref):
    cv = c_ref[...]
    s = (cv * jax.nn.sigmoid(cv)).astype(BF16)
    o_ref[...] = jnp.dot(s, w_ref[...].astype(BF16), preferred_element_type=F32) + b_ref[...]


def _mod_vectors(c_all, ada_w, ada_b):
    depth = ada_w.shape[0]
    return pl.pallas_call(
        _mod_kernel,
        out_shape=jax.ShapeDtypeStruct((depth, 16, N_MOD * D), F32),
        grid=(depth, N_MOD),
        in_specs=[
            pl.BlockSpec((16, D), lambda l, n: (0, 0)),
            pl.BlockSpec((None, D, D), lambda l, n: (l, 0, n)),
            pl.BlockSpec((None, 1, D), lambda l, n: (l, 0, n)),
        ],
        out_specs=pl.BlockSpec((None, 16, D), lambda l, n: (l, 0, n)),
        compiler_params=_params(("arbitrary", "arbitrary")),
        name="mod_vectors",
    )(c_all, ada_w, ada_b)


def _halo_specs():
    prev = pl.BlockSpec((HALO, D), lambda i: (jnp.maximum(i * (TM // HALO) - 1, 0), 0))
    nxt = pl.BlockSpec((HALO, D), lambda i: (jnp.minimum((i + 1) * (TM // HALO), N_ROWS // HALO - 1), 0))
    return prev, nxt


def _bt_specs(steps, offset):
    per_tile = STEPS // steps

    def prompt_idx(i):
        ip = jnp.minimum(i, N_P_TILES - 1)
        blk = jnp.clip((ip % P_TILES_PER_SEQ) * per_tile + offset, 0, P_LEN // steps - 1)
        return (ip // P_TILES_PER_SEQ, 0, blk, 0)

    def sample_idx(i):
        blk = jnp.clip((i - N_P_TILES) * per_tile + offset, 0, S_LEN // steps - 1)
        return (0, blk, 0)

    return (pl.BlockSpec((None, NB, steps, D), prompt_idx), pl.BlockSpec((NB, steps, D), sample_idx))


def _to_rows(x_bt):
    n = x_bt.shape[1]
    return jnp.swapaxes(x_bt, 0, 1).reshape(n * NB, D)


def _lru_in_kernel(xa_ref, xap_ref, xan_ref, xb_ref, xbp_ref, xbn_ref, mod_ref, g_ref, wi_ref, cw_ref, cb_ref,
                   gate_ref, xr_ref, xc_ref, hbuf):
    i = pl.program_id(0)
    shift = mod_ref[:, 0:D]
    scale1 = 1.0 + mod_ref[:, D:2 * D]

    is_p = i < N_P_TILES
    xc_ref[...] = _to_rows(jnp.where(is_p, xa_ref[...], xb_ref[...]))
    x_prev = _to_rows(jnp.where(is_p, xap_ref[...], xbp_ref[...]))[NB * NB - HALO:NB * NB]
    x_next = _to_rows(jnp.where(is_p, xan_ref[...], xbn_ref[...]))[0:HALO]
    _fill_hbuf(hbuf, xc_ref, x_prev, x_next, g_ref[...], scale1, shift)

    first = _is_seq_first(i)
    last = _is_seq_last(i)

    for c in range(D // COL_CHUNK):
        cs = slice(COL_CHUNK * c, COL_CHUNK * (c + 1))
        cx = slice(D + COL_CHUNK * c, D + COL_CHUNK * (c + 1))
        gate_ref[:, cs] = jax.nn.gelu(jnp.dot(hbuf[HALO:HALO + TM], wi_ref[:, cs], preferred_element_type=F32))
        u = jnp.dot(hbuf[...], wi_ref[:, cx], preferred_element_type=F32)
        head = jnp.where(first, 0.0, u[0:16])
        tail = jnp.where(last, 0.0, u[TM + 16:TM + 24])
        m2 = jnp.concatenate([head, u[16:TM]], axis=0)
        m1 = jnp.concatenate([head[8:16], u[16:TM + 8]], axis=0)
        p1 = jnp.concatenate([u[24:TM + 16], tail], axis=0)
        xr_ref[:, cs] = (cb_ref[:, cs] + m2 * cw_ref[0:1, cs] + m1 * cw_ref[1:2, cs]
                         + u[16:TM + 16] * cw_ref[2:3, cs] + p1 * cw_ref[3:4, cs])


def _lru_in(xp4, xs, mod, g, wi, cw, cb):
    p_tile, s_tile = _bt_specs(STEPS, 0)
    p_prev, s_prev = _bt_specs(NB, -1)
    p_next, s_next = _bt_specs(NB, STEPS // NB)
    tile = pl.BlockSpec((TM, D), lambda i: (i, 0))
    rows = jax.ShapeDtypeStruct((N_ROWS, D), F32)
    return pl.pallas_call(
        _lru_in_kernel,
        out_shape=(rows, rows, rows),
        grid=(NT,),
        in_specs=[p_tile, p_prev, p_next, s_tile, s_prev, s_next,
                  _mod_spec(lambda i: i), _const_spec((1, D)),
                  _const_spec((D, 2 * D)), _const_spec((4, D)), _const_spec((1, D))],
        out_specs=(tile, tile, tile),
        scratch_shapes=[pltpu.VMEM((TM + 2 * HALO, D), BF16)],
        compiler_params=_params(("arbitrary",)),
        name="lru_in",
    )(xp4, xp4, xp4, xs, xs, xs, mod, g, wi, cw, cb)


def _lru_gates(xr_ref, wp_ref, ba_ref, bi_ref, sp_ref, a_s, bx_s, xb_s):
    xb_s[...] = xr_ref[...].astype(BF16)
    for p in range(8):
        cols = slice(128 * p, 128 * (p + 1))
        z = jnp.dot(xb_s[:, cols], wp_ref[p], preferred_element_type=F32)
        r = jax.nn.sigmoid(z[:, 0:128] + ba_ref[:, cols])
        ig = jax.nn.sigmoid(z[:, 128:256] + bi_ref[:, cols])
        a = jnp.exp2(r * sp_ref[:, cols])
        a_s[:, cols] = a
        bx_s[:, cols] = jnp.sqrt(1.0 - a * a) * (ig * xr_ref[:, cols])


def _lru_scan(a_s, bx_s, h_s, out_ref, reverse):
    h = h_s[...]
    for s in range(STEPS):
        t = (STEPS - 1 - s) if reverse else s
        rows = slice(8 * t, 8 * (t + 1))
        h = a_s[rows] * h + bx_s[rows]
        out_ref[rows] = h
    h_s[...] = h


def _lru_bwd_kernel(xr_ref, wp_ref, ba_ref, bi_ref, sp_ref, h0_ref,
                    hs_ref, fin_ref, a_s, bx_s, xb_s, h_s):
    j = NT - 1 - pl.program_id(0)

    @pl.when(_is_seq_last(j))
    def _():
        h_s[...] = h0_ref[...]

    _lru_gates(xr_ref, wp_ref, ba_ref, bi_ref, sp_ref, a_s, bx_s, xb_s)
    _lru_scan(a_s, bx_s, h_s, hs_ref, True)
    fin_ref[...] = h_s[...]


def _lru_scratch():
    return [pltpu.VMEM((TM, D), F32), pltpu.VMEM((TM, D), F32), pltpu.VMEM((TM, D), BF16),
            pltpu.VMEM((NB, D), F32)]


def _lru_bwd(xr, wp, ba, bi, sp, h0):
    tile = pl.BlockSpec((TM, D), lambda i: (NT - 1 - i, 0))
    seq = pl.BlockSpec((None, NB, D), lambda i: (_seq_id(NT - 1 - i), 0, 0))
    return pl.pallas_call(
        _lru_bwd_kernel,
        out_shape=(jax.ShapeDtypeStruct((N_ROWS, D), F32), jax.ShapeDtypeStruct((N_SEQ, NB, D), F32)),
        grid=(NT,),
        in_specs=[tile, _const_spec((8, 128, 256)), _const_spec((1, D)), _const_spec((1, D)),
                  _const_spec((1, D)), seq],
        out_specs=(tile, seq),
        scratch_shapes=_lru_scratch(),
        compiler_params=_params(("arbitrary",)),
        name="lru_bwd",
    )(xr, wp, ba, bi, sp, h0)


def _lru_fwd_kernel(xr_ref, gate_ref, hsb_ref, x_ref, mod_ref, wp_ref, ba_ref, bi_ref, sp_ref, h0_ref,
                    wo_ref, x1_ref, fin_ref, a_s, bx_s, xb_s, h_s):
    j = pl.program_id(0)

    @pl.when(_is_seq_first(j))
    def _():
        h_s[...] = h0_ref[...]

    _lru_gates(xr_ref, wp_ref, ba_ref, bi_ref, sp_ref, a_s, bx_s, xb_s)
    _lru_scan(a_s, bx_s, h_s, bx_s, False)
    fin_ref[...] = h_s[...]

    xb_s[...] = ((bx_s[...] + hsb_ref[...]) * gate_ref[...]).astype(BF16)
    for c in range(D // COL_CHUNK):
        cs = slice(COL_CHUNK * c, COL_CHUNK * (c + 1))
        out = jnp.dot(xb_s[...], wo_ref[:, cs], preferred_element_type=F32)
        x1_ref[:, cs] = x_ref[:, cs] + _per_batch(out, mod_ref[:, 2 * D + COL_CHUNK * c:2 * D + COL_CHUNK * (c + 1)])


def _lru_fwd(xr, gate, hsb, x, mod, wp, ba, bi, sp, h0, wo):
    tile = pl.BlockSpec((TM, D), lambda i: (i, 0))
    seq = pl.BlockSpec((None, NB, D), lambda i: (_seq_id(i), 0, 0))
    return pl.pallas_call(
        _lru_fwd_kernel,
        out_shape=(jax.ShapeDtypeStruct((N_ROWS, D), F32), jax.ShapeDtypeStruct((N_SEQ, NB, D), F32)),
        grid=(NT,),
        in_specs=[tile, tile, tile, tile, _mod_spec(lambda i: i), _const_spec((8, 128, 256)),
                  _const_spec((1, D)), _const_spec((1, D)), _const_spec((1, D)), seq, _const_spec((D, D))],
        out_specs=(tile, seq),
        scratch_shapes=_lru_scratch(),
        compiler_params=_params(("arbitrary",)),
        name="lru_fwd",
    )(xr, gate, hsb, x, mod, wp, ba, bi, sp, h0, wo)


def _ffn_kernel(x_ref, xp_ref, xn_ref, mod_ref, g_ref, wu_ref, cw_ref, cb_ref, wd_ref, gf_ref,
                *rest, final_norm):
    if final_norm:
        op_ref, os_ref, hbuf, act = rest
    else:
        o_ref, hbuf, act = rest
    i = pl.program_id(0)
    shift = mod_ref[:, 3 * D:4 * D]
    scale1 = 1.0 + mod_ref[:, 4 * D:5 * D]
    _fill_hbuf(hbuf, x_ref, xp_ref[...], xn_ref[...], g_ref[...], scale1, shift)

    is_prompt = i < N_P_TILES
    seg_first = jnp.logical_not(is_prompt & ((i % P_TILES_PER_SEQ) != 0))
    seg_last = jnp.logical_not(is_prompt & ((i % P_TILES_PER_SEQ) != P_TILES_PER_SEQ - 1))

    def conv3(cs):
        u = jnp.dot(hbuf[...], wu_ref[:, cs], preferred_element_type=F32)
        p0 = jnp.where(seg_first, 0.0, u[8:16])
        n0 = jnp.where(seg_last, 0.0, u[TM + 16:TM + 24])
        prev = jnp.concatenate([p0, u[16:TM + 8]], axis=0)
        nxt = jnp.concatenate([u[24:TM + 16], n0], axis=0)
        return (cb_ref[:, cs] + prev * cw_ref[0:1, cs] + u[16:TM + 16] * cw_ref[1:2, cs]
                + nxt * cw_ref[2:3, cs])

    for c in range(D_FF // FF_CHUNK):
        v = conv3(slice(FF_CHUNK * c, FF_CHUNK * (c + 1)))
        gg = conv3(slice(D_FF + FF_CHUNK * c, D_FF + FF_CHUNK * (c + 1)))
        act[:, FF_CHUNK * c:FF_CHUNK * (c + 1)] = (v * (gg * jax.nn.sigmoid(gg))).astype(BF16)

    out = jnp.dot(act[...], wd_ref[...], preferred_element_type=F32)
    y = x_ref[...] + _per_batch(out, mod_ref[:, 5 * D:6 * D])
    if final_norm:
        y = jnp.swapaxes(_rms(y, gf_ref[...]).reshape(STEPS, NB, D), 0, 1)

        @pl.when(i < N_P_TILES)
        def _():
            op_ref[...] = y

        @pl.when(i >= N_P_TILES)
        def _():
            os_ref[...] = y
    else:
        o_ref[...] = y


def _ffn(x, mod, g, wu, cw, cb, wd, gf, final_norm):
    prev, nxt = _halo_specs()
    tile = pl.BlockSpec((TM, D), lambda i: (i, 0))
    if final_norm:
        out_shape = (jax.ShapeDtypeStruct((2, NB, P_LEN, D), F32), jax.ShapeDtypeStruct((NB, S_LEN, D), F32))
        out_specs = _bt_specs(STEPS, 0)
    else:
        out_shape = jax.ShapeDtypeStruct((N_ROWS, D), F32)
        out_specs = tile
    return pl.pallas_call(
        functools.partial(_ffn_kernel, final_norm=final_norm),
        out_shape=out_shape,
        grid=(NT,),
        in_specs=[tile, prev, nxt, _mod_spec(lambda i: i), _const_spec((1, D)),
                  _const_spec((D, 2 * D_FF)), _const_spec((3, 2 * D_FF)), _const_spec((1, 2 * D_FF)),
                  _const_spec((D_FF, D)), _const_spec((1, D))],
        out_specs=out_specs,
        scratch_shapes=[pltpu.VMEM((TM + 2 * HALO, D), BF16), pltpu.VMEM((TM, D_FF), BF16)],
        compiler_params=_params(("arbitrary",)),
        name="conv_ffn",
    )(x, x, x, mod, g, wu, cw, cb, wd, gf)


def _s5_core(ubuf, bb_ref, cc_ref, ab_ref, slabs, hst, y_write, reverse):
    def b_piece(kt, n):
        cs = slice(S5_BN * n, S5_BN * (n + 1))
        slabs[kt, :, cs] = jnp.dot(ubuf[:, 256 * kt:256 * (kt + 1)], bb_ref[kt, :, cs],
                                   preferred_element_type=F32)

    def c_half(kt, half):
        cs = slice(D * half, D * (half + 1))
        return jnp.dot(slabs[kt, :, cs].astype(BF16), cc_ref[kt, cs, :], preferred_element_type=F32)

    n_b = SLAB // S5_BN
    every = STEPS // n_b
    quarter = STEPS // 4
    for n in range(n_b):
        b_piece(0, n)
    for kt in range(S5_KT):
        slab = slabs.at[kt]
        a_re = jnp.broadcast_to(ab_ref[kt, 0:1, :], (NB, D))
        a_im = jnp.broadcast_to(ab_ref[kt, 1:2, :], (NB, D))
        h_re = hst[kt, :, 0:D]
        h_im = hst[kt, :, D:2 * D]
        part = None
        for s in range(STEPS):
            if s % every == 0 and kt + 1 < S5_KT:
                b_piece(kt + 1, s // every)
            if s == quarter and kt >= 1:
                part = c_half(kt - 1, 0)
            if s == 3 * quarter and kt >= 1:
                y_write(kt - 1, part + c_half(kt - 1, 1))
            t = (STEPS - 1 - s) if reverse else s
            rows = slice(8 * t, 8 * (t + 1))
            n_re = a_re * h_re - a_im * h_im + slab[rows, 0:D]
            n_im = a_re * h_im + a_im * h_re + slab[rows, D:2 * D]
            slab[rows, 0:D] = n_re
            slab[rows, D:2 * D] = n_im
            h_re, h_im = n_re, n_im
        hst[kt, :, 0:D] = h_re
        hst[kt, :, D:2 * D] = h_im
    y_write(S5_KT - 1, c_half(S5_KT - 1, 0) + c_half(S5_KT - 1, 1))


def _s5_bwd_kernel(x_ref, mod_ref, g_ref, bb_ref, cc_ref, ab_ref, h0_ref,
                   yb_ref, fin_ref, ubuf, slabs, hst):
    j = NT - 1 - pl.program_id(0)

    @pl.when(_is_seq_last(j))
    def _():
        hst[...] = h0_ref[...]

    _fill_norm(ubuf, 0, x_ref, g_ref[...], 1.0 + mod_ref[:, D:2 * D], mod_ref[:, 0:D])

    def y_write(kt, val):
        yb_ref[:, 256 * kt:256 * (kt + 1)] = val

    _s5_core(ubuf, bb_ref, cc_ref, ab_ref, slabs, hst, y_write, True)
    fin_ref[...] = hst[...]


def _s5_bwd(x, mod, g, bb, cc, ab, h0):
    tile = pl.BlockSpec((TM, D), lambda i: (NT - 1 - i, 0))
    seq = pl.BlockSpec((None, S5_KT, NB, SLAB), lambda i: (_seq_id(NT - 1 - i), 0, 0, 0))
    return pl.pallas_call(
        _s5_bwd_kernel,
        out_shape=(jax.ShapeDtypeStruct((N_ROWS, D), F32),
                   jax.ShapeDtypeStruct((N_SEQ, S5_KT, NB, SLAB), F32)),
        grid=(NT,),
        in_specs=[tile, _mod_spec(lambda i: NT - 1 - i), _const_spec((1, D)),
                  _const_spec((S5_KT, 256, SLAB)), _const_spec((S5_KT, SLAB, 256)),
                  _const_spec((S5_KT, 2, D)), seq],
        out_specs=(tile, seq),
        scratch_shapes=[pltpu.VMEM((TM, D), BF16), pltpu.VMEM((S5_KT, TM, SLAB), F32),
                        pltpu.VMEM((S5_KT, NB, SLAB), F32)],
        compiler_params=_params(("arbitrary",)),
        name="s5_bwd",
    )(x, mod, g, bb, cc, ab, h0)


def _s5_fwd_kernel(x_ref, yb_ref, mod_ref, g_ref, bb_ref, cc_ref, ab_ref, h0_ref, dsk_ref, wglu_ref,
                   x1_ref, fin_ref, ubuf, zbuf, u_s, slabs, hst):
    j = pl.program_id(0)

    @pl.when(_is_seq_first(j))
    def _():
        hst[...] = h0_ref[...]

    _fill_norm(ubuf, 0, x_ref, g_ref[...], 1.0 + mod_ref[:, D:2 * D], mod_ref[:, 0:D], u_s)

    def y_write(kt, val):
        cs = slice(256 * kt, 256 * (kt + 1))
        y = dsk_ref[:, cs] * u_s[:, cs] + yb_ref[:, cs] + val
        zbuf[:, cs] = jax.nn.gelu(y).astype(BF16)

    _s5_core(ubuf, bb_ref, cc_ref, ab_ref, slabs, hst, y_write, False)
    fin_ref[...] = hst[...]

    for c in range(D // COL_CHUNK):
        cs = slice(COL_CHUNK * c, COL_CHUNK * (c + 1))
        v = jnp.dot(zbuf[...], wglu_ref[:, cs], preferred_element_type=F32)
        gg = jnp.dot(zbuf[...], wglu_ref[:, D + COL_CHUNK * c:D + COL_CHUNK * (c + 1)], preferred_element_type=F32)
        out = v * jax.nn.sigmoid(gg)
        x1_ref[:, cs] = x_ref[:, cs] + _per_batch(out, mod_ref[:, 2 * D + COL_CHUNK * c:2 * D + COL_CHUNK * (c + 1)])


def _s5_fwd(x, yb, mod, g, bb, cc, ab, h0, dsk, wglu):
    tile = pl.BlockSpec((TM, D), lambda i: (i, 0))
    seq = pl.BlockSpec((None, S5_KT, NB, SLAB), lambda i: (_seq_id(i), 0, 0, 0))
    return pl.pallas_call(
        _s5_fwd_kernel,
        out_shape=(jax.ShapeDtypeStruct((N_ROWS, D), F32),
                   jax.ShapeDtypeStruct((N_SEQ, S5_KT, NB, SLAB), F32)),
        grid=(NT,),
        in_specs=[tile, tile, _mod_spec(lambda i: i), _const_spec((1, D)),
                  _const_spec((S5_KT, 256, SLAB)), _const_spec((S5_KT, SLAB, 256)),
                  _const_spec((S5_KT, 2, D)), seq, _const_spec((1, D)), _const_spec((D, 2 * D))],
        out_specs=(tile, seq),
        scratch_shapes=[pltpu.VMEM((TM, D), BF16), pltpu.VMEM((TM, D), BF16), pltpu.VMEM((TM, D), F32),
                        pltpu.VMEM((S5_KT, TM, SLAB), F32), pltpu.VMEM((S5_KT, NB, SLAB), F32)],
        compiler_params=_params(("arbitrary",)),
        name="s5_fwd",
    )(x, yb, mod, g, bb, cc, ab, h0, dsk, wglu)


def _pair_blockdiag(w_a, w_i):
    same = (jnp.arange(2)[:, None, None, None] == jnp.arange(2)[None, None, :, None])

    def bd(w):
        w4 = w.astype(F32).reshape(8, 2, 64, 1, 64)
        return jnp.where(same[None], w4, 0.0).reshape(8, 128, 128)

    return jnp.concatenate([bd(w_a), bd(w_i)], axis=-1).astype(BF16)


def _s5_dir_params(a_re, a_im, log_dt, b_re, b_im, c_re, c_im):
    l_re = a_re.astype(F32)
    l_im = a_im.astype(F32)
    dt = jnp.exp(log_dt.astype(F32))[:, None]
    mag = jnp.exp(l_re * dt)
    ab_re = mag * jnp.cos(l_im * dt)
    ab_im = mag * jnp.sin(l_im * dt)
    den = l_re * l_re + l_im * l_im
    k_re = ((ab_re - 1.0) * l_re + ab_im * l_im) / den
    k_im = (ab_im * l_re - (ab_re - 1.0) * l_im) / den
    br = b_re.astype(F32)
    bi = b_im.astype(F32)
    bb_re = br * k_re[..., None] - bi * k_im[..., None]
    bb_im = br * k_im[..., None] + bi * k_re[..., None]
    same = (jnp.arange(16)[:, None, None, None] == jnp.arange(16)[None, None, :, None])[None]

    def b_blk(m):
        m5 = m.reshape(S5_KT, 16, 64, 16).transpose(0, 1, 3, 2)[:, :, :, None, :]
        return jnp.where(same, m5, 0.0).reshape(S5_KT, 256, 1024)

    def c_blk(m):
        m5 = m.reshape(S5_KT, 16, 16, 64).transpose(0, 1, 3, 2)[:, :, :, None, :]
        return jnp.where(same, m5, 0.0).reshape(S5_KT, 1024, 256)

    bb = jnp.concatenate([b_blk(bb_re), b_blk(bb_im)], axis=-1).astype(BF16)
    cc = jnp.concatenate([c_blk(c_re.astype(F32)), c_blk(-c_im.astype(F32))], axis=1).astype(BF16)
    ab = jnp.stack([ab_re.reshape(S5_KT, D), ab_im.reshape(S5_KT, D)], axis=1)
    return bb, cc, ab


def _s5_state_to_slab(s_re, s_im):
    re = s_re.astype(F32).reshape(NB, S5_KT, D).transpose(1, 0, 2)
    im = s_im.astype(F32).reshape(NB, S5_KT, D).transpose(1, 0, 2)
    return jnp.concatenate([re, im], axis=-1)


def _s5_slab_to_state(fin):
    re = fin[..., 0:D].transpose(0, 2, 1, 3).reshape(2 * NB, 64, 64)
    im = fin[..., D:2 * D].transpose(0, 2, 1, 3).reshape(2 * NB, 64, 64)
    return re, im


def kernel(x_prompt, x_sample, state_lru, state_s5_re, state_s5_im, c, c_ctx, ada_w, ada_b, norm_mix, norm_ffn, norm_final, lru_w_in, lru_conv_w, lru_conv_b, lru_w_a, lru_b_a, lru_w_i, lru_b_i, lru_lambda, lru_w_out, s5_a_re, s5_a_im, s5_log_dt, s5_b_re, s5_b_im, s5_c_re, s5_c_im, s5_d, s5_w_glu, ffn_w_up, ffn_conv_w, ffn_conv_b, ffn_w_down):
    c_all = jnp.concatenate([c.astype(F32), c_ctx.astype(F32)[None], jnp.zeros((7, D), F32)], axis=0)
    mods = _mod_vectors(c_all, ada_w.astype(F32), ada_b.astype(F32)[:, None, :])

    def mod_of(l):
        return jnp.stack([jnp.broadcast_to(mods[l, 8][None], (NB, N_MOD * D)), mods[l, 0:NB]], axis=0)

    xp4 = x_prompt.astype(F32).reshape(2, NB, P_LEN, D)
    gf = norm_final[None].astype(F32)

    mod0 = mod_of(0)
    gate, xr, x = _lru_in(xp4, x_sample.astype(F32), mod0, norm_mix[0][None].astype(F32), lru_w_in[0].astype(BF16),
                          lru_conv_w[0].astype(F32), lru_conv_b[0][None].astype(F32))
    zeros_h = jnp.zeros((NB, D), F32)
    sp = jax.nn.softplus(-lru_lambda[0].astype(F32)) * (-LRU_C * LOG2_E)
    h0_f = jnp.stack([zeros_h, zeros_h, state_lru[:, 0, 0].astype(F32)], axis=0)
    h0_b = jnp.stack([zeros_h, zeros_h, state_lru[:, 0, 1].astype(F32)], axis=0)
    hsb, fin_b = _lru_bwd(xr, _pair_blockdiag(lru_w_a[0, 1], lru_w_i[0, 1]),
                          lru_b_a[0, 1][None].astype(F32), lru_b_i[0, 1][None].astype(F32), sp[1][None], h0_b)
    x, fin_f = _lru_fwd(xr, gate, hsb, x, mod0, _pair_blockdiag(lru_w_a[0, 0], lru_w_i[0, 0]),
                        lru_b_a[0, 0][None].astype(F32), lru_b_i[0, 0][None].astype(F32), sp[0][None], h0_f,
                        lru_w_out[0].astype(BF16))
    x = _ffn(x, mod0, norm_ffn[0][None].astype(F32), ffn_w_up[0].astype(BF16), ffn_conv_w[0].astype(F32),
             ffn_conv_b[0][None].astype(F32), ffn_w_down[0].astype(BF16), gf, False)
    new_lru = jnp.stack([fin_f[0:2].reshape(2 * NB, D), fin_b[0:2].reshape(2 * NB, D)], axis=1)[:, None]

    mod1 = mod_of(1)
    g1 = norm_mix[1][None].astype(F32)
    zeros_s = jnp.zeros((S5_KT, NB, SLAB), F32)
    dirs = []
    for d in range(2):
        bb, cc, ab = _s5_dir_params(s5_a_re[0, d], s5_a_im[0, d], s5_log_dt[0, d], s5_b_re[0, d], s5_b_im[0, d],
                                    s5_c_re[0, d], s5_c_im[0, d])
        h0 = jnp.stack([zeros_s, zeros_s, _s5_state_to_slab(state_s5_re[:, 0, d], state_s5_im[:, 0, d])], axis=0)
        dirs.append((bb, cc, ab, h0))
    yb, fin_sb = _s5_bwd(x, mod1, g1, *dirs[1])
    x, fin_sf = _s5_fwd(x, yb, mod1, g1, *dirs[0], s5_d[0][None].astype(F32), s5_w_glu[0].astype(BF16))
    y_p4, y_sample = _ffn(x, mod1, norm_ffn[1][None].astype(F32), ffn_w_up[1].astype(BF16), ffn_conv_w[1].astype(F32),
             ffn_conv_b[1][None].astype(F32), ffn_w_down[1].astype(BF16), gf, True)
    f_re, f_im = _s5_slab_to_state(fin_sf[0:2])
    b_re, b_im = _s5_slab_to_state(fin_sb[0:2])
    new_s5_re = jnp.stack([f_re, b_re], axis=1)[:, None]
    new_s5_im = jnp.stack([f_im, b_im], axis=1)[:, None]

    return (y_p4.reshape(2 * NB, P_LEN, D), y_sample, new_lru, new_s5_re, new_s5_im)
```

```python
import functools

import jax
import jax.numpy as jnp
from jax import lax
from jax.experimental import pallas as pl
from jax.experimental.pallas import tpu as pltpu

F32 = jnp.float32
BF16 = jnp.bfloat16

D = 1024
D_FF = 2816
N_MOD = 6
EPS = 1e-6
LRU_C = 8.0
LOG2_E = 1.4426950408889634
NB = 8
STEPS = 64
TM = STEPS * NB
HALO = 16
NORM_ROWS = 64
FF_CHUNK = 256
COL_CHUNK = 256
P_LEN = 256
S_LEN = 4096
P_TILES_PER_SEQ = P_LEN // STEPS
N_P_TILES = 2 * P_TILES_PER_SEQ
N_S_TILES = S_LEN // STEPS
NT = N_P_TILES + N_S_TILES
N_ROWS = NT * TM
N_SEQ = 3
S5_KT = 4
SLAB = 2048
S5_BN = 256
VMEM_LIMIT = 56 * 1024 * 1024


def _seq_id(j):
    return jnp.where(j >= N_P_TILES, 2, jnp.where(j >= P_TILES_PER_SEQ, 1, 0))


def _is_seq_first(j):
    return (j == 0) | (j == P_TILES_PER_SEQ) | (j == N_P_TILES)


def _is_seq_last(j):
    return (j == P_TILES_PER_SEQ - 1) | (j == N_P_TILES - 1) | (j == NT - 1)


def _params(sem):
    return pltpu.CompilerParams(dimension_semantics=sem, vmem_limit_bytes=VMEM_LIMIT)


def _const_spec(shape):
    n = len(shape)
    return pl.BlockSpec(shape, lambda *_: (0,) * n, pipeline_mode=pl.Buffered(1))


def _mod_spec(tile_of):
    return pl.BlockSpec((None, NB, N_MOD * D),
                        lambda i: (jnp.where(tile_of(i) >= N_P_TILES, 1, 0), 0, 0))


def _rms(xv, g):
    ms = jnp.mean(xv * xv, axis=-1, keepdims=True)
    return xv * lax.rsqrt(ms + EPS) * g


def _per_batch(y, vec):
    r, n = y.shape
    return (y.reshape(r // NB, NB, n) * vec[None]).reshape(r, n)


def _norm_mod(xv, g, scale1, shift):
    r = xv.shape[0]
    y = _rms(xv, g).reshape(r // NB, NB, D) * scale1[None] + shift[None]
    return y.reshape(r, D)


def _fill_norm(dst, dst_off, x_ref, g, scale1, shift, f32_dst=None):
    for k in range(TM // NORM_ROWS):
        r0 = k * NORM_ROWS
        y = _norm_mod(x_ref[r0:r0 + NORM_ROWS], g, scale1, shift)
        dst[dst_off + r0:dst_off + r0 + NORM_ROWS] = y.astype(BF16)
        if f32_dst is not None:
            f32_dst[r0:r0 + NORM_ROWS] = y


def _fill_hbuf(hbuf, x_ref, x_prev, x_next, g, scale1, shift):
    hbuf[0:HALO] = _norm_mod(x_prev, g, scale1, shift).astype(BF16)
    hbuf[HALO + TM:HALO + TM + HALO] = _norm_mod(x_next, g, scale1, shift).astype(BF16)
    _fill_norm(hbuf, HALO, x_ref, g, scale1, shift)


def _mod_kernel(c_ref, w_ref, b_ref, o_ref):
    cv = c_ref[...]
    s = (cv * jax.nn.sigmoid(cv)).astype(BF16)
    o_ref[...] = jnp.dot(s, w_ref[...].astype(BF16), preferred_element_type=F32) + b_ref[...]


def _mod_vectors(c_all, ada_w, ada_b):
    depth = ada_w.shape[0]
    return pl.pallas_call(
        _mod_kernel,
        out_shape=jax.ShapeDtypeStruct((depth, 16, N_MOD * D), F32),
        grid=(depth, N_MOD),
        in_specs=[
            pl.BlockSpec((16, D), lambda l, n: (0, 0)),
            pl.BlockSpec((None, D, D), lambda l, n: (l, 0, n)),
            pl.BlockSpec((None, 1, D), lambda l, n: (l, 0, n)),
        ],
        out_specs=pl.BlockSpec((None, 16, D), lambda l, n: (l, 0, n)),
        compiler_params=_params(("arbitrary", "arbitrary")),
        name="mod_vectors",
    )(c_all, ada_w, ada_b)


def _halo_specs():
    prev = pl.BlockSpec((HALO, D), lambda i: (jnp.maximum(i * (TM // HALO) - 1, 0), 0))
    nxt = pl.BlockSpec((HALO, D), lambda i: (jnp.minimum((i + 1) * (TM // HALO), N_ROWS // HALO - 1), 0))
    return prev, nxt


def _bt_specs(steps, offset):
    per_tile = STEPS // steps

    def prompt_idx(i):
        ip = jnp.minimum(i, N_P_TILES - 1)
        blk = jnp.clip((ip % P_TILES_PER_SEQ) * per_tile + offset, 0, P_LEN // steps - 1)
        return (ip // P_TILES_PER_SEQ, 0, blk, 0)

    def sample_idx(i):
        blk = jnp.clip((i - N_P_TILES) * per_tile + offset, 0, S_LEN // steps - 1)
        return (0, blk, 0)

    return (pl.BlockSpec((None, NB, steps, D), prompt_idx), pl.BlockSpec((NB, steps, D), sample_idx))


def _to_rows(x_bt):
    n = x_bt.shape[1]
    return jnp.swapaxes(x_bt, 0, 1).reshape(n * NB, D)


def _lru_in_kernel(xa_ref, xap_ref, xan_ref, xb_ref, xbp_ref, xbn_ref, mod_ref, g_ref, wi_ref, cw_ref, cb_ref,
                   gate_ref, xr_ref, xc_ref, hbuf):
    i = pl.program_id(0)
    shift = mod_ref[:, 0:D]
    scale1 = 1.0 + mod_ref[:, D:2 * D]

    is_p = i < N_P_TILES
    xc_ref[...] = _to_rows(jnp.where(is_p, xa_ref[...], xb_ref[...]))
    x_prev = _to_rows(jnp.where(is_p, xap_ref[...], xbp_ref[...]))[NB * NB - HALO:NB * NB]
    x_next = _to_rows(jnp.where(is_p, xan_ref[...], xbn_ref[...]))[0:HALO]
    _fill_hbuf(hbuf, xc_ref, x_prev, x_next, g_ref[...], scale1, shift)

    first = _is_seq_first(i)
    last = _is_seq_last(i)

    for c in range(D // COL_CHUNK):
        cs = slice(COL_CHUNK * c, COL_CHUNK * (c + 1))
        cx = slice(D + COL_CHUNK * c, D + COL_CHUNK * (c + 1))
        gate_ref[:, cs] = jax.nn.gelu(
            jnp.dot(hbuf[HALO:HALO + TM], wi_ref[:, cs], preferred_element_type=F32)).astype(BF16)
        u = jnp.dot(hbuf[...], wi_ref[:, cx], preferred_element_type=F32)
        head = jnp.where(first, 0.0, u[0:16])
        tail = jnp.where(last, 0.0, u[TM + 16:TM + 24])
        m2 = jnp.concatenate([head, u[16:TM]], axis=0)
        m1 = jnp.concatenate([head[8:16], u[16:TM + 8]], axis=0)
        p1 = jnp.concatenate([u[24:TM + 16], tail], axis=0)
        xr_ref[:, cs] = (cb_ref[:, cs] + m2 * cw_ref[0:1, cs] + m1 * cw_ref[1:2, cs]
                         + u[16:TM + 16] * cw_ref[2:3, cs] + p1 * cw_ref[3:4, cs])


def _lru_in(xp4, xs, mod, g, wi, cw, cb):
    p_tile, s_tile = _bt_specs(STEPS, 0)
    p_prev, s_prev = _bt_specs(NB, -1)
    p_next, s_next = _bt_specs(NB, STEPS // NB)
    tile = pl.BlockSpec((TM, D), lambda i: (i, 0))
    rows = jax.ShapeDtypeStruct((N_ROWS, D), F32)
    return pl.pallas_call(
        _lru_in_kernel,
        out_shape=(jax.ShapeDtypeStruct((N_ROWS, D), BF16), rows, rows),
        grid=(NT,),
        in_specs=[p_tile, p_prev, p_next, s_tile, s_prev, s_next,
                  _mod_spec(lambda i: i), _const_spec((1, D)),
                  _const_spec((D, 2 * D)), _const_spec((4, D)), _const_spec((1, D))],
        out_specs=(tile, tile, tile),
        scratch_shapes=[pltpu.VMEM((TM + 2 * HALO, D), BF16)],
        compiler_params=_params(("arbitrary",)),
        name="lru_in",
    )(xp4, xp4, xp4, xs, xs, xs, mod, g, wi, cw, cb)


def _lru_gates(xr_ref, wp_ref, ba_ref, bi_ref, sp_ref, a_s, bx_s, xb_s):
    xb_s[...] = xr_ref[...].astype(BF16)
    for p in range(8):
        cols = slice(128 * p, 128 * (p + 1))
        z = jnp.dot(xb_s[:, cols], wp_ref[p], preferred_element_type=F32)
        r = jax.nn.sigmoid(z[:, 0:128] + ba_ref[:, cols])
        ig = jax.nn.sigmoid(z[:, 128:256] + bi_ref[:, cols])
        a = jnp.exp2(r * sp_ref[:, cols])
        a_s[:, cols] = a
        bx_s[:, cols] = jnp.sqrt(1.0 - a * a) * (ig * xr_ref[:, cols])


def _lru_scan(a_s, bx_s, h_s, out_ref, reverse):
    h = h_s[...]
    for s in range(STEPS):
        t = (STEPS - 1 - s) if reverse else s
        rows = slice(8 * t, 8 * (t + 1))
        h = a_s[rows] * h + bx_s[rows]
        out_ref[rows] = h
    h_s[...] = h


def _lru_bwd_kernel(xr_ref, wp_ref, ba_ref, bi_ref, sp_ref, h0_ref,
                    hs_ref, fin_ref, a_s, bx_s, xb_s, h_s):
    j = NT - 1 - pl.program_id(0)

    @pl.when(_is_seq_last(j))
    def _():
        h_s[...] = h0_ref[...]

    _lru_gates(xr_ref, wp_ref, ba_ref, bi_ref, sp_ref, a_s, bx_s, xb_s)
    _lru_scan(a_s, bx_s, h_s, bx_s, True)
    hs_ref[...] = bx_s[...].astype(BF16)
    fin_ref[...] = h_s[...]


def _lru_scratch():
    return [pltpu.VMEM((TM, D), F32), pltpu.VMEM((TM, D), F32), pltpu.VMEM((TM, D), BF16),
            pltpu.VMEM((NB, D), F32)]


def _lru_bwd(xr, wp, ba, bi, sp, h0):
    tile = pl.BlockSpec((TM, D), lambda i: (NT - 1 - i, 0))
    seq = pl.BlockSpec((None, NB, D), lambda i: (_seq_id(NT - 1 - i), 0, 0))
    return pl.pallas_call(
        _lru_bwd_kernel,
        out_shape=(jax.ShapeDtypeStruct((N_ROWS, D), BF16), jax.ShapeDtypeStruct((N_SEQ, NB, D), F32)),
        grid=(NT,),
        in_specs=[tile, _const_spec((8, 128, 256)), _const_spec((1, D)), _const_spec((1, D)),
                  _const_spec((1, D)), seq],
        out_specs=(tile, seq),
        scratch_shapes=_lru_scratch(),
        compiler_params=_params(("arbitrary",)),
        name="lru_bwd",
    )(xr, wp, ba, bi, sp, h0)


def _lru_fwd_kernel(xr_ref, gate_ref, hsb_ref, x_ref, mod_ref, wp_ref, ba_ref, bi_ref, sp_ref, h0_ref,
                    wo_ref, x1_ref, fin_ref, a_s, bx_s, xb_s, h_s):
    j = pl.program_id(0)

    @pl.when(_is_seq_first(j))
    def _():
        h_s[...] = h0_ref[...]

    _lru_gates(xr_ref, wp_ref, ba_ref, bi_ref, sp_ref, a_s, bx_s, xb_s)
    _lru_scan(a_s, bx_s, h_s, bx_s, False)
    fin_ref[...] = h_s[...]

    xb_s[...] = ((bx_s[...] + hsb_ref[...].astype(F32)) * gate_ref[...].astype(F32)).astype(BF16)
    for c in range(D // COL_CHUNK):
        cs = slice(COL_CHUNK * c, COL_CHUNK * (c + 1))
        out = jnp.dot(xb_s[...], wo_ref[:, cs], preferred_element_type=F32)
        x1_ref[:, cs] = x_ref[:, cs] + _per_batch(out, mod_ref[:, 2 * D + COL_CHUNK * c:2 * D + COL_CHUNK * (c + 1)])


def _lru_fwd(xr, gate, hsb, x, mod, wp, ba, bi, sp, h0, wo):
    tile = pl.BlockSpec((TM, D), lambda i: (i, 0))
    seq = pl.BlockSpec((None, NB, D), lambda i: (_seq_id(i), 0, 0))
    return pl.pallas_call(
        _lru_fwd_kernel,
        out_shape=(jax.ShapeDtypeStruct((N_ROWS, D), F32), jax.ShapeDtypeStruct((N_SEQ, NB, D), F32)),
        grid=(NT,),
        in_specs=[tile, tile, tile, tile, _mod_spec(lambda i: i), _const_spec((8, 128, 256)),
                  _const_spec((1, D)), _const_spec((1, D)), _const_spec((1, D)), seq, _const_spec((D, D))],
        out_specs=(tile, seq),
        scratch_shapes=_lru_scratch(),
        compiler_params=_params(("arbitrary",)),
        name="lru_fwd",
    )(xr, gate, hsb, x, mod, wp, ba, bi, sp, h0, wo)


def _ffn_kernel(x_ref, xp_ref, xn_ref, mod_ref, g_ref, wu_ref, cw_ref, cb_ref, wd_ref, gf_ref,
                *rest, final_norm):
    if final_norm:
        op_ref, os_ref, hbuf, act = rest
    else:
        o_ref, hbuf, act = rest
    i = pl.program_id(0)
    shift = mod_ref[:, 3 * D:4 * D]
    scale1 = 1.0 + mod_ref[:, 4 * D:5 * D]
    _fill_hbuf(hbuf, x_ref, xp_ref[...], xn_ref[...], g_ref[...], scale1, shift)

    is_prompt = i < N_P_TILES
    seg_first = jnp.logical_not(is_prompt & ((i % P_TILES_PER_SEQ) != 0))
    seg_last = jnp.logical_not(is_prompt & ((i % P_TILES_PER_SEQ) != P_TILES_PER_SEQ - 1))

    def conv3(cs):
        u = jnp.dot(hbuf[...], wu_ref[:, cs], preferred_element_type=F32)
        p0 = jnp.where(seg_first, 0.0, u[8:16])
        n0 = jnp.where(seg_last, 0.0, u[TM + 16:TM + 24])
        prev = jnp.concatenate([p0, u[16:TM + 8]], axis=0)
        nxt = jnp.concatenate([u[24:TM + 16], n0], axis=0)
        return (cb_ref[:, cs] + prev * cw_ref[0:1, cs] + u[16:TM + 16] * cw_ref[1:2, cs]
                + nxt * cw_ref[2:3, cs])

    for c in range(D_FF // FF_CHUNK):
        v = conv3(slice(FF_CHUNK * c, FF_CHUNK * (c + 1)))
        gg = conv3(slice(D_FF + FF_CHUNK * c, D_FF + FF_CHUNK * (c + 1)))
        act[:, FF_CHUNK * c:FF_CHUNK * (c + 1)] = (v * (gg * jax.nn.sigmoid(gg))).astype(BF16)

    out = jnp.dot(act[...], wd_ref[...], preferred_element_type=F32)
    y = x_ref[...] + _per_batch(out, mod_ref[:, 5 * D:6 * D])
    if final_norm:
        y = jnp.swapaxes(_rms(y, gf_ref[...]).reshape(STEPS, NB, D), 0, 1)

        @pl.when(i < N_P_TILES)
        def _():
            op_ref[...] = y

        @pl.when(i >= N_P_TILES)
        def _():
            os_ref[...] = y
    else:
        o_ref[...] = y


def _ffn(x, mod, g, wu, cw, cb, wd, gf, final_norm):
    prev, nxt = _halo_specs()
    tile = pl.BlockSpec((TM, D), lambda i: (i, 0))
    if final_norm:
        out_shape = (jax.ShapeDtypeStruct((2, NB, P_LEN, D), F32), jax.ShapeDtypeStruct((NB, S_LEN, D), F32))
        out_specs = _bt_specs(STEPS, 0)
    else:
        out_shape = jax.ShapeDtypeStruct((N_ROWS, D), F32)
        out_specs = tile
    return pl.pallas_call(
        functools.partial(_ffn_kernel, final_norm=final_norm),
        out_shape=out_shape,
        grid=(NT,),
        in_specs=[tile, prev, nxt, _mod_spec(lambda i: i), _const_spec((1, D)),
                  _const_spec((D, 2 * D_FF)), _const_spec((3, 2 * D_FF)), _const_spec((1, 2 * D_FF)),
                  _const_spec((D_FF, D)), _const_spec((1, D))],
        out_specs=out_specs,
        scratch_shapes=[pltpu.VMEM((TM + 2 * HALO, D), BF16), pltpu.VMEM((TM, D_FF), BF16)],
        compiler_params=_params(("arbitrary",)),
        name="conv_ffn",
    )(x, x, x, mod, g, wu, cw, cb, wd, gf)


def _s5_core(ubuf, bb_ref, cc_ref, ab_ref, slabs, hs16, hst, y_write, reverse):
    def b_piece(kt, n):
        cs = slice(S5_BN * n, S5_BN * (n + 1))
        slabs[kt % 2, :, cs] = jnp.dot(ubuf[:, 256 * kt:256 * (kt + 1)], bb_ref[kt, :, cs],
                                       preferred_element_type=F32)

    def c_half(kt, half):
        cs = slice(D * half, D * (half + 1))
        return jnp.dot(hs16[kt % 2, :, cs], cc_ref[kt, cs, :], preferred_element_type=F32)

    n_b = SLAB // S5_BN
    every = STEPS // n_b
    quarter = STEPS // 4
    for n in range(n_b):
        b_piece(0, n)
    for kt in range(S5_KT):
        slab = slabs.at[kt % 2]
        hs = hs16.at[kt % 2]
        a_re = jnp.broadcast_to(ab_ref[kt, 0:1, :], (NB, D))
        a_im = jnp.broadcast_to(ab_ref[kt, 1:2, :], (NB, D))
        h_re = hst[kt, :, 0:D]
        h_im = hst[kt, :, D:2 * D]
        part = None
        for s in range(STEPS):
            if s % every == 0 and kt + 1 < S5_KT:
                b_piece(kt + 1, s // every)
            if s == quarter and kt >= 1:
                part = c_half(kt - 1, 0)
            if s == 3 * quarter and kt >= 1:
                y_write(kt - 1, part + c_half(kt - 1, 1))
            t = (STEPS - 1 - s) if reverse else s
            rows = slice(8 * t, 8 * (t + 1))
            n_re = a_re * h_re - a_im * h_im + slab[rows, 0:D]
            n_im = a_re * h_im + a_im * h_re + slab[rows, D:2 * D]
            if s % 2 == 1:
                lo = min(t, t + 1 if reverse else t - 1)
                pair = slice(8 * lo, 8 * lo + 16)
                first, second = ((n_re, h_re), (n_im, h_im)) if reverse else ((h_re, n_re), (h_im, n_im))
                hs[pair, 0:D] = jnp.concatenate(first, axis=0).astype(BF16)
                hs[pair, D:2 * D] = jnp.concatenate(second, axis=0).astype(BF16)
            h_re, h_im = n_re, n_im
        hst[kt, :, 0:D] = h_re
        hst[kt, :, D:2 * D] = h_im
    y_write(S5_KT - 1, c_half(S5_KT - 1, 0) + c_half(S5_KT - 1, 1))


def _s5_bwd_kernel(x_ref, mod_ref, g_ref, bb_ref, cc_ref, ab_ref, h0_ref,
                   yb_ref, fin_ref, ubuf, slabs, hs16, hst):
    j = NT - 1 - pl.program_id(0)

    @pl.when(_is_seq_last(j))
    def _():
        hst[...] = h0_ref[...]

    _fill_norm(ubuf, 0, x_ref, g_ref[...], 1.0 + mod_ref[:, D:2 * D], mod_ref[:, 0:D])

    def y_write(kt, val):
        yb_ref[:, 256 * kt:256 * (kt + 1)] = val

    _s5_core(ubuf, bb_ref, cc_ref, ab_ref, slabs, hs16, hst, y_write, True)
    fin_ref[...] = hst[...]


def _s5_bwd(x, mod, g, bb, cc, ab, h0):
    tile = pl.BlockSpec((TM, D), lambda i: (NT - 1 - i, 0))
    seq = pl.BlockSpec((None, S5_KT, NB, SLAB), lambda i: (_seq_id(NT - 1 - i), 0, 0, 0))
    return pl.pallas_call(
        _s5_bwd_kernel,
        out_shape=(jax.ShapeDtypeStruct((N_ROWS, D), F32),
                   jax.ShapeDtypeStruct((N_SEQ, S5_KT, NB, SLAB), F32)),
        grid=(NT,),
        in_specs=[tile, _mod_spec(lambda i: NT - 1 - i), _const_spec((1, D)),
                  _const_spec((S5_KT, 256, SLAB)), _const_spec((S5_KT, SLAB, 256)),
                  _const_spec((S5_KT, 2, D)), seq],
        out_specs=(tile, seq),
        scratch_shapes=[pltpu.VMEM((TM, D), BF16), pltpu.VMEM((2, TM, SLAB), F32), pltpu.VMEM((2, TM, SLAB), BF16),
                        pltpu.VMEM((S5_KT, NB, SLAB), F32)],
        compiler_params=_params(("arbitrary",)),
        name="s5_bwd",
    )(x, mod, g, bb, cc, ab, h0)


def _s5_fwd_kernel(x_ref, yb_ref, mod_ref, g_ref, bb_ref, cc_ref, ab_ref, h0_ref, dsk_ref, wglu_ref,
                   x1_ref, fin_ref, ubuf, zbuf, u_s, slabs, hs16, hst):
    j = pl.program_id(0)

    @pl.when(_is_seq_first(j))
    def _():
        hst[...] = h0_ref[...]

    _fill_norm(ubuf, 0, x_ref, g_ref[...], 1.0 + mod_ref[:, D:2 * D], mod_ref[:, 0:D], u_s)

    def y_write(kt, val):
        cs = slice(256 * kt, 256 * (kt + 1))
        y = dsk_ref[:, cs] * u_s[:, cs] + yb_ref[:, cs] + val
        zbuf[:, cs] = jax.nn.gelu(y).astype(BF16)

    _s5_core(ubuf, bb_ref, cc_ref, ab_ref, slabs, hs16, hst, y_write, False)
    fin_ref[...] = hst[...]

    for c in range(D // COL_CHUNK):
        cs = slice(COL_CHUNK * c, COL_CHUNK * (c + 1))
        v = jnp.dot(zbuf[...], wglu_ref[:, cs], preferred_element_type=F32)
        gg = jnp.dot(zbuf[...], wglu_ref[:, D + COL_CHUNK * c:D + COL_CHUNK * (c + 1)], preferred_element_type=F32)
        out = v * jax.nn.sigmoid(gg)
        x1_ref[:, cs] = x_ref[:, cs] + _per_batch(out, mod_ref[:, 2 * D + COL_CHUNK * c:2 * D + COL_CHUNK * (c + 1)])


def _s5_fwd(x, yb, mod, g, bb, cc, ab, h0, dsk, wglu):
    tile = pl.BlockSpec((TM, D), lambda i: (i, 0))
    seq = pl.BlockSpec((None, S5_KT, NB, SLAB), lambda i: (_seq_id(i), 0, 0, 0))
    return pl.pallas_call(
        _s5_fwd_kernel,
        out_shape=(jax.ShapeDtypeStruct((N_ROWS, D), F32),
                   jax.ShapeDtypeStruct((N_SEQ, S5_KT, NB, SLAB), F32)),
        grid=(NT,),
        in_specs=[tile, tile, _mod_spec(lambda i: i), _const_spec((1, D)),
                  _const_spec((S5_KT, 256, SLAB)), _const_spec((S5_KT, SLAB, 256)),
                  _const_spec((S5_KT, 2, D)), seq, _const_spec((1, D)), _const_spec((D, 2 * D))],
        out_specs=(tile, seq),
        scratch_shapes=[pltpu.VMEM((TM, D), BF16), pltpu.VMEM((TM, D), BF16), pltpu.VMEM((TM, D), F32),
                        pltpu.VMEM((2, TM, SLAB), F32), pltpu.VMEM((2, TM, SLAB), BF16),
                        pltpu.VMEM((S5_KT, NB, SLAB), F32)],
        compiler_params=_params(("arbitrary",)),
        name="s5_fwd",
    )(x, yb, mod, g, bb, cc, ab, h0, dsk, wglu)


def _pair_blockdiag(w_a, w_i):
    same = (jnp.arange(2)[:, None, None, None] == jnp.arange(2)[None, None, :, None])

    def bd(w):
        w4 = w.astype(F32).reshape(8, 2, 64, 1, 64)
        return jnp.where(same[None], w4, 0.0).reshape(8, 128, 128)

    return jnp.concatenate([bd(w_a), bd(w_i)], axis=-1).astype(BF16)


def _s5_dir_params(a_re, a_im, log_dt, b_re, b_im, c_re, c_im):
    l_re = a_re.astype(F32)
    l_im = a_im.astype(F32)
    dt = jnp.exp(log_dt.astype(F32))[:, None]
    mag = jnp.exp(l_re * dt)
    ab_re = mag * jnp.cos(l_im * dt)
    ab_im = mag * jnp.sin(l_im * dt)
    den = l_re * l_re + l_im * l_im
    k_re = ((ab_re - 1.0) * l_re + ab_im * l_im) / den
    k_im = (ab_im * l_re - (ab_re - 1.0) * l_im) / den
    br = b_re.astype(F32)
    bi = b_im.astype(F32)
    bb_re = br * k_re[..., None] - bi * k_im[..., None]
    bb_im = br * k_im[..., None] + bi * k_re[..., None]
    same = (jnp.arange(16)[:, None, None, None] == jnp.arange(16)[None, None, :, None])[None]

    def b_blk(m):
        m5 = m.reshape(S5_KT, 16, 64, 16).transpose(0, 1, 3, 2)[:, :, :, None, :]
        return jnp.where(same, m5, 0.0).reshape(S5_KT, 256, 1024)

    def c_blk(m):
        m5 = m.reshape(S5_KT, 16, 16, 64).transpose(0, 1, 3, 2)[:, :, :, None, :]
        return jnp.where(same, m5, 0.0).reshape(S5_KT, 1024, 256)

    bb = jnp.concatenate([b_blk(bb_re), b_blk(bb_im)], axis=-1).astype(BF16)
    cc = jnp.concatenate([c_blk(c_re.astype(F32)), c_blk(-c_im.astype(F32))], axis=1).astype(BF16)
    ab = jnp.stack([ab_re.reshape(S5_KT, D), ab_im.reshape(S5_KT, D)], axis=1)
    return bb, cc, ab


def _s5_state_to_slab(s_re, s_im):
    re = s_re.astype(F32).reshape(NB, S5_KT, D).transpose(1, 0, 2)
    im = s_im.astype(F32).reshape(NB, S5_KT, D).transpose(1, 0, 2)
    return jnp.concatenate([re, im], axis=-1)


def _s5_slab_to_state(fin):
    re = fin[..., 0:D].transpose(0, 2, 1, 3).reshape(2 * NB, 64, 64)
    im = fin[..., D:2 * D].transpose(0, 2, 1, 3).reshape(2 * NB, 64, 64)
    return re, im


def kernel(x_prompt, x_sample, state_lru, state_s5_re, state_s5_im, c, c_ctx, ada_w, ada_b, norm_mix, norm_ffn, norm_final, lru_w_in, lru_conv_w, lru_conv_b, lru_w_a, lru_b_a, lru_w_i, lru_b_i, lru_lambda, lru_w_out, s5_a_re, s5_a_im, s5_log_dt, s5_b_re, s5_b_im, s5_c_re, s5_c_im, s5_d, s5_w_glu, ffn_w_up, ffn_conv_w, ffn_conv_b, ffn_w_down):
    c_all = jnp.concatenate([c.astype(F32), c_ctx.astype(F32)[None], jnp.zeros((7, D), F32)], axis=0)
    mods = _mod_vectors(c_all, ada_w.astype(F32), ada_b.astype(F32)[:, None, :])

    def mod_of(l):
        return jnp.stack([jnp.broadcast_to(mods[l, 8][None], (NB, N_MOD * D)), mods[l, 0:NB]], axis=0)

    xp4 = x_prompt.astype(F32).reshape(2, NB, P_LEN, D)
    gf = norm_final[None].astype(F32)

    mod0 = mod_of(0)
    gate, xr, x = _lru_in(xp4, x_sample.astype(F32), mod0, norm_mix[0][None].astype(F32), lru_w_in[0].astype(BF16),
                          lru_conv_w[0].astype(F32), lru_conv_b[0][None].astype(F32))
    zeros_h = jnp.zeros((NB, D), F32)
    sp = jax.nn.softplus(-lru_lambda[0].astype(F32)) * (-LRU_C * LOG2_E)
    h0_f = jnp.stack([zeros_h, zeros_h, state_lru[:, 0, 0].astype(F32)], axis=0)
    h0_b = jnp.stack([zeros_h, zeros_h, state_lru[:, 0, 1].astype(F32)], axis=0)
    hsb, fin_b = _lru_bwd(xr, _pair_blockdiag(lru_w_a[0, 1], lru_w_i[0, 1]),
                          lru_b_a[0, 1][None].astype(F32), lru_b_i[0, 1][None].astype(F32), sp[1][None], h0_b)
    x, fin_f = _lru_fwd(xr, gate, hsb, x, mod0, _pair_blockdiag(lru_w_a[0, 0], lru_w_i[0, 0]),
                        lru_b_a[0, 0][None].astype(F32), lru_b_i[0, 0][None].astype(F32), sp[0][None], h0_f,
                        lru_w_out[0].astype(BF16))
    x = _ffn(x, mod0, norm_ffn[0][None].astype(F32), ffn_w_up[0].astype(BF16), ffn_conv_w[0].astype(F32),
             ffn_conv_b[0][None].astype(F32), ffn_w_down[0].astype(BF16), gf, False)
    new_lru = jnp.stack([fin_f[0:2].reshape(2 * NB, D), fin_b[0:2].reshape(2 * NB, D)], axis=1)[:, None]

    mod1 = mod_of(1)
    g1 = norm_mix[1][None].astype(F32)
    zeros_s = jnp.zeros((S5_KT, NB, SLAB), F32)
    dirs = []
    for d in range(2):
        bb, cc, ab = _s5_dir_params(s5_a_re[0, d], s5_a_im[0, d], s5_log_dt[0, d], s5_b_re[0, d], s5_b_im[0, d],
                                    s5_c_re[0, d], s5_c_im[0, d])
        h0 = jnp.stack([zeros_s, zeros_s, _s5_state_to_slab(state_s5_re[:, 0, d], state_s5_im[:, 0, d])], axis=0)
        dirs.append((bb, cc, ab, h0))
    yb, fin_sb = _s5_bwd(x, mod1, g1, *dirs[1])
    x, fin_sf = _s5_fwd(x, yb, mod1, g1, *dirs[0], s5_d[0][None].astype(F32), s5_w_glu[0].astype(BF16))
    y_p4, y_sample = _ffn(x, mod1, norm_ffn[1][None].astype(F32), ffn_w_up[1].astype(BF16), ffn_conv_w[1].astype(F32),
             ffn_conv_b[1][None].astype(F32), ffn_w_down[1].astype(BF16), gf, True)
    f_re, f_im = _s5_slab_to_state(fin_sf[0:2])
    b_re, b_im = _s5_slab_to_state(fin_sb[0:2])
    new_s5_re = jnp.stack([f_re, b_re], axis=1)[:, None]
    new_s5_im = jnp.stack([f_im, b_im], axis=1)[:, None]

    return (y_p4.reshape(2 * NB, P_LEN, D), y_sample, new_lru, new_s5_re, new_s5_im)
```

```python
import functools

import jax
import jax.numpy as jnp
from jax import lax
from jax.experimental import pallas as pl
from jax.experimental.pallas import tpu as pltpu

F32 = jnp.float32
BF16 = jnp.bfloat16

D = 1024
D_FF = 2816
N_MOD = 6
EPS = 1e-6
LRU_C = 8.0
LOG2_E = 1.4426950408889634
NB = 8
STEPS = 64
TM = STEPS * NB
HALO = 16
NORM_ROWS = 64
FF_CHUNK = 256
FFN_SUB = 2
COL_CHUNK = 256
P_LEN = 256
S_LEN = 4096
P_TILES_PER_SEQ = P_LEN // STEPS
N_P_TILES = 2 * P_TILES_PER_SEQ
N_S_TILES = S_LEN // STEPS
NT = N_P_TILES + N_S_TILES
N_ROWS = NT * TM
N_SEQ = 3
S5_KT = 4
SLAB = 2048
S5_BN = 256
VMEM_LIMIT = 56 * 1024 * 1024


def _seq_id(j):
    return jnp.where(j >= N_P_TILES, 2, jnp.where(j >= P_TILES_PER_SEQ, 1, 0))


def _is_seq_first(j):
    return (j == 0) | (j == P_TILES_PER_SEQ) | (j == N_P_TILES)


def _is_seq_last(j):
    return (j == P_TILES_PER_SEQ - 1) | (j == N_P_TILES - 1) | (j == NT - 1)


def _params(sem):
    return pltpu.CompilerParams(dimension_semantics=sem, vmem_limit_bytes=VMEM_LIMIT)


def _const_spec(shape):
    n = len(shape)
    return pl.BlockSpec(shape, lambda *_: (0,) * n, pipeline_mode=pl.Buffered(1))


def _mod_spec(tile_of):
    return pl.BlockSpec((None, NB, N_MOD * D),
                        lambda i: (jnp.where(tile_of(i) >= N_P_TILES, 1, 0), 0, 0))


def _rms(xv, g):
    ms = jnp.mean(xv * xv, axis=-1, keepdims=True)
    return xv * lax.rsqrt(ms + EPS) * g


def _per_batch(y, vec):
    r, n = y.shape
    return (y.reshape(r // NB, NB, n) * vec[None]).reshape(r, n)


def _norm_mod(xv, g, scale1, shift):
    r = xv.shape[0]
    y = _rms(xv, g).reshape(r // NB, NB, D) * scale1[None] + shift[None]
    return y.reshape(r, D)


def _fill_norm(dst, dst_off, x_ref, g, scale1, shift, f32_dst=None):
    for k in range(TM // NORM_ROWS):
        r0 = k * NORM_ROWS
        y = _norm_mod(x_ref[r0:r0 + NORM_ROWS], g, scale1, shift)
        dst[dst_off + r0:dst_off + r0 + NORM_ROWS] = y.astype(BF16)
        if f32_dst is not None:
            f32_dst[r0:r0 + NORM_ROWS] = y


def _fill_hbuf(hbuf, x_ref, x_prev, x_next, g, scale1, shift):
    hbuf[0:HALO] = _norm_mod(x_prev, g, scale1, shift).astype(BF16)
    hbuf[HALO + TM:HALO + TM + HALO] = _norm_mod(x_next, g, scale1, shift).astype(BF16)
    _fill_norm(hbuf, HALO, x_ref, g, scale1, shift)


def _mod_kernel(c_ref, w_ref, b_ref, o_ref):
    cv = c_ref[...]
    s = (cv * jax.nn.sigmoid(cv)).astype(BF16)
    o_ref[...] = jnp.dot(s, w_ref[...].astype(BF16), preferred_element_type=F32) + b_ref[...]


def _mod_vectors(c_all, ada_w, ada_b):
    depth = ada_w.shape[0]
    return pl.pallas_call(
        _mod_kernel,
        out_shape=jax.ShapeDtypeStruct((depth, 16, N_MOD * D), F32),
        grid=(depth, N_MOD),
        in_specs=[
            pl.BlockSpec((16, D), lambda l, n: (0, 0)),
            pl.BlockSpec((None, D, D), lambda l, n: (l, 0, n)),
            pl.BlockSpec((None, 1, D), lambda l, n: (l, 0, n)),
        ],
        out_specs=pl.BlockSpec((None, 16, D), lambda l, n: (l, 0, n)),
        compiler_params=_params(("arbitrary", "arbitrary")),
        name="mod_vectors",
    )(c_all, ada_w, ada_b)


def _halo_specs():
    prev = pl.BlockSpec((HALO, D), lambda i: (jnp.maximum(i * (TM // HALO) - 1, 0), 0))
    nxt = pl.BlockSpec((HALO, D), lambda i: (jnp.minimum((i + 1) * (TM // HALO), N_ROWS // HALO - 1), 0))
    return prev, nxt


def _bt_specs(steps, offset, tile_steps=STEPS):
    per_tile = tile_steps // steps
    tiles_per_seq = P_LEN // tile_steps
    n_p = 2 * tiles_per_seq

    def prompt_idx(i):
        ip = jnp.minimum(i, n_p - 1)
        blk = jnp.clip((ip % tiles_per_seq) * per_tile + offset, 0, P_LEN // steps - 1)
        return (ip // tiles_per_seq, 0, blk, 0)

    def sample_idx(i):
        blk = jnp.clip((i - n_p) * per_tile + offset, 0, S_LEN // steps - 1)
        return (0, blk, 0)

    return (pl.BlockSpec((None, NB, steps, D), prompt_idx), pl.BlockSpec((NB, steps, D), sample_idx))


def _to_rows(x_bt):
    n = x_bt.shape[1]
    return jnp.swapaxes(x_bt, 0, 1).reshape(n * NB, D)


def _lru_in_kernel(xa_ref, xap_ref, xan_ref, xb_ref, xbp_ref, xbn_ref, mod_ref, g_ref, wi_ref, cw_ref, cb_ref,
                   gate_ref, xr_ref, xc_ref, hbuf):
    i = pl.program_id(0)
    shift = mod_ref[:, 0:D]
    scale1 = 1.0 + mod_ref[:, D:2 * D]

    is_p = i < N_P_TILES
    xc_ref[...] = _to_rows(jnp.where(is_p, xa_ref[...], xb_ref[...]))
    x_prev = _to_rows(jnp.where(is_p, xap_ref[...], xbp_ref[...]))[NB * NB - HALO:NB * NB]
    x_next = _to_rows(jnp.where(is_p, xan_ref[...], xbn_ref[...]))[0:HALO]
    _fill_hbuf(hbuf, xc_ref, x_prev, x_next, g_ref[...], scale1, shift)

    first = _is_seq_first(i)
    last = _is_seq_last(i)

    for c in range(D // COL_CHUNK):
        cs = slice(COL_CHUNK * c, COL_CHUNK * (c + 1))
        cx = slice(D + COL_CHUNK * c, D + COL_CHUNK * (c + 1))
        gate_ref[:, cs] = jax.nn.gelu(
            jnp.dot(hbuf[HALO:HALO + TM], wi_ref[:, cs], preferred_element_type=F32)).astype(BF16)
        u = jnp.dot(hbuf[...], wi_ref[:, cx], preferred_element_type=F32)
        head = jnp.where(first, 0.0, u[0:16])
        tail = jnp.where(last, 0.0, u[TM + 16:TM + 24])
        m2 = jnp.concatenate([head, u[16:TM]], axis=0)
        m1 = jnp.concatenate([head[8:16], u[16:TM + 8]], axis=0)
        p1 = jnp.concatenate([u[24:TM + 16], tail], axis=0)
        xr_ref[:, cs] = (cb_ref[:, cs] + m2 * cw_ref[0:1, cs] + m1 * cw_ref[1:2, cs]
                         + u[16:TM + 16] * cw_ref[2:3, cs] + p1 * cw_ref[3:4, cs])


def _lru_in(xp4, xs, mod, g, wi, cw, cb):
    p_tile, s_tile = _bt_specs(STEPS, 0)
    p_prev, s_prev = _bt_specs(NB, -1)
    p_next, s_next = _bt_specs(NB, STEPS // NB)
    tile = pl.BlockSpec((TM, D), lambda i: (i, 0))
    rows = jax.ShapeDtypeStruct((N_ROWS, D), F32)
    return pl.pallas_call(
        _lru_in_kernel,
        out_shape=(jax.ShapeDtypeStruct((N_ROWS, D), BF16), rows, rows),
        grid=(NT,),
        in_specs=[p_tile, p_prev, p_next, s_tile, s_prev, s_next,
                  _mod_spec(lambda i: i), _const_spec((1, D)),
                  _const_spec((D, 2 * D)), _const_spec((4, D)), _const_spec((1, D))],
        out_specs=(tile, tile, tile),
        scratch_shapes=[pltpu.VMEM((TM + 2 * HALO, D), BF16)],
        compiler_params=_params(("arbitrary",)),
        name="lru_in",
    )(xp4, xp4, xp4, xs, xs, xs, mod, g, wi, cw, cb)


def _lru_gates(xr_ref, wp_ref, ba_ref, bi_ref, sp_ref, a_s, bx_s, xb_s):
    xb_s[...] = xr_ref[...].astype(BF16)
    for p in range(8):
        cols = slice(128 * p, 128 * (p + 1))
        z = jnp.dot(xb_s[:, cols], wp_ref[p], preferred_element_type=F32)
        r = jax.nn.sigmoid(z[:, 0:128] + ba_ref[:, cols])
        ig = jax.nn.sigmoid(z[:, 128:256] + bi_ref[:, cols])
        a = jnp.exp2(r * sp_ref[:, cols])
        a_s[:, cols] = a
        s = 1.0 - a * a
        root = jnp.where(s > 0.0, s * lax.rsqrt(s), 0.0)
        bx_s[:, cols] = root * (ig * xr_ref[:, cols])


def _lru_scan(a_s, bx_s, h_s, out_ref, reverse):
    h = h_s[...]
    for s in range(STEPS):
        t = (STEPS - 1 - s) if reverse else s
        rows = slice(8 * t, 8 * (t + 1))
        h = a_s[rows] * h + bx_s[rows]
        out_ref[rows] = h
    h_s[...] = h


def _lru_bwd_kernel(xr_ref, wp_ref, ba_ref, bi_ref, sp_ref, h0_ref,
                    hs_ref, fin_ref, a_s, bx_s, xb_s, h_s):
    j = NT - 1 - pl.program_id(0)

    @pl.when(_is_seq_last(j))
    def _():
        h_s[...] = h0_ref[...]

    _lru_gates(xr_ref, wp_ref, ba_ref, bi_ref, sp_ref, a_s, bx_s, xb_s)
    _lru_scan(a_s, bx_s, h_s, bx_s, True)
    hs_ref[...] = bx_s[...].astype(BF16)
    fin_ref[...] = h_s[...]


def _lru_scratch():
    return [pltpu.VMEM((TM, D), F32), pltpu.VMEM((TM, D), F32), pltpu.VMEM((TM, D), BF16),
            pltpu.VMEM((NB, D), F32)]


def _lru_bwd(xr, wp, ba, bi, sp, h0):
    tile = pl.BlockSpec((TM, D), lambda i: (NT - 1 - i, 0))
    seq = pl.BlockSpec((None, NB, D), lambda i: (_seq_id(NT - 1 - i), 0, 0))
    return pl.pallas_call(
        _lru_bwd_kernel,
        out_shape=(jax.ShapeDtypeStruct((N_ROWS, D), BF16), jax.ShapeDtypeStruct((N_SEQ, NB, D), F32)),
        grid=(NT,),
        in_specs=[tile, _const_spec((8, 128, 256)), _const_spec((1, D)), _const_spec((1, D)),
                  _const_spec((1, D)), seq],
        out_specs=(tile, seq),
        scratch_shapes=_lru_scratch(),
        compiler_params=_params(("arbitrary",)),
        name="lru_bwd",
    )(xr, wp, ba, bi, sp, h0)


def _lru_fwd_kernel(xr_ref, gate_ref, hsb_ref, x_ref, mod_ref, wp_ref, ba_ref, bi_ref, sp_ref, h0_ref,
                    wo_ref, x1_ref, fin_ref, a_s, bx_s, xb_s, h_s):
    j = pl.program_id(0)

    @pl.when(_is_seq_first(j))
    def _():
        h_s[...] = h0_ref[...]

    _lru_gates(xr_ref, wp_ref, ba_ref, bi_ref, sp_ref, a_s, bx_s, xb_s)
    _lru_scan(a_s, bx_s, h_s, bx_s, False)
    fin_ref[...] = h_s[...]

    xb_s[...] = ((bx_s[...] + hsb_ref[...].astype(F32)) * gate_ref[...].astype(F32)).astype(BF16)
    for c in range(D // COL_CHUNK):
        cs = slice(COL_CHUNK * c, COL_CHUNK * (c + 1))
        out = jnp.dot(xb_s[...], wo_ref[:, cs], preferred_element_type=F32)
        x1_ref[:, cs] = x_ref[:, cs] + _per_batch(out, mod_ref[:, 2 * D + COL_CHUNK * c:2 * D + COL_CHUNK * (c + 1)])


def _lru_fwd(xr, gate, hsb, x, mod, wp, ba, bi, sp, h0, wo):
    tile = pl.BlockSpec((TM, D), lambda i: (i, 0))
    seq = pl.BlockSpec((None, NB, D), lambda i: (_seq_id(i), 0, 0))
    return pl.pallas_call(
        _lru_fwd_kernel,
        out_shape=(jax.ShapeDtypeStruct((N_ROWS, D), F32), jax.ShapeDtypeStruct((N_SEQ, NB, D), F32)),
        grid=(NT,),
        in_specs=[tile, tile, tile, tile, _mod_spec(lambda i: i), _const_spec((8, 128, 256)),
                  _const_spec((1, D)), _const_spec((1, D)), _const_spec((1, D)), seq, _const_spec((D, D))],
        out_specs=(tile, seq),
        scratch_shapes=_lru_scratch(),
        compiler_params=_params(("arbitrary",)),
        name="lru_fwd",
    )(xr, gate, hsb, x, mod, wp, ba, bi, sp, h0, wo)


def _ffn_kernel(x_ref, xp_ref, xn_ref, mod_ref, g_ref, wu_ref, cw_ref, cb_ref, wd_ref, gf_ref,
                *rest, final_norm):
    if final_norm:
        op_ref, os_ref, hbuf, act = rest
    else:
        o_ref, hbuf, act = rest
    k = pl.program_id(0)
    g = g_ref[...]
    shift = mod_ref[:, 3 * D:4 * D]
    scale1 = 1.0 + mod_ref[:, 4 * D:5 * D]
    gt2 = mod_ref[:, 5 * D:6 * D]
    rows_all = FFN_SUB * TM

    def fill_piece(p):
        n_blocks = rows_all // NORM_ROWS
        if p == 0:
            hbuf[0:HALO] = _norm_mod(xp_ref[...], g, scale1, shift).astype(BF16)
        elif p <= n_blocks:
            r0 = (p - 1) * NORM_ROWS
            hbuf[HALO + r0:HALO + r0 + NORM_ROWS] = _norm_mod(
                x_ref[r0:r0 + NORM_ROWS], g, scale1, shift).astype(BF16)
        elif p == n_blocks + 1:
            hbuf[HALO + rows_all:HALO + rows_all + HALO] = _norm_mod(xn_ref[...], g, scale1, shift).astype(BF16)

    n_pieces = rows_all // NORM_ROWS + 2
    first_pieces = TM // NORM_ROWS + 2
    for p in range(first_pieces):
        fill_piece(p)
    next_piece = first_pieces
    finals = []

    for s in range(FFN_SUB):
        tile = FFN_SUB * k + s
        is_prompt = tile < N_P_TILES
        seg_first = jnp.logical_not(is_prompt & ((tile % P_TILES_PER_SEQ) != 0))
        seg_last = jnp.logical_not(is_prompt & ((tile % P_TILES_PER_SEQ) != P_TILES_PER_SEQ - 1))
        h_rows = slice(TM * s, TM * s + TM + 2 * HALO)

        def conv3(cs):
            u = jnp.dot(hbuf[h_rows], wu_ref[:, cs], preferred_element_type=F32)
            p0 = jnp.where(seg_first, 0.0, u[8:16])
            n0 = jnp.where(seg_last, 0.0, u[TM + 16:TM + 24])
            prev = jnp.concatenate([p0, u[16:TM + 8]], axis=0)
            nxt = jnp.concatenate([u[24:TM + 16], n0], axis=0)
            return (cb_ref[:, cs] + prev * cw_ref[0:1, cs] + u[16:TM + 16] * cw_ref[1:2, cs]
                    + nxt * cw_ref[2:3, cs])

        for c in range(D_FF // FF_CHUNK):
            v = conv3(slice(FF_CHUNK * c, FF_CHUNK * (c + 1)))
            gg = conv3(slice(D_FF + FF_CHUNK * c, D_FF + FF_CHUNK * (c + 1)))
            act[s, :, FF_CHUNK * c:FF_CHUNK * (c + 1)] = (v * (gg * jax.nn.sigmoid(gg))).astype(BF16)
            if next_piece < n_pieces:
                fill_piece(next_piece)
                next_piece += 1

        out = jnp.dot(act[s], wd_ref[...], preferred_element_type=F32)
        y = x_ref[TM * s:TM * (s + 1)] + _per_batch(out, gt2)
        if final_norm:
            finals.append(jnp.swapaxes(_rms(y, gf_ref[...]).reshape(STEPS, NB, D), 0, 1))
        else:
            o_ref[TM * s:TM * (s + 1)] = y

    if final_norm:
        y_bt = jnp.concatenate(finals, axis=1)

        @pl.when(k < N_P_TILES // FFN_SUB)
        def _():
            op_ref[...] = y_bt

        @pl.when(k >= N_P_TILES // FFN_SUB)
        def _():
            os_ref[...] = y_bt


def _ffn(x, mod, layer, g, wu, cw, cb, wd, gf, final_norm):
    rows = FFN_SUB * TM
    n_steps = NT // FFN_SUB
    blk = pl.BlockSpec((rows, D), lambda i: (i, 0))
    prev = pl.BlockSpec((HALO, D), lambda i: (jnp.maximum(i * (rows // HALO) - 1, 0), 0))
    nxt = pl.BlockSpec((HALO, D), lambda i: (jnp.minimum((i + 1) * (rows // HALO), N_ROWS // HALO - 1), 0))
    mod_spec = pl.BlockSpec((None, NB, N_MOD * D), lambda i: (jnp.where(i >= N_P_TILES // FFN_SUB, 1, 0), 0, 0))

    def layer_spec(shape):
        n = len(shape)
        return pl.BlockSpec((None,) + shape, lambda *_: (layer,) + (0,) * n, pipeline_mode=pl.Buffered(1))

    if final_norm:
        out_shape = (jax.ShapeDtypeStruct((2, NB, P_LEN, D), F32), jax.ShapeDtypeStruct((NB, S_LEN, D), F32))
        out_specs = _bt_specs(FFN_SUB * STEPS, 0, FFN_SUB * STEPS)
    else:
        out_shape = jax.ShapeDtypeStruct((N_ROWS, D), F32)
        out_specs = blk
    return pl.pallas_call(
        functools.partial(_ffn_kernel, final_norm=final_norm),
        out_shape=out_shape,
        grid=(n_steps,),
        in_specs=[blk, prev, nxt, mod_spec, layer_spec((1, D)),
                  layer_spec((D, 2 * D_FF)), layer_spec((3, 2 * D_FF)), layer_spec((1, 2 * D_FF)),
                  layer_spec((D_FF, D)), _const_spec((1, D))],
        out_specs=out_specs,
        scratch_shapes=[pltpu.VMEM((rows + 2 * HALO, D), BF16), pltpu.VMEM((FFN_SUB, TM, D_FF), BF16)],
        compiler_params=_params(("arbitrary",)),
        name="conv_ffn",
    )(x, x, x, mod, g, wu, cw, cb, wd, gf)


def _s5_core(ubuf, bb_ref, cc_ref, ab_ref, slabs, hs16, hst, y_write, reverse):
    def b_piece(kt, n):
        cs = slice(S5_BN * n, S5_BN * (n + 1))
        slabs[kt % 2, :, cs] = jnp.dot(ubuf[:, 256 * kt:256 * (kt + 1)], bb_ref[kt, :, cs],
                                       preferred_element_type=F32)

    def c_half(kt, half):
        cs = slice(D * half, D * (half + 1))
        return jnp.dot(hs16[kt % 2, :, cs], cc_ref[kt, cs, :], preferred_element_type=F32)

    n_b = SLAB // S5_BN
    every = STEPS // n_b
    quarter = STEPS // 4
    for n in range(n_b):
        b_piece(0, n)
    for kt in range(S5_KT):
        slab = slabs.at[kt % 2]
        hs = hs16.at[kt % 2]
        a_re = jnp.broadcast_to(ab_ref[kt, 0:1, :], (NB, D))
        a_im = jnp.broadcast_to(ab_ref[kt, 1:2, :], (NB, D))
        h_re = hst[kt, :, 0:D]
        h_im = hst[kt, :, D:2 * D]
        part = None
        for s in range(STEPS):
            if s % every == 0 and kt + 1 < S5_KT:
                b_piece(kt + 1, s // every)
            if s == quarter and kt >= 1:
                part = c_half(kt - 1, 0)
            if s == 3 * quarter and kt >= 1:
                y_write(kt - 1, part + c_half(kt - 1, 1))
            t = (STEPS - 1 - s) if reverse else s
            rows = slice(8 * t, 8 * (t + 1))
            n_re = a_re * h_re - a_im * h_im + slab[rows, 0:D]
            n_im = a_re * h_im + a_im * h_re + slab[rows, D:2 * D]
            if s % 2 == 1:
                lo = min(t, t + 1 if reverse else t - 1)
                pair = slice(8 * lo, 8 * lo + 16)
                first, second = ((n_re, h_re), (n_im, h_im)) if reverse else ((h_re, n_re), (h_im, n_im))
                hs[pair, 0:D] = jnp.concatenate(first, axis=0).astype(BF16)
                hs[pair, D:2 * D] = jnp.concatenate(second, axis=0).astype(BF16)
            h_re, h_im = n_re, n_im
        hst[kt, :, 0:D] = h_re
        hst[kt, :, D:2 * D] = h_im
    y_write(S5_KT - 1, c_half(S5_KT - 1, 0) + c_half(S5_KT - 1, 1))


def _s5_bwd_kernel(x_ref, mod_ref, g_ref, bb_ref, cc_ref, ab_ref, h0_ref,
                   yb_ref, fin_ref, ubuf, slabs, hs16, hst):
    j = NT - 1 - pl.program_id(0)

    @pl.when(_is_seq_last(j))
    def _():
        hst[...] = h0_ref[...]

    _fill_norm(ubuf, 0, x_ref, g_ref[...], 1.0 + mod_ref[:, D:2 * D], mod_ref[:, 0:D])

    def y_write(kt, val):
        yb_ref[:, 256 * kt:256 * (kt + 1)] = val

    _s5_core(ubuf, bb_ref, cc_ref, ab_ref, slabs, hs16, hst, y_write, True)
    fin_ref[...] = hst[...]


def _s5_bwd(x, mod, g, bb, cc, ab, h0):
    tile = pl.BlockSpec((TM, D), lambda i: (NT - 1 - i, 0))
    seq = pl.BlockSpec((None, S5_KT, NB, SLAB), lambda i: (_seq_id(NT - 1 - i), 0, 0, 0))
    return pl.pallas_call(
        _s5_bwd_kernel,
        out_shape=(jax.ShapeDtypeStruct((N_ROWS, D), F32),
                   jax.ShapeDtypeStruct((N_SEQ, S5_KT, NB, SLAB), F32)),
        grid=(NT,),
        in_specs=[tile, _mod_spec(lambda i: NT - 1 - i), _const_spec((1, D)),
                  _const_spec((S5_KT, 256, SLAB)), _const_spec((S5_KT, SLAB, 256)),
                  _const_spec((S5_KT, 2, D)), seq],
        out_specs=(tile, seq),
        scratch_shapes=[pltpu.VMEM((TM, D), BF16), pltpu.VMEM((2, TM, SLAB), F32), pltpu.VMEM((2, TM, SLAB), BF16),
                        pltpu.VMEM((S5_KT, NB, SLAB), F32)],
        compiler_params=_params(("arbitrary",)),
        name="s5_bwd",
    )(x, mod, g, bb, cc, ab, h0)


def _s5_fwd_kernel(x_ref, yb_ref, mod_ref, g_ref, bb_ref, cc_ref, ab_ref, h0_ref, dsk_ref, wglu_ref,
                   x1_ref, fin_ref, ubuf, zbuf, u_s, slabs, hs16, hst):
    j = pl.program_id(0)

    @pl.when(_is_seq_first(j))
    def _():
        hst[...] = h0_ref[...]

    _fill_norm(ubuf, 0, x_ref, g_ref[...], 1.0 + mod_ref[:, D:2 * D], mod_ref[:, 0:D], u_s)

    def y_write(kt, val):
        cs = slice(256 * kt, 256 * (kt + 1))
        y = dsk_ref[:, cs] * u_s[:, cs] + yb_ref[:, cs] + val
        zbuf[:, cs] = jax.nn.gelu(y).astype(BF16)

    _s5_core(ubuf, bb_ref, cc_ref, ab_ref, slabs, hs16, hst, y_write, False)
    fin_ref[...] = hst[...]

    for c in range(D // COL_CHUNK):
        cs = slice(COL_CHUNK * c, COL_CHUNK * (c + 1))
        v = jnp.dot(zbuf[...], wglu_ref[:, cs], preferred_element_type=F32)
        gg = jnp.dot(zbuf[...], wglu_ref[:, D + COL_CHUNK * c:D + COL_CHUNK * (c + 1)], preferred_element_type=F32)
        out = v * jax.nn.sigmoid(gg)
        x1_ref[:, cs] = x_ref[:, cs] + _per_batch(out, mod_ref[:, 2 * D + COL_CHUNK * c:2 * D + COL_CHUNK * (c + 1)])


def _s5_fwd(x, yb, mod, g, bb, cc, ab, h0, dsk, wglu):
    tile = pl.BlockSpec((TM, D), lambda i: (i, 0))
    seq = pl.BlockSpec((None, S5_KT, NB, SLAB), lambda i: (_seq_id(i), 0, 0, 0))
    return pl.pallas_call(
        _s5_fwd_kernel,
        out_shape=(jax.ShapeDtypeStruct((N_ROWS, D), F32),
                   jax.ShapeDtypeStruct((N_SEQ, S5_KT, NB, SLAB), F32)),
        grid=(NT,),
        in_specs=[tile, tile, _mod_spec(lambda i: i), _const_spec((1, D)),
                  _const_spec((S5_KT, 256, SLAB)), _const_spec((S5_KT, SLAB, 256)),
                  _const_spec((S5_KT, 2, D)), seq, _const_spec((1, D)), _const_spec((D, 2 * D))],
        out_specs=(tile, seq),
        scratch_shapes=[pltpu.VMEM((TM, D), BF16), pltpu.VMEM((TM, D), BF16), pltpu.VMEM((TM, D), F32),
                        pltpu.VMEM((2, TM, SLAB), F32), pltpu.VMEM((2, TM, SLAB), BF16),
                        pltpu.VMEM((S5_KT, NB, SLAB), F32)],
        compiler_params=_params(("arbitrary",)),
        name="s5_fwd",
    )(x, yb, mod, g, bb, cc, ab, h0, dsk, wglu)


def _pair_blockdiag(w_a, w_i):
    same = (jnp.arange(2)[:, None, None, None] == jnp.arange(2)[None, None, :, None])

    def bd(w):
        w4 = w.astype(F32).reshape(8, 2, 64, 1, 64)
        return jnp.where(same[None], w4, 0.0).reshape(8, 128, 128)

    return jnp.concatenate([bd(w_a), bd(w_i)], axis=-1).astype(BF16)


def _s5_dir_params(a_re, a_im, log_dt, b_re, b_im, c_re, c_im):
    l_re = a_re.astype(F32)
    l_im = a_im.astype(F32)
    dt = jnp.exp(log_dt.astype(F32))[:, None]
    mag = jnp.exp(l_re * dt)
    ab_re = mag * jnp.cos(l_im * dt)
    ab_im = mag * jnp.sin(l_im * dt)
    den = l_re * l_re + l_im * l_im
    k_re = ((ab_re - 1.0) * l_re + ab_im * l_im) / den
    k_im = (ab_im * l_re - (ab_re - 1.0) * l_im) / den
    br = b_re.astype(F32)
    bi = b_im.astype(F32)
    bb_re = br * k_re[..., None] - bi * k_im[..., None]
    bb_im = br * k_im[..., None] + bi * k_re[..., None]
    same = (jnp.arange(16)[:, None, None, None] == jnp.arange(16)[None, None, :, None])[None]

    def b_blk(m):
        m5 = m.reshape(S5_KT, 16, 64, 16).transpose(0, 1, 3, 2)[:, :, :, None, :]
        return jnp.where(same, m5, 0.0).reshape(S5_KT, 256, 1024)

    def c_blk(m):
        m5 = m.reshape(S5_KT, 16, 16, 64).transpose(0, 1, 3, 2)[:, :, :, None, :]
        return jnp.where(same, m5, 0.0).reshape(S5_KT, 1024, 256)

    bb = jnp.concatenate([b_blk(bb_re), b_blk(bb_im)], axis=-1).astype(BF16)
    cc = jnp.concatenate([c_blk(c_re.astype(F32)), c_blk(-c_im.astype(F32))], axis=1).astype(BF16)
    ab = jnp.stack([ab_re.reshape(S5_KT, D), ab_im.reshape(S5_KT, D)], axis=1)
    return bb, cc, ab


def _s5_state_to_slab(s_re, s_im):
    re = s_re.astype(F32).reshape(NB, S5_KT, D).transpose(1, 0, 2)
    im = s_im.astype(F32).reshape(NB, S5_KT, D).transpose(1, 0, 2)
    return jnp.concatenate([re, im], axis=-1)


def _s5_slab_to_state(fin):
    re = fin[..., 0:D].transpose(0, 2, 1, 3).reshape(2 * NB, 64, 64)
    im = fin[..., D:2 * D].transpose(0, 2, 1, 3).reshape(2 * NB, 64, 64)
    return re, im


def kernel(x_prompt, x_sample, state_lru, state_s5_re, state_s5_im, c, c_ctx, ada_w, ada_b, norm_mix, norm_ffn, norm_final, lru_w_in, lru_conv_w, lru_conv_b, lru_w_a, lru_b_a, lru_w_i, lru_b_i, lru_lambda, lru_w_out, s5_a_re, s5_a_im, s5_log_dt, s5_b_re, s5_b_im, s5_c_re, s5_c_im, s5_d, s5_w_glu, ffn_w_up, ffn_conv_w, ffn_conv_b, ffn_w_down):
    c_all = jnp.concatenate([c.astype(F32), c_ctx.astype(F32)[None], jnp.zeros((7, D), F32)], axis=0)
    mods = _mod_vectors(c_all, ada_w.astype(F32), ada_b.astype(F32)[:, None, :])

    def mod_of(l):
        return jnp.stack([jnp.broadcast_to(mods[l, 8][None], (NB, N_MOD * D)), mods[l, 0:NB]], axis=0)

    xp4 = x_prompt.astype(F32).reshape(2, NB, P_LEN, D)
    gf = norm_final[None].astype(F32)

    mod0 = mod_of(0)
    gate, xr, x = _lru_in(xp4, x_sample.astype(F32), mod0, norm_mix[0][None].astype(F32), lru_w_in[0].astype(BF16),
                          lru_conv_w[0].astype(F32), lru_conv_b[0][None].astype(F32))
    zeros_h = jnp.zeros((NB, D), F32)
    sp = jax.nn.softplus(-lru_lambda[0].astype(F32)) * (-LRU_C * LOG2_E)
    h0_f = jnp.stack([zeros_h, zeros_h, state_lru[:, 0, 0].astype(F32)], axis=0)
    h0_b = jnp.stack([zeros_h, zeros_h, state_lru[:, 0, 1].astype(F32)], axis=0)
    hsb, fin_b = _lru_bwd(xr, _pair_blockdiag(lru_w_a[0, 1], lru_w_i[0, 1]),
                          lru_b_a[0, 1][None].astype(F32), lru_b_i[0, 1][None].astype(F32), sp[1][None], h0_b)
    x, fin_f = _lru_fwd(xr, gate, hsb, x, mod0, _pair_blockdiag(lru_w_a[0, 0], lru_w_i[0, 0]),
                        lru_b_a[0, 0][None].astype(F32), lru_b_i[0, 0][None].astype(F32), sp[0][None], h0_f,
                        lru_w_out[0].astype(BF16))
    ffn_args = (norm_ffn.astype(F32)[:, None, :], ffn_w_up.astype(BF16), ffn_conv_w.astype(F32),
                ffn_conv_b.astype(F32)[:, None, :], ffn_w_down.astype(BF16), gf)
    x = _ffn(x, mod0, 0, *ffn_args, False)
    new_lru = jnp.stack([fin_f[0:2].reshape(2 * NB, D), fin_b[0:2].reshape(2 * NB, D)], axis=1)[:, None]

    mod1 = mod_of(1)
    g1 = norm_mix[1][None].astype(F32)
    zeros_s = jnp.zeros((S5_KT, NB, SLAB), F32)
    dirs = []
    for d in range(2):
        bb, cc, ab = _s5_dir_params(s5_a_re[0, d], s5_a_im[0, d], s5_log_dt[0, d], s5_b_re[0, d], s5_b_im[0, d],
                                    s5_c_re[0, d], s5_c_im[0, d])
        h0 = jnp.stack([zeros_s, zeros_s, _s5_state_to_slab(state_s5_re[:, 0, d], state_s5_im[:, 0, d])], axis=0)
        dirs.append((bb, cc, ab, h0))
    yb, fin_sb = _s5_bwd(x, mod1, g1, *dirs[1])
    x, fin_sf = _s5_fwd(x, yb, mod1, g1, *dirs[0], s5_d[0][None].astype(F32), s5_w_glu[0].astype(BF16))
    y_p4, y_sample = _ffn(x, mod1, 1, *ffn_args, True)
    f_re, f_im = _s5_slab_to_state(fin_sf[0:2])
    b_re, b_im = _s5_slab_to_state(fin_sb[0:2])
    new_s5_re = jnp.stack([f_re, b_re], axis=1)[:, None]
    new_s5_im = jnp.stack([f_im, b_im], axis=1)[:, None]

    return (y_p4.reshape(2 * NB, P_LEN, D), y_sample, new_lru, new_s5_re, new_s5_im)
```

```python
import functools

import jax
import jax.numpy as jnp
from jax import lax
from jax.experimental import pallas as pl
from jax.experimental.pallas import tpu as pltpu

F32 = jnp.float32
BF16 = jnp.bfloat16

D = 1024
D_FF = 2816
N_MOD = 6
EPS = 1e-6
LRU_C = 8.0
LOG2_E = 1.4426950408889634
NB = 8
STEPS = 64
TM = STEPS * NB
HALO = 16
NORM_ROWS = 64
FF_CHUNK = 256
FFN_SUB = 2
COL_CHUNK = 256
P_LEN = 256
S_LEN = 4096
P_TILES_PER_SEQ = P_LEN // STEPS
N_P_TILES = 2 * P_TILES_PER_SEQ
N_S_TILES = S_LEN // STEPS
NT = N_P_TILES + N_S_TILES
N_ROWS = NT * TM
N_SEQ = 3
S5_KT = 4
SLAB = 2048
S5_BN = 256
VMEM_LIMIT = 56 * 1024 * 1024


def _seq_id(j):
    return jnp.where(j >= N_P_TILES, 2, jnp.where(j >= P_TILES_PER_SEQ, 1, 0))


def _is_seq_first(j):
    return (j == 0) | (j == P_TILES_PER_SEQ) | (j == N_P_TILES)


def _is_seq_last(j):
    return (j == P_TILES_PER_SEQ - 1) | (j == N_P_TILES - 1) | (j == NT - 1)


def _params(sem):
    return pltpu.CompilerParams(dimension_semantics=sem, vmem_limit_bytes=VMEM_LIMIT)


def _const_spec(shape):
    n = len(shape)
    return pl.BlockSpec(shape, lambda *_: (0,) * n, pipeline_mode=pl.Buffered(1))


def _mod_spec(tile_of):
    return pl.BlockSpec((None, NB, N_MOD * D),
                        lambda i: (jnp.where(tile_of(i) >= N_P_TILES, 1, 0), 0, 0))


def _rms(xv, g):
    ms = jnp.mean(xv * xv, axis=-1, keepdims=True)
    return xv * lax.rsqrt(ms + EPS) * g


def _per_batch(y, vec):
    r, n = y.shape
    return (y.reshape(r // NB, NB, n) * vec[None]).reshape(r, n)


def _norm_mod(xv, g, scale1, shift):
    r = xv.shape[0]
    y = _rms(xv, g).reshape(r // NB, NB, D) * scale1[None] + shift[None]
    return y.reshape(r, D)


def _fill_norm(dst, dst_off, x_ref, g, scale1, shift, f32_dst=None):
    for k in range(TM // NORM_ROWS):
        r0 = k * NORM_ROWS
        y = _norm_mod(x_ref[r0:r0 + NORM_ROWS], g, scale1, shift)
        dst[dst_off + r0:dst_off + r0 + NORM_ROWS] = y.astype(BF16)
        if f32_dst is not None:
            f32_dst[r0:r0 + NORM_ROWS] = y


def _fill_hbuf(hbuf, x_ref, x_prev, x_next, g, scale1, shift):
    hbuf[0:HALO] = _norm_mod(x_prev, g, scale1, shift).astype(BF16)
    hbuf[HALO + TM:HALO + TM + HALO] = _norm_mod(x_next, g, scale1, shift).astype(BF16)
    _fill_norm(hbuf, HALO, x_ref, g, scale1, shift)


def _mod_kernel(c_ref, w_ref, b_ref, o_ref):
    cv = c_ref[...]
    s = (cv * jax.nn.sigmoid(cv)).astype(BF16)
    o_ref[...] = jnp.dot(s, w_ref[...].astype(BF16), preferred_element_type=F32) + b_ref[...]


def _mod_vectors(c_all, ada_w, ada_b):
    depth = ada_w.shape[0]
    return pl.pallas_call(
        _mod_kernel,
        out_shape=jax.ShapeDtypeStruct((depth, 16, N_MOD * D), F32),
        grid=(depth, N_MOD),
        in_specs=[
            pl.BlockSpec((16, D), lambda l, n: (0, 0)),
            pl.BlockSpec((None, D, D), lambda l, n: (l, 0, n)),
            pl.BlockSpec((None, 1, D), lambda l, n: (l, 0, n)),
        ],
        out_specs=pl.BlockSpec((None, 16, D), lambda l, n: (l, 0, n)),
        compiler_params=_params(("arbitrary", "arbitrary")),
        name="mod_vectors",
    )(c_all, ada_w, ada_b)


def _bt_specs(steps, offset, tile_steps=STEPS, tile_of=lambda i: i):
    per_tile = tile_steps // steps
    tiles_per_seq = P_LEN // tile_steps
    n_p = 2 * tiles_per_seq

    def prompt_idx(i):
        ip = jnp.minimum(tile_of(i), n_p - 1)
        blk = jnp.clip((ip % tiles_per_seq) * per_tile + offset, 0, P_LEN // steps - 1)
        return (ip // tiles_per_seq, 0, blk, 0)

    def sample_idx(i):
        blk = jnp.clip((tile_of(i) - n_p) * per_tile + offset, 0, S_LEN // steps - 1)
        return (0, blk, 0)

    return (pl.BlockSpec((None, NB, steps, D), prompt_idx), pl.BlockSpec((NB, steps, D), sample_idx))


def _to_rows(x_bt):
    n = x_bt.shape[1]
    return jnp.swapaxes(x_bt, 0, 1).reshape(n * NB, D)


def _lru_gate_math(z, ba, bi, sp, xr):
    t_r = jnp.tanh(z[:, 0:128] + ba)
    t_i = jnp.tanh(z[:, 128:256] + bi)
    a = jnp.exp2(sp + sp * t_r)
    s = 1.0 - a * a
    root = jnp.where(s > 0.0, s * lax.rsqrt(s), 0.0)
    return a, root * ((0.5 + 0.5 * t_i) * xr)


def _lru_scan(a_s, bx_s, h_s, out_ref, reverse):
    h = h_s[...]
    for s in range(STEPS):
        t = (STEPS - 1 - s) if reverse else s
        rows = slice(8 * t, 8 * (t + 1))
        h = a_s[rows] * h + bx_s[rows]
        out_ref[rows] = h
    h_s[...] = h


def _lru_a_kernel(xa_ref, xap_ref, xan_ref, xb_ref, xbp_ref, xbn_ref, mod_ref, g_ref, wi_ref, cw_ref, cb_ref,
                  wp_ref, ba_ref, bi_ref, sp_ref, h0_ref,
                  gate_ref, xr_ref, xc_ref, hs_ref, fin_ref, hbuf, a_s, bx_s, h_s):
    j = NT - 1 - pl.program_id(0)

    @pl.when(_is_seq_last(j))
    def _():
        h_s[...] = h0_ref[...]

    shift = mod_ref[:, 0:D]
    scale1 = 1.0 + mod_ref[:, D:2 * D]

    is_p = j < N_P_TILES
    xc_ref[...] = _to_rows(jnp.where(is_p, xa_ref[...], xb_ref[...]))
    x_prev = _to_rows(jnp.where(is_p, xap_ref[...], xbp_ref[...]))[NB * NB - HALO:NB * NB]
    x_next = _to_rows(jnp.where(is_p, xan_ref[...], xbn_ref[...]))[0:HALO]
    _fill_hbuf(hbuf, xc_ref, x_prev, x_next, g_ref[...], scale1, shift)

    first = _is_seq_first(j)
    last = _is_seq_last(j)

    for c in range(D // COL_CHUNK):
        cs = slice(COL_CHUNK * c, COL_CHUNK * (c + 1))
        cx = slice(D + COL_CHUNK * c, D + COL_CHUNK * (c + 1))
        gate_ref[:, cs] = jax.nn.gelu(
            jnp.dot(hbuf[HALO:HALO + TM], wi_ref[:, cs], preferred_element_type=F32)).astype(BF16)
        u = jnp.dot(hbuf[...], wi_ref[:, cx], preferred_element_type=F32)
        head = jnp.where(first, 0.0, u[0:16])
        tail = jnp.where(last, 0.0, u[TM + 16:TM + 24])
        m2 = jnp.concatenate([head, u[16:TM]], axis=0)
        m1 = jnp.concatenate([head[8:16], u[16:TM + 8]], axis=0)
        p1 = jnp.concatenate([u[24:TM + 16], tail], axis=0)
        xr = (cb_ref[:, cs] + m2 * cw_ref[0:1, cs] + m1 * cw_ref[1:2, cs]
              + u[16:TM + 16] * cw_ref[2:3, cs] + p1 * cw_ref[3:4, cs])
        xr_ref[:, cs] = xr
        xr16 = xr.astype(BF16)
        for h in range(COL_CHUNK // 128):
            p = (COL_CHUNK // 128) * c + h
            cols = slice(128 * p, 128 * (p + 1))
            z = jnp.dot(xr16[:, 128 * h:128 * (h + 1)], wp_ref[p], preferred_element_type=F32)
            a, bx = _lru_gate_math(z, ba_ref[:, cols], bi_ref[:, cols], sp_ref[:, cols],
                                   xr[:, 128 * h:128 * (h + 1)])
            a_s[:, cols] = a
            bx_s[:, cols] = bx

    _lru_scan(a_s, bx_s, h_s, bx_s, True)
    hs_ref[...] = bx_s[...].astype(BF16)
    fin_ref[...] = h_s[...]


def _lru_a(xp4, xs, mod, g, wi, cw, cb, wp, ba, bi, sp, h0):
    rev = lambda i: NT - 1 - i
    p_tile, s_tile = _bt_specs(STEPS, 0, tile_of=rev)
    p_prev, s_prev = _bt_specs(NB, -1, tile_of=rev)
    p_next, s_next = _bt_specs(NB, STEPS // NB, tile_of=rev)
    tile = pl.BlockSpec((TM, D), lambda i: (rev(i), 0))
    seq = pl.BlockSpec((None, NB, D), lambda i: (_seq_id(rev(i)), 0, 0))
    rows = jax.ShapeDtypeStruct((N_ROWS, D), F32)
    rows16 = jax.ShapeDtypeStruct((N_ROWS, D), BF16)
    return pl.pallas_call(
        _lru_a_kernel,
        out_shape=(rows16, rows, rows, rows16, jax.ShapeDtypeStruct((N_SEQ, NB, D), F32)),
        grid=(NT,),
        in_specs=[p_tile, p_prev, p_next, s_tile, s_prev, s_next,
                  _mod_spec(rev), _const_spec((1, D)),
                  _const_spec((D, 2 * D)), _const_spec((4, D)), _const_spec((1, D)),
                  _const_spec((8, 128, 256)), _const_spec((1, D)), _const_spec((1, D)), _const_spec((1, D)), seq],
        out_specs=(tile, tile, tile, tile, seq),
        scratch_shapes=[pltpu.VMEM((TM + 2 * HALO, D), BF16), pltpu.VMEM((TM, D), F32), pltpu.VMEM((TM, D), F32),
                        pltpu.VMEM((NB, D), F32)],
        compiler_params=_params(("arbitrary",)),
        name="lru_a",
    )(xp4, xp4, xp4, xs, xs, xs, mod, g, wi, cw, cb, wp, ba, bi, sp, h0)


def _lru_fwd_kernel(xr_ref, gate_ref, hsb_ref, x_ref, mod_ref, wp_ref, ba_ref, bi_ref, sp_ref, h0_ref,
                    wo_ref, x1_ref, fin_ref, a_s, bx_s, xb_s, h_s):
    j = pl.program_id(0)

    @pl.when(_is_seq_first(j))
    def _():
        h_s[...] = h0_ref[...]

    xb_s[...] = xr_ref[...].astype(BF16)
    for p in range(8):
        cols = slice(128 * p, 128 * (p + 1))
        z = jnp.dot(xb_s[:, cols], wp_ref[p], preferred_element_type=F32)
        a, bx = _lru_gate_math(z, ba_ref[:, cols], bi_ref[:, cols], sp_ref[:, cols], xr_ref[:, cols])
        a_s[:, cols] = a
        bx_s[:, cols] = bx
    _lru_scan(a_s, bx_s, h_s, bx_s, False)
    fin_ref[...] = h_s[...]

    xb_s[...] = ((bx_s[...] + hsb_ref[...].astype(F32)) * gate_ref[...].astype(F32)).astype(BF16)
    for c in range(D // COL_CHUNK):
        cs = slice(COL_CHUNK * c, COL_CHUNK * (c + 1))
        out = jnp.dot(xb_s[...], wo_ref[:, cs], preferred_element_type=F32)
        x1_ref[:, cs] = x_ref[:, cs] + _per_batch(out, mod_ref[:, 2 * D + COL_CHUNK * c:2 * D + COL_CHUNK * (c + 1)])


def _lru_fwd(xr, gate, hsb, x, mod, wp, ba, bi, sp, h0, wo):
    tile = pl.BlockSpec((TM, D), lambda i: (i, 0))
    seq = pl.BlockSpec((None, NB, D), lambda i: (_seq_id(i), 0, 0))
    return pl.pallas_call(
        _lru_fwd_kernel,
        out_shape=(jax.ShapeDtypeStruct((N_ROWS, D), F32), jax.ShapeDtypeStruct((N_SEQ, NB, D), F32)),
        grid=(NT,),
        in_specs=[tile, tile, tile, tile, _mod_spec(lambda i: i), _const_spec((8, 128, 256)),
                  _const_spec((1, D)), _const_spec((1, D)), _const_spec((1, D)), seq, _const_spec((D, D))],
        out_specs=(tile, seq),
        scratch_shapes=[pltpu.VMEM((TM, D), F32), pltpu.VMEM((TM, D), F32), pltpu.VMEM((TM, D), BF16),
                        pltpu.VMEM((NB, D), F32)],
        compiler_params=_params(("arbitrary",)),
        name="lru_fwd",
    )(xr, gate, hsb, x, mod, wp, ba, bi, sp, h0, wo)


def _ffn_kernel(x_ref, xp_ref, xn_ref, mod_ref, g_ref, wu_ref, cw_ref, cb_ref, wd_ref, gf_ref,
                *rest, final_norm):
    if final_norm:
        op_ref, os_ref, hbuf, act = rest
    else:
        o_ref, hbuf, act = rest
    k = pl.program_id(0)
    g = g_ref[...]
    shift = mod_ref[:, 3 * D:4 * D]
    scale1 = 1.0 + mod_ref[:, 4 * D:5 * D]
    gt2 = mod_ref[:, 5 * D:6 * D]
    rows_all = FFN_SUB * TM

    def fill_piece(p):
        n_blocks = rows_all // NORM_ROWS
        if p == 0:
            hbuf[0:HALO] = _norm_mod(xp_ref[...], g, scale1, shift).astype(BF16)
        elif p <= n_blocks:
            r0 = (p - 1) * NORM_ROWS
            hbuf[HALO + r0:HALO + r0 + NORM_ROWS] = _norm_mod(
                x_ref[r0:r0 + NORM_ROWS], g, scale1, shift).astype(BF16)
        elif p == n_blocks + 1:
            hbuf[HALO + rows_all:HALO + rows_all + HALO] = _norm_mod(xn_ref[...], g, scale1, shift).astype(BF16)

    n_pieces = rows_all // NORM_ROWS + 2
    first_pieces = TM // NORM_ROWS + 2
    for p in range(first_pieces):
        fill_piece(p)
    next_piece = first_pieces
    finals = []

    for s in range(FFN_SUB):
        tile = FFN_SUB * k + s
        is_prompt = tile < N_P_TILES
        seg_first = jnp.logical_not(is_prompt & ((tile % P_TILES_PER_SEQ) != 0))
        seg_last = jnp.logical_not(is_prompt & ((tile % P_TILES_PER_SEQ) != P_TILES_PER_SEQ - 1))
        h_rows = slice(TM * s, TM * s + TM + 2 * HALO)

        def conv3(cs):
            u = jnp.dot(hbuf[h_rows], wu_ref[:, cs], preferred_element_type=F32)
            p0 = jnp.where(seg_first, 0.0, u[8:16])
            n0 = jnp.where(seg_last, 0.0, u[TM + 16:TM + 24])
            prev = jnp.concatenate([p0, u[16:TM + 8]], axis=0)
            nxt = jnp.concatenate([u[24:TM + 16], n0], axis=0)
            return (cb_ref[:, cs] + prev * cw_ref[0:1, cs] + u[16:TM + 16] * cw_ref[1:2, cs]
                    + nxt * cw_ref[2:3, cs])

        for c in range(D_FF // FF_CHUNK):
            v = conv3(slice(FF_CHUNK * c, FF_CHUNK * (c + 1)))
            gg = conv3(slice(D_FF + FF_CHUNK * c, D_FF + FF_CHUNK * (c + 1)))
            act[s, :, FF_CHUNK * c:FF_CHUNK * (c + 1)] = (v * (gg * jax.nn.sigmoid(gg))).astype(BF16)
            if next_piece < n_pieces:
                fill_piece(next_piece)
                next_piece += 1

        out = jnp.dot(act[s], wd_ref[...], preferred_element_type=F32)
        y = x_ref[TM * s:TM * (s + 1)] + _per_batch(out, gt2)
        if final_norm:
            finals.append(jnp.swapaxes(_rms(y, gf_ref[...]).reshape(STEPS, NB, D), 0, 1))
        else:
            o_ref[TM * s:TM * (s + 1)] = y

    if final_norm:
        y_bt = jnp.concatenate(finals, axis=1)

        @pl.when(k < N_P_TILES // FFN_SUB)
        def _():
            op_ref[...] = y_bt

        @pl.when(k >= N_P_TILES // FFN_SUB)
        def _():
            os_ref[...] = y_bt


def _ffn(x, mod, layer, g, wu, cw, cb, wd, gf, final_norm):
    rows = FFN_SUB * TM
    n_steps = NT // FFN_SUB
    blk = pl.BlockSpec((rows, D), lambda i: (i, 0))
    prev = pl.BlockSpec((HALO, D), lambda i: (jnp.maximum(i * (rows // HALO) - 1, 0), 0))
    nxt = pl.BlockSpec((HALO, D), lambda i: (jnp.minimum((i + 1) * (rows // HALO), N_ROWS // HALO - 1), 0))
    mod_spec = pl.BlockSpec((None, NB, N_MOD * D), lambda i: (jnp.where(i >= N_P_TILES // FFN_SUB, 1, 0), 0, 0))

    def layer_spec(shape):
        n = len(shape)
        return pl.BlockSpec((None,) + shape, lambda *_: (layer,) + (0,) * n, pipeline_mode=pl.Buffered(1))

    if final_norm:
        out_shape = (jax.ShapeDtypeStruct((2, NB, P_LEN, D), F32), jax.ShapeDtypeStruct((NB, S_LEN, D), F32))
        out_specs = _bt_specs(FFN_SUB * STEPS, 0, FFN_SUB * STEPS)
    else:
        out_shape = jax.ShapeDtypeStruct((N_ROWS, D), F32)
        out_specs = blk
    return pl.pallas_call(
        functools.partial(_ffn_kernel, final_norm=final_norm),
        out_shape=out_shape,
        grid=(n_steps,),
        in_specs=[blk, prev, nxt, mod_spec, layer_spec((1, D)),
                  layer_spec((D, 2 * D_FF)), layer_spec((3, 2 * D_FF)), layer_spec((1, 2 * D_FF)),
                  layer_spec((D_FF, D)), _const_spec((1, D))],
        out_specs=out_specs,
        scratch_shapes=[pltpu.VMEM((rows + 2 * HALO, D), BF16), pltpu.VMEM((FFN_SUB, TM, D_FF), BF16)],
        compiler_params=_params(("arbitrary",)),
        name="conv_ffn",
    )(x, x, x, mod, g, wu, cw, cb, wd, gf)


def _s5_core(ubuf, bb_ref, cc_ref, ab_ref, slabs, hs16, hst, y_write, reverse):
    def b_piece(kt, n):
        cs = slice(S5_BN * n, S5_BN * (n + 1))
        slabs[kt % 2, :, cs] = jnp.dot(ubuf[:, 256 * kt:256 * (kt + 1)], bb_ref[kt, :, cs],
                                       preferred_element_type=F32)

    def c_half(kt, half):
        cs = slice(D * half, D * (half + 1))
        return jnp.dot(hs16[kt % 2, :, cs], cc_ref[kt, cs, :], preferred_element_type=F32)

    n_b = SLAB // S5_BN
    every = STEPS // n_b
    quarter = STEPS // 4
    for n in range(n_b):
        b_piece(0, n)
    for kt in range(S5_KT):
        slab = slabs.at[kt % 2]
        hs = hs16.at[kt % 2]
        a_re = jnp.broadcast_to(ab_ref[kt, 0:1, :], (NB, D))
        a_im = jnp.broadcast_to(ab_ref[kt, 1:2, :], (NB, D))
        h_re = hst[kt, :, 0:D]
        h_im = hst[kt, :, D:2 * D]
        part = None
        for s in range(STEPS):
            if s % every == 0 and kt + 1 < S5_KT:
                b_piece(kt + 1, s // every)
            if s == quarter and kt >= 1:
                part = c_half(kt - 1, 0)
            if s == 3 * quarter and kt >= 1:
                y_write(kt - 1, part + c_half(kt - 1, 1))
            t = (STEPS - 1 - s) if reverse else s
            rows = slice(8 * t, 8 * (t + 1))
            n_re = a_re * h_re - a_im * h_im + slab[rows, 0:D]
            n_im = a_re * h_im + a_im * h_re + slab[rows, D:2 * D]
            if s % 2 == 1:
                lo = min(t, t + 1 if reverse else t - 1)
                pair = slice(8 * lo, 8 * lo + 16)
                first, second = ((n_re, h_re), (n_im, h_im)) if reverse else ((h_re, n_re), (h_im, n_im))
                hs[pair, 0:D] = jnp.concatenate(first, axis=0).astype(BF16)
                hs[pair, D:2 * D] = jnp.concatenate(second, axis=0).astype(BF16)
            h_re, h_im = n_re, n_im
        hst[kt, :, 0:D] = h_re
        hst[kt, :, D:2 * D] = h_im
    y_write(S5_KT - 1, c_half(S5_KT - 1, 0) + c_half(S5_KT - 1, 1))


def _s5_bwd_kernel(x_ref, mod_ref, g_ref, bb_ref, cc_ref, ab_ref, h0_ref,
                   yb_ref, fin_ref, ubuf, slabs, hs16, hst):
    j = NT - 1 - pl.program_id(0)

    @pl.when(_is_seq_last(j))
    def _():
        hst[...] = h0_ref[...]

    _fill_norm(ubuf, 0, x_ref, g_ref[...], 1.0 + mod_ref[:, D:2 * D], mod_ref[:, 0:D])

    def y_write(kt, val):
        yb_ref[:, 256 * kt:256 * (kt + 1)] = val

    _s5_core(ubuf, bb_ref, cc_ref, ab_ref, slabs, hs16, hst, y_write, True)
    fin_ref[...] = hst[...]


def _s5_bwd(x, mod, g, bb, cc, ab, h0):
    tile = pl.BlockSpec((TM, D), lambda i: (NT - 1 - i, 0))
    seq = pl.BlockSpec((None, S5_KT, NB, SLAB), lambda i: (_seq_id(NT - 1 - i), 0, 0, 0))
    return pl.pallas_call(
        _s5_bwd_kernel,
        out_shape=(jax.ShapeDtypeStruct((N_ROWS, D), F32),
                   jax.ShapeDtypeStruct((N_SEQ, S5_KT, NB, SLAB), F32)),
        grid=(NT,),
        in_specs=[tile, _mod_spec(lambda i: NT - 1 - i), _const_spec((1, D)),
                  _const_spec((S5_KT, 256, SLAB)), _const_spec((S5_KT, SLAB, 256)),
                  _const_spec((S5_KT, 2, D)), seq],
        out_specs=(tile, seq),
        scratch_shapes=[pltpu.VMEM((TM, D), BF16), pltpu.VMEM((2, TM, SLAB), F32), pltpu.VMEM((2, TM, SLAB), BF16),
                        pltpu.VMEM((S5_KT, NB, SLAB), F32)],
        compiler_params=_params(("arbitrary",)),
        name="s5_bwd",
    )(x, mod, g, bb, cc, ab, h0)


def _s5_fwd_kernel(x_ref, yb_ref, mod_ref, g_ref, bb_ref, cc_ref, ab_ref, h0_ref, dsk_ref, wglu_ref,
                   x1_ref, fin_ref, ubuf, zbuf, u_s, slabs, hs16, hst):
    j = pl.program_id(0)

    @pl.when(_is_seq_first(j))
    def _():
        hst[...] = h0_ref[...]

    _fill_norm(ubuf, 0, x_ref, g_ref[...], 1.0 + mod_ref[:, D:2 * D], mod_ref[:, 0:D], u_s)

    def y_write(kt, val):
        cs = slice(256 * kt, 256 * (kt + 1))
        y = dsk_ref[:, cs] * u_s[:, cs] + yb_ref[:, cs] + val
        zbuf[:, cs] = jax.nn.gelu(y).astype(BF16)

    _s5_core(ubuf, bb_ref, cc_ref, ab_ref, slabs, hs16, hst, y_write, False)
    fin_ref[...] = hst[...]

    for c in range(D // COL_CHUNK):
        cs = slice(COL_CHUNK * c, COL_CHUNK * (c + 1))
        v = jnp.dot(zbuf[...], wglu_ref[:, cs], preferred_element_type=F32)
        gg = jnp.dot(zbuf[...], wglu_ref[:, D + COL_CHUNK * c:D + COL_CHUNK * (c + 1)], preferred_element_type=F32)
        out = v * jax.nn.sigmoid(gg)
        x1_ref[:, cs] = x_ref[:, cs] + _per_batch(out, mod_ref[:, 2 * D + COL_CHUNK * c:2 * D + COL_CHUNK * (c + 1)])


def _s5_fwd(x, yb, mod, g, bb, cc, ab, h0, dsk, wglu):
    tile = pl.BlockSpec((TM, D), lambda i: (i, 0))
    seq = pl.BlockSpec((None, S5_KT, NB, SLAB), lambda i: (_seq_id(i), 0, 0, 0))
    return pl.pallas_call(
        _s5_fwd_kernel,
        out_shape=(jax.ShapeDtypeStruct((N_ROWS, D), F32),
                   jax.ShapeDtypeStruct((N_SEQ, S5_KT, NB, SLAB), F32)),
        grid=(NT,),
        in_specs=[tile, tile, _mod_spec(lambda i: i), _const_spec((1, D)),
                  _const_spec((S5_KT, 256, SLAB)), _const_spec((S5_KT, SLAB, 256)),
                  _const_spec((S5_KT, 2, D)), seq, _const_spec((1, D)), _const_spec((D, 2 * D))],
        out_specs=(tile, seq),
        scratch_shapes=[pltpu.VMEM((TM, D), BF16), pltpu.VMEM((TM, D), BF16), pltpu.VMEM((TM, D), F32),
                        pltpu.VMEM((2, TM, SLAB), F32), pltpu.VMEM((2, TM, SLAB), BF16),
                        pltpu.VMEM((S5_KT, NB, SLAB), F32)],
        compiler_params=_params(("arbitrary",)),
        name="s5_fwd",
    )(x, yb, mod, g, bb, cc, ab, h0, dsk, wglu)


def _pair_blockdiag(w_a, w_i):
    same = (jnp.arange(2)[:, None, None, None] == jnp.arange(2)[None, None, :, None])

    def bd(w):
        w4 = w.astype(F32).reshape(8, 2, 64, 1, 64)
        return jnp.where(same[None], w4, 0.0).reshape(8, 128, 128)

    return (0.5 * jnp.concatenate([bd(w_a), bd(w_i)], axis=-1)).astype(BF16)


def _s5_dir_params(a_re, a_im, log_dt, b_re, b_im, c_re, c_im):
    l_re = a_re.astype(F32)
    l_im = a_im.astype(F32)
    dt = jnp.exp(log_dt.astype(F32))[:, None]
    mag = jnp.exp(l_re * dt)
    ab_re = mag * jnp.cos(l_im * dt)
    ab_im = mag * jnp.sin(l_im * dt)
    den = l_re * l_re + l_im * l_im
    k_re = ((ab_re - 1.0) * l_re + ab_im * l_im) / den
    k_im = (ab_im * l_re - (ab_re - 1.0) * l_im) / den
    br = b_re.astype(F32)
    bi = b_im.astype(F32)
    bb_re = br * k_re[..., None] - bi * k_im[..., None]
    bb_im = br * k_im[..., None] + bi * k_re[..., None]
    same = (jnp.arange(16)[:, None, None, None] == jnp.arange(16)[None, None, :, None])[None]

    def b_blk(m):
        m5 = m.reshape(S5_KT, 16, 64, 16).transpose(0, 1, 3, 2)[:, :, :, None, :]
        return jnp.where(same, m5, 0.0).reshape(S5_KT, 256, 1024)

    def c_blk(m):
        m5 = m.reshape(S5_KT, 16, 16, 64).transpose(0, 1, 3, 2)[:, :, :, None, :]
        return jnp.where(same, m5, 0.0).reshape(S5_KT, 1024, 256)

    bb = jnp.concatenate([b_blk(bb_re), b_blk(bb_im)], axis=-1).astype(BF16)
    cc = jnp.concatenate([c_blk(c_re.astype(F32)), c_blk(-c_im.astype(F32))], axis=1).astype(BF16)
    ab = jnp.stack([ab_re.reshape(S5_KT, D), ab_im.reshape(S5_KT, D)], axis=1)
    return bb, cc, ab


def _s5_state_to_slab(s_re, s_im):
    re = s_re.astype(F32).reshape(NB, S5_KT, D).transpose(1, 0, 2)
    im = s_im.astype(F32).reshape(NB, S5_KT, D).transpose(1, 0, 2)
    return jnp.concatenate([re, im], axis=-1)


def _s5_slab_to_state(fin):
    re = fin[..., 0:D].transpose(0, 2, 1, 3).reshape(2 * NB, 64, 64)
    im = fin[..., D:2 * D].transpose(0, 2, 1, 3).reshape(2 * NB, 64, 64)
    return re, im


def kernel(x_prompt, x_sample, state_lru, state_s5_re, state_s5_im, c, c_ctx, ada_w, ada_b, norm_mix, norm_ffn, norm_final, lru_w_in, lru_conv_w, lru_conv_b, lru_w_a, lru_b_a, lru_w_i, lru_b_i, lru_lambda, lru_w_out, s5_a_re, s5_a_im, s5_log_dt, s5_b_re, s5_b_im, s5_c_re, s5_c_im, s5_d, s5_w_glu, ffn_w_up, ffn_conv_w, ffn_conv_b, ffn_w_down):
    c_all = jnp.concatenate([c.astype(F32), c_ctx.astype(F32)[None], jnp.zeros((7, D), F32)], axis=0)
    mods = _mod_vectors(c_all, ada_w.astype(F32), ada_b.astype(F32)[:, None, :])

    def mod_of(l):
        return jnp.stack([jnp.broadcast_to(mods[l, 8][None], (NB, N_MOD * D)), mods[l, 0:NB]], axis=0)

    xp4 = x_prompt.astype(F32).reshape(2, NB, P_LEN, D)
    gf = norm_final[None].astype(F32)

    mod0 = mod_of(0)
    zeros_h = jnp.zeros((NB, D), F32)
    sp = jax.nn.softplus(-lru_lambda[0].astype(F32)) * (-LRU_C * LOG2_E * 0.5)
    ba = 0.5 * lru_b_a[0].astype(F32)
    bi = 0.5 * lru_b_i[0].astype(F32)
    h0_f = jnp.stack([zeros_h, zeros_h, state_lru[:, 0, 0].astype(F32)], axis=0)
    h0_b = jnp.stack([zeros_h, zeros_h, state_lru[:, 0, 1].astype(F32)], axis=0)
    gate, xr, x, hsb, fin_b = _lru_a(xp4, x_sample.astype(F32), mod0, norm_mix[0][None].astype(F32),
                                     lru_w_in[0].astype(BF16), lru_conv_w[0].astype(F32),
                                     lru_conv_b[0][None].astype(F32),
                                     _pair_blockdiag(lru_w_a[0, 1], lru_w_i[0, 1]), ba[1][None], bi[1][None],
                                     sp[1][None], h0_b)
    x, fin_f = _lru_fwd(xr, gate, hsb, x, mod0, _pair_blockdiag(lru_w_a[0, 0], lru_w_i[0, 0]),
                        ba[0][None], bi[0][None], sp[0][None], h0_f, lru_w_out[0].astype(BF16))
    ffn_args = (norm_ffn.astype(F32)[:, None, :], ffn_w_up.astype(BF16), ffn_conv_w.astype(F32),
                ffn_conv_b.astype(F32)[:, None, :], ffn_w_down.astype(BF16), gf)
    x = _ffn(x, mod0, 0, *ffn_args, False)
    new_lru = jnp.stack([fin_f[0:2].reshape(2 * NB, D), fin_b[0:2].reshape(2 * NB, D)], axis=1)[:, None]

    mod1 = mod_of(1)
    g1 = norm_mix[1][None].astype(F32)
    zeros_s = jnp.zeros((S5_KT, NB, SLAB), F32)
    dirs = []
    for d in range(2):
        bb, cc, ab = _s5_dir_params(s5_a_re[0, d], s5_a_im[0, d], s5_log_dt[0, d], s5_b_re[0, d], s5_b_im[0, d],
                                    s5_c_re[0, d], s5_c_im[0, d])
        h0 = jnp.stack([zeros_s, zeros_s, _s5_state_to_slab(state_s5_re[:, 0, d], state_s5_im[:, 0, d])], axis=0)
        dirs.append((bb, cc, ab, h0))
    yb, fin_sb = _s5_bwd(x, mod1, g1, *dirs[1])
    x, fin_sf = _s5_fwd(x, yb, mod1, g1, *dirs[0], s5_d[0][None].astype(F32), s5_w_glu[0].astype(BF16))
    y_p4, y_sample = _ffn(x, mod1, 1, *ffn_args, True)
    f_re, f_im = _s5_slab_to_state(fin_sf[0:2])
    b_re, b_im = _s5_slab_to_state(fin_sb[0:2])
    new_s5_re = jnp.stack([f_re, b_re], axis=1)[:, None]
    new_s5_im = jnp.stack([f_im, b_im], axis=1)[:, None]

    return (y_p4.reshape(2 * NB, P_LEN, D), y_sample, new_lru, new_s5_re, new_s5_im)
```

```python
import functools

import jax
import jax.numpy as jnp
from jax import lax
from jax.experimental import pallas as pl
from jax.experimental.pallas import tpu as pltpu

F32 = jnp.float32
BF16 = jnp.bfloat16

D = 1024
D_FF = 2816
N_MOD = 6
EPS = 1e-6
LRU_C = 8.0
LOG2_E = 1.4426950408889634
NB = 8
STEPS = 64
TM = STEPS * NB
HALO = 16
NORM_ROWS = 64
FF_CHUNK = 256
FFN_SUB = 2
COL_CHUNK = 256
P_LEN = 256
S_LEN = 4096
P_TILES_PER_SEQ = P_LEN // STEPS
N_P_TILES = 2 * P_TILES_PER_SEQ
N_S_TILES = S_LEN // STEPS
NT = N_P_TILES + N_S_TILES
N_ROWS = NT * TM
N_SEQ = 3
S5_KT = 4
SLAB = 2048
S5_BN = 256
VMEM_LIMIT = 56 * 1024 * 1024


def _seq_id(j):
    return jnp.where(j >= N_P_TILES, 2, jnp.where(j >= P_TILES_PER_SEQ, 1, 0))


def _is_seq_first(j):
    return (j == 0) | (j == P_TILES_PER_SEQ) | (j == N_P_TILES)


def _is_seq_last(j):
    return (j == P_TILES_PER_SEQ - 1) | (j == N_P_TILES - 1) | (j == NT - 1)


def _params(sem):
    return pltpu.CompilerParams(dimension_semantics=sem, vmem_limit_bytes=VMEM_LIMIT)


def _const_spec(shape):
    n = len(shape)
    return pl.BlockSpec(shape, lambda *_: (0,) * n, pipeline_mode=pl.Buffered(1))


def _mod_spec(tile_of):
    return pl.BlockSpec((None, NB, N_MOD * D),
                        lambda i: (jnp.where(tile_of(i) >= N_P_TILES, 1, 0), 0, 0))


def _rms(xv, g):
    ms = jnp.mean(xv * xv, axis=-1, keepdims=True)
    return xv * lax.rsqrt(ms + EPS) * g


def _per_batch(y, vec):
    r, n = y.shape
    return (y.reshape(r // NB, NB, n) * vec[None]).reshape(r, n)


def _norm_mod(xv, g, scale1, shift):
    r = xv.shape[0]
    y = _rms(xv, g).reshape(r // NB, NB, D) * scale1[None] + shift[None]
    return y.reshape(r, D)


def _fill_norm(dst, dst_off, x_ref, g, scale1, shift, f32_dst=None):
    for k in range(TM // NORM_ROWS):
        r0 = k * NORM_ROWS
        y = _norm_mod(x_ref[r0:r0 + NORM_ROWS], g, scale1, shift)
        dst[dst_off + r0:dst_off + r0 + NORM_ROWS] = y.astype(BF16)
        if f32_dst is not None:
            f32_dst[r0:r0 + NORM_ROWS] = y


def _fill_hbuf(hbuf, x_ref, x_prev, x_next, g, scale1, shift):
    hbuf[0:HALO] = _norm_mod(x_prev, g, scale1, shift).astype(BF16)
    hbuf[HALO + TM:HALO + TM + HALO] = _norm_mod(x_next, g, scale1, shift).astype(BF16)
    _fill_norm(hbuf, HALO, x_ref, g, scale1, shift)


def _mod_kernel(c_ref, w_ref, b_ref, o_ref):
    cv = c_ref[...]
    s = (cv * jax.nn.sigmoid(cv)).astype(BF16)
    o_ref[...] = jnp.dot(s, w_ref[...].astype(BF16), preferred_element_type=F32) + b_ref[...]


def _mod_vectors(c_all, ada_w, ada_b):
    depth = ada_w.shape[0]
    return pl.pallas_call(
        _mod_kernel,
        out_shape=jax.ShapeDtypeStruct((depth, 16, N_MOD * D), F32),
        grid=(depth, N_MOD),
        in_specs=[
            pl.BlockSpec((16, D), lambda l, n: (0, 0)),
            pl.BlockSpec((None, D, D), lambda l, n: (l, 0, n)),
            pl.BlockSpec((None, 1, D), lambda l, n: (l, 0, n)),
        ],
        out_specs=pl.BlockSpec((None, 16, D), lambda l, n: (l, 0, n)),
        compiler_params=_params(("arbitrary", "arbitrary")),
        name="mod_vectors",
    )(c_all, ada_w, ada_b)


def _bt_specs(steps, offset, tile_steps=STEPS, tile_of=lambda i: i):
    per_tile = tile_steps // steps
    tiles_per_seq = P_LEN // tile_steps
    n_p = 2 * tiles_per_seq

    def prompt_idx(i):
        ip = jnp.minimum(tile_of(i), n_p - 1)
        blk = jnp.clip((ip % tiles_per_seq) * per_tile + offset, 0, P_LEN // steps - 1)
        return (ip // tiles_per_seq, 0, blk, 0)

    def sample_idx(i):
        blk = jnp.clip((tile_of(i) - n_p) * per_tile + offset, 0, S_LEN // steps - 1)
        return (0, blk, 0)

    return (pl.BlockSpec((None, NB, steps, D), prompt_idx), pl.BlockSpec((NB, steps, D), sample_idx))


def _to_rows(x_bt):
    n = x_bt.shape[1]
    return jnp.swapaxes(x_bt, 0, 1).reshape(n * NB, D)


def _lru_gate_math(z, ba, bi, sp, xr):
    t_r = jnp.tanh(z[:, 0:128] + ba)
    t_i = jnp.tanh(z[:, 128:256] + bi)
    a = jnp.exp2(sp + sp * t_r)
    s = 1.0 - a * a
    root = jnp.where(s > 0.0, s * lax.rsqrt(s), 0.0)
    return a, root * ((0.5 + 0.5 * t_i) * xr)


def _lru_scan(a_s, bx_s, h_s, out_ref, reverse):
    h = h_s[...]
    for s in range(STEPS):
        t = (STEPS - 1 - s) if reverse else s
        rows = slice(8 * t, 8 * (t + 1))
        h = a_s[rows] * h + bx_s[rows]
        out_ref[rows] = h
    h_s[...] = h


def _lru_a_kernel(xa_ref, xap_ref, xan_ref, xb_ref, xbp_ref, xbn_ref, mod_ref, g_ref, wi_ref, cw_ref, cb_ref,
                  wp_ref, ba_ref, bi_ref, sp_ref, h0_ref,
                  gate_ref, xr_ref, hs_ref, fin_ref, hbuf, xc_s, a_s, bx_s, h_s):
    j = NT - 1 - pl.program_id(0)

    @pl.when(_is_seq_last(j))
    def _():
        h_s[...] = h0_ref[...]

    shift = mod_ref[:, 0:D]
    scale1 = 1.0 + mod_ref[:, D:2 * D]

    is_p = j < N_P_TILES
    xc_s[...] = _to_rows(jnp.where(is_p, xa_ref[...], xb_ref[...]))
    x_prev = _to_rows(jnp.where(is_p, xap_ref[...], xbp_ref[...]))[NB * NB - HALO:NB * NB]
    x_next = _to_rows(jnp.where(is_p, xan_ref[...], xbn_ref[...]))[0:HALO]
    _fill_hbuf(hbuf, xc_s, x_prev, x_next, g_ref[...], scale1, shift)

    first = _is_seq_first(j)
    last = _is_seq_last(j)

    for c in range(D // COL_CHUNK):
        cs = slice(COL_CHUNK * c, COL_CHUNK * (c + 1))
        cx = slice(D + COL_CHUNK * c, D + COL_CHUNK * (c + 1))
        gate_ref[:, cs] = jax.nn.gelu(
            jnp.dot(hbuf[HALO:HALO + TM], wi_ref[:, cs], preferred_element_type=F32)).astype(BF16)
        u = jnp.dot(hbuf[...], wi_ref[:, cx], preferred_element_type=F32)
        head = jnp.where(first, 0.0, u[0:16])
        tail = jnp.where(last, 0.0, u[TM + 16:TM + 24])
        m2 = jnp.concatenate([head, u[16:TM]], axis=0)
        m1 = jnp.concatenate([head[8:16], u[16:TM + 8]], axis=0)
        p1 = jnp.concatenate([u[24:TM + 16], tail], axis=0)
        xr = (cb_ref[:, cs] + m2 * cw_ref[0:1, cs] + m1 * cw_ref[1:2, cs]
              + u[16:TM + 16] * cw_ref[2:3, cs] + p1 * cw_ref[3:4, cs])
        xr16 = xr.astype(BF16)
        xr_ref[:, cs] = xr16
        for h in range(COL_CHUNK // 128):
            p = (COL_CHUNK // 128) * c + h
            cols = slice(128 * p, 128 * (p + 1))
            z = jnp.dot(xr16[:, 128 * h:128 * (h + 1)], wp_ref[p], preferred_element_type=F32)
            a, bx = _lru_gate_math(z, ba_ref[:, cols], bi_ref[:, cols], sp_ref[:, cols],
                                   xr[:, 128 * h:128 * (h + 1)])
            a_s[:, cols] = a
            bx_s[:, cols] = bx

    _lru_scan(a_s, bx_s, h_s, bx_s, True)
    hs_ref[...] = bx_s[...].astype(BF16)
    fin_ref[...] = h_s[...]


def _lru_a(xp4, xs, mod, g, wi, cw, cb, wp, ba, bi, sp, h0):
    rev = lambda i: NT - 1 - i
    p_tile, s_tile = _bt_specs(STEPS, 0, tile_of=rev)
    p_prev, s_prev = _bt_specs(NB, -1, tile_of=rev)
    p_next, s_next = _bt_specs(NB, STEPS // NB, tile_of=rev)
    tile = pl.BlockSpec((TM, D), lambda i: (rev(i), 0))
    seq = pl.BlockSpec((None, NB, D), lambda i: (_seq_id(rev(i)), 0, 0))
    rows16 = jax.ShapeDtypeStruct((N_ROWS, D), BF16)
    return pl.pallas_call(
        _lru_a_kernel,
        out_shape=(rows16, rows16, rows16, jax.ShapeDtypeStruct((N_SEQ, NB, D), F32)),
        grid=(NT,),
        in_specs=[p_tile, p_prev, p_next, s_tile, s_prev, s_next,
                  _mod_spec(rev), _const_spec((1, D)),
                  _const_spec((D, 2 * D)), _const_spec((4, D)), _const_spec((1, D)),
                  _const_spec((8, 128, 256)), _const_spec((1, D)), _const_spec((1, D)), _const_spec((1, D)), seq],
        out_specs=(tile, tile, tile, seq),
        scratch_shapes=[pltpu.VMEM((TM + 2 * HALO, D), BF16), pltpu.VMEM((TM, D), F32), pltpu.VMEM((TM, D), F32),
                        pltpu.VMEM((TM, D), F32), pltpu.VMEM((NB, D), F32)],
        compiler_params=_params(("arbitrary",)),
        name="lru_a",
    )(xp4, xp4, xp4, xs, xs, xs, mod, g, wi, cw, cb, wp, ba, bi, sp, h0)


def _lru_fwd_kernel(xr_ref, gate_ref, hsb_ref, xa_ref, xb_ref, mod_ref, wp_ref, ba_ref, bi_ref, sp_ref, h0_ref,
                    wo_ref, x1_ref, fin_ref, a_s, bx_s, y_s, h_s):
    j = pl.program_id(0)

    @pl.when(_is_seq_first(j))
    def _():
        h_s[...] = h0_ref[...]

    for p in range(8):
        cols = slice(128 * p, 128 * (p + 1))
        z = jnp.dot(xr_ref[:, cols], wp_ref[p], preferred_element_type=F32)
        a, bx = _lru_gate_math(z, ba_ref[:, cols], bi_ref[:, cols], sp_ref[:, cols],
                               xr_ref[:, cols].astype(F32))
        a_s[:, cols] = a
        bx_s[:, cols] = bx
    _lru_scan(a_s, bx_s, h_s, bx_s, False)
    fin_ref[...] = h_s[...]

    y_s[...] = ((bx_s[...] + hsb_ref[...].astype(F32)) * gate_ref[...].astype(F32)).astype(BF16)
    x_rows = _to_rows(jnp.where(j < N_P_TILES, xa_ref[...], xb_ref[...]))
    for c in range(D // COL_CHUNK):
        cs = slice(COL_CHUNK * c, COL_CHUNK * (c + 1))
        out = jnp.dot(y_s[...], wo_ref[:, cs], preferred_element_type=F32)
        x1_ref[:, cs] = x_rows[:, cs] + _per_batch(out, mod_ref[:, 2 * D + COL_CHUNK * c:2 * D + COL_CHUNK * (c + 1)])


def _lru_fwd(xr, gate, hsb, xp4, xs, mod, wp, ba, bi, sp, h0, wo):
    tile = pl.BlockSpec((TM, D), lambda i: (i, 0))
    p_tile, s_tile = _bt_specs(STEPS, 0)
    seq = pl.BlockSpec((None, NB, D), lambda i: (_seq_id(i), 0, 0))
    return pl.pallas_call(
        _lru_fwd_kernel,
        out_shape=(jax.ShapeDtypeStruct((N_ROWS, D), F32), jax.ShapeDtypeStruct((N_SEQ, NB, D), F32)),
        grid=(NT,),
        in_specs=[tile, tile, tile, p_tile, s_tile, _mod_spec(lambda i: i), _const_spec((8, 128, 256)),
                  _const_spec((1, D)), _const_spec((1, D)), _const_spec((1, D)), seq, _const_spec((D, D))],
        out_specs=(tile, seq),
        scratch_shapes=[pltpu.VMEM((TM, D), F32), pltpu.VMEM((TM, D), F32), pltpu.VMEM((TM, D), BF16),
                        pltpu.VMEM((NB, D), F32)],
        compiler_params=_params(("arbitrary",)),
        name="lru_fwd",
    )(xr, gate, hsb, xp4, xs, mod, wp, ba, bi, sp, h0, wo)


def _ffn_kernel(x_ref, xp_ref, xn_ref, mod_ref, g_ref, wu_ref, cw_ref, cb_ref, wd_ref, gf_ref,
                *rest, final_norm):
    if final_norm:
        op_ref, os_ref, hbuf, act = rest
    else:
        o_ref, hbuf, act = rest
    k = pl.program_id(0)
    g = g_ref[...]
    shift = mod_ref[:, 3 * D:4 * D]
    scale1 = 1.0 + mod_ref[:, 4 * D:5 * D]
    gt2 = mod_ref[:, 5 * D:6 * D]
    rows_all = FFN_SUB * TM

    def fill_piece(p):
        n_blocks = rows_all // NORM_ROWS
        if p == 0:
            hbuf[0:HALO] = _norm_mod(xp_ref[...], g, scale1, shift).astype(BF16)
        elif p <= n_blocks:
            r0 = (p - 1) * NORM_ROWS
            hbuf[HALO + r0:HALO + r0 + NORM_ROWS] = _norm_mod(
                x_ref[r0:r0 + NORM_ROWS], g, scale1, shift).astype(BF16)
        elif p == n_blocks + 1:
            hbuf[HALO + rows_all:HALO + rows_all + HALO] = _norm_mod(xn_ref[...], g, scale1, shift).astype(BF16)

    n_pieces = rows_all // NORM_ROWS + 2
    first_pieces = TM // NORM_ROWS + 2
    for p in range(first_pieces):
        fill_piece(p)
    next_piece = first_pieces
    finals = []

    for s in range(FFN_SUB):
        tile = FFN_SUB * k + s
        is_prompt = tile < N_P_TILES
        seg_first = jnp.logical_not(is_prompt & ((tile % P_TILES_PER_SEQ) != 0))
        seg_last = jnp.logical_not(is_prompt & ((tile % P_TILES_PER_SEQ) != P_TILES_PER_SEQ - 1))
        h_rows = slice(TM * s, TM * s + TM + 2 * HALO)

        def conv3(cs):
            u = jnp.dot(hbuf[h_rows], wu_ref[:, cs], preferred_element_type=F32)
            p0 = jnp.where(seg_first, 0.0, u[8:16])
            n0 = jnp.where(seg_last, 0.0, u[TM + 16:TM + 24])
            prev = jnp.concatenate([p0, u[16:TM + 8]], axis=0)
            nxt = jnp.concatenate([u[24:TM + 16], n0], axis=0)
            return (cb_ref[:, cs] + prev * cw_ref[0:1, cs] + u[16:TM + 16] * cw_ref[1:2, cs]
                    + nxt * cw_ref[2:3, cs])

        for c in range(D_FF // FF_CHUNK):
            v = conv3(slice(FF_CHUNK * c, FF_CHUNK * (c + 1)))
            gg = conv3(slice(D_FF + FF_CHUNK * c, D_FF + FF_CHUNK * (c + 1)))
            act[s, :, FF_CHUNK * c:FF_CHUNK * (c + 1)] = (v * (gg * jax.nn.sigmoid(gg))).astype(BF16)
            if next_piece < n_pieces:
                fill_piece(next_piece)
                next_piece += 1

        out = jnp.dot(act[s], wd_ref[...], preferred_element_type=F32)
        y = x_ref[TM * s:TM * (s + 1)] + _per_batch(out, gt2)
        if final_norm:
            finals.append(jnp.swapaxes(_rms(y, gf_ref[...]).reshape(STEPS, NB, D), 0, 1))
        else:
            o_ref[TM * s:TM * (s + 1)] = y

    if final_norm:
        y_bt = jnp.concatenate(finals, axis=1)

        @pl.when(k < N_P_TILES // FFN_SUB)
        def _():
            op_ref[...] = y_bt

        @pl.when(k >= N_P_TILES // FFN_SUB)
        def _():
            os_ref[...] = y_bt


def _ffn(x, mod, layer, g, wu, cw, cb, wd, gf, final_norm):
    rows = FFN_SUB * TM
    n_steps = NT // FFN_SUB
    blk = pl.BlockSpec((rows, D), lambda i: (i, 0))
    prev = pl.BlockSpec((HALO, D), lambda i: (jnp.maximum(i * (rows // HALO) - 1, 0), 0))
    nxt = pl.BlockSpec((HALO, D), lambda i: (jnp.minimum((i + 1) * (rows // HALO), N_ROWS // HALO - 1), 0))
    mod_spec = pl.BlockSpec((None, NB, N_MOD * D), lambda i: (jnp.where(i >= N_P_TILES // FFN_SUB, 1, 0), 0, 0))

    def layer_spec(shape):
        n = len(shape)
        return pl.BlockSpec((None,) + shape, lambda *_: (layer,) + (0,) * n, pipeline_mode=pl.Buffered(1))

    if final_norm:
        out_shape = (jax.ShapeDtypeStruct((2, NB, P_LEN, D), F32), jax.ShapeDtypeStruct((NB, S_LEN, D), F32))
        out_specs = _bt_specs(FFN_SUB * STEPS, 0, FFN_SUB * STEPS)
    else:
        out_shape = jax.ShapeDtypeStruct((N_ROWS, D), F32)
        out_specs = blk
    return pl.pallas_call(
        functools.partial(_ffn_kernel, final_norm=final_norm),
        out_shape=out_shape,
        grid=(n_steps,),
        in_specs=[blk, prev, nxt, mod_spec, layer_spec((1, D)),
                  layer_spec((D, 2 * D_FF)), layer_spec((3, 2 * D_FF)), layer_spec((1, 2 * D_FF)),
                  layer_spec((D_FF, D)), _const_spec((1, D))],
        out_specs=out_specs,
        scratch_shapes=[pltpu.VMEM((rows + 2 * HALO, D), BF16), pltpu.VMEM((FFN_SUB, TM, D_FF), BF16)],
        compiler_params=_params(("arbitrary",)),
        name="conv_ffn",
    )(x, x, x, mod, g, wu, cw, cb, wd, gf)


def _s5_core(ubuf, bb_ref, cc_ref, ab_ref, slabs, hs16, hst, y_write, reverse):
    def b_piece(kt, n):
        cs = slice(S5_BN * n, S5_BN * (n + 1))
        slabs[kt % 2, :, cs] = jnp.dot(ubuf[:, 256 * kt:256 * (kt + 1)], bb_ref[kt, :, cs],
                                       preferred_element_type=F32)

    def c_half(kt, half):
        cs = slice(D * half, D * (half + 1))
        return jnp.dot(hs16[kt % 2, :, cs], cc_ref[kt, cs, :], preferred_element_type=F32)

    n_b = SLAB // S5_BN
    every = STEPS // n_b
    quarter = STEPS // 4
    for n in range(n_b):
        b_piece(0, n)
    for kt in range(S5_KT):
        slab = slabs.at[kt % 2]
        hs = hs16.at[kt % 2]
        a_re = jnp.broadcast_to(ab_ref[kt, 0:1, :], (NB, D))
        a_im = jnp.broadcast_to(ab_ref[kt, 1:2, :], (NB, D))
        h_re = hst[kt, :, 0:D]
        h_im = hst[kt, :, D:2 * D]
        part = None
        for s in range(STEPS):
            if s % every == 0 and kt + 1 < S5_KT:
                b_piece(kt + 1, s // every)
            if s == quarter and kt >= 1:
                part = c_half(kt - 1, 0)
            if s == 3 * quarter and kt >= 1:
                y_write(kt - 1, part + c_half(kt - 1, 1))
            t = (STEPS - 1 - s) if reverse else s
            rows = slice(8 * t, 8 * (t + 1))
            n_re = a_re * h_re - a_im * h_im + slab[rows, 0:D]
            n_im = a_re * h_im + a_im * h_re + slab[rows, D:2 * D]
            if s % 2 == 1:
                lo = min(t, t + 1 if reverse else t - 1)
                pair = slice(8 * lo, 8 * lo + 16)
                first, second = ((n_re, h_re), (n_im, h_im)) if reverse else ((h_re, n_re), (h_im, n_im))
                hs[pair, 0:D] = jnp.concatenate(first, axis=0).astype(BF16)
                hs[pair, D:2 * D] = jnp.concatenate(second, axis=0).astype(BF16)
            h_re, h_im = n_re, n_im
        hst[kt, :, 0:D] = h_re
        hst[kt, :, D:2 * D] = h_im
    y_write(S5_KT - 1, c_half(S5_KT - 1, 0) + c_half(S5_KT - 1, 1))


def _s5_bwd_kernel(x_ref, mod_ref, g_ref, bb_ref, cc_ref, ab_ref, h0_ref,
                   yb_ref, fin_ref, ubuf, slabs, hs16, hst):
    j = NT - 1 - pl.program_id(0)

    @pl.when(_is_seq_last(j))
    def _():
        hst[...] = h0_ref[...]

    _fill_norm(ubuf, 0, x_ref, g_ref[...], 1.0 + mod_ref[:, D:2 * D], mod_ref[:, 0:D])

    def y_write(kt, val):
        yb_ref[:, 256 * kt:256 * (kt + 1)] = val

    _s5_core(ubuf, bb_ref, cc_ref, ab_ref, slabs, hs16, hst, y_write, True)
    fin_ref[...] = hst[...]


def _s5_bwd(x, mod, g, bb, cc, ab, h0):
    tile = pl.BlockSpec((TM, D), lambda i: (NT - 1 - i, 0))
    seq = pl.BlockSpec((None, S5_KT, NB, SLAB), lambda i: (_seq_id(NT - 1 - i), 0, 0, 0))
    return pl.pallas_call(
        _s5_bwd_kernel,
        out_shape=(jax.ShapeDtypeStruct((N_ROWS, D), F32),
                   jax.ShapeDtypeStruct((N_SEQ, S5_KT, NB, SLAB), F32)),
        grid=(NT,),
        in_specs=[tile, _mod_spec(lambda i: NT - 1 - i), _const_spec((1, D)),
                  _const_spec((S5_KT, 256, SLAB)), _const_spec((S5_KT, SLAB, 256)),
                  _const_spec((S5_KT, 2, D)), seq],
        out_specs=(tile, seq),
        scratch_shapes=[pltpu.VMEM((TM, D), BF16), pltpu.VMEM((2, TM, SLAB), F32), pltpu.VMEM((2, TM, SLAB), BF16),
                        pltpu.VMEM((S5_KT, NB, SLAB), F32)],
        compiler_params=_params(("arbitrary",)),
        name="s5_bwd",
    )(x, mod, g, bb, cc, ab, h0)


def _s5_fwd_kernel(x_ref, yb_ref, mod_ref, g_ref, bb_ref, cc_ref, ab_ref, h0_ref, dsk_ref, wglu_ref,
                   x1_ref, fin_ref, ubuf, zbuf, u_s, slabs, hs16, hst):
    j = pl.program_id(0)

    @pl.when(_is_seq_first(j))
    def _():
        hst[...] = h0_ref[...]

    _fill_norm(ubuf, 0, x_ref, g_ref[...], 1.0 + mod_ref[:, D:2 * D], mod_ref[:, 0:D], u_s)

    def y_write(kt, val):
        cs = slice(256 * kt, 256 * (kt + 1))
        y = dsk_ref[:, cs] * u_s[:, cs] + yb_ref[:, cs] + val
        zbuf[:, cs] = jax.nn.gelu(y).astype(BF16)

    _s5_core(ubuf, bb_ref, cc_ref, ab_ref, slabs, hs16, hst, y_write, False)
    fin_ref[...] = hst[...]

    for c in range(D // COL_CHUNK):
        cs = slice(COL_CHUNK * c, COL_CHUNK * (c + 1))
        v = jnp.dot(zbuf[...], wglu_ref[:, cs], preferred_element_type=F32)
        gg = jnp.dot(zbuf[...], wglu_ref[:, D + COL_CHUNK * c:D + COL_CHUNK * (c + 1)], preferred_element_type=F32)
        out = v * jax.nn.sigmoid(gg)
        x1_ref[:, cs] = x_ref[:, cs] + _per_batch(out, mod_ref[:, 2 * D + COL_CHUNK * c:2 * D + COL_CHUNK * (c + 1)])


def _s5_fwd(x, yb, mod, g, bb, cc, ab, h0, dsk, wglu):
    tile = pl.BlockSpec((TM, D), lambda i: (i, 0))
    seq = pl.BlockSpec((None, S5_KT, NB, SLAB), lambda i: (_seq_id(i), 0, 0, 0))
    return pl.pallas_call(
        _s5_fwd_kernel,
        out_shape=(jax.ShapeDtypeStruct((N_ROWS, D), F32),
                   jax.ShapeDtypeStruct((N_SEQ, S5_KT, NB, SLAB), F32)),
        grid=(NT,),
        in_specs=[tile, tile, _mod_spec(lambda i: i), _const_spec((1, D)),
                  _const_spec((S5_KT, 256, SLAB)), _const_spec((S5_KT, SLAB, 256)),
                  _const_spec((S5_KT, 2, D)), seq, _const_spec((1, D)), _const_spec((D, 2 * D))],
        out_specs=(tile, seq),
        scratch_shapes=[pltpu.VMEM((TM, D), BF16), pltpu.VMEM((TM, D), BF16), pltpu.VMEM((TM, D), F32),
                        pltpu.VMEM((2, TM, SLAB), F32), pltpu.VMEM((2, TM, SLAB), BF16),
                        pltpu.VMEM((S5_KT, NB, SLAB), F32)],
        compiler_params=_params(("arbitrary",)),
        name="s5_fwd",
    )(x, yb, mod, g, bb, cc, ab, h0, dsk, wglu)


def _pair_blockdiag(w_a, w_i):
    same = (jnp.arange(2)[:, None, None, None] == jnp.arange(2)[None, None, :, None])

    def bd(w):
        w4 = w.astype(F32).reshape(8, 2, 64, 1, 64)
        return jnp.where(same[None], w4, 0.0).reshape(8, 128, 128)

    return (0.5 * jnp.concatenate([bd(w_a), bd(w_i)], axis=-1)).astype(BF16)


def _s5_dir_params(a_re, a_im, log_dt, b_re, b_im, c_re, c_im):
    l_re = a_re.astype(F32)
    l_im = a_im.astype(F32)
    dt = jnp.exp(log_dt.astype(F32))[:, None]
    mag = jnp.exp(l_re * dt)
    ab_re = mag * jnp.cos(l_im * dt)
    ab_im = mag * jnp.sin(l_im * dt)
    den = l_re * l_re + l_im * l_im
    k_re = ((ab_re - 1.0) * l_re + ab_im * l_im) / den
    k_im = (ab_im * l_re - (ab_re - 1.0) * l_im) / den
    br = b_re.astype(F32)
    bi = b_im.astype(F32)
    bb_re = br * k_re[..., None] - bi * k_im[..., None]
    bb_im = br * k_im[..., None] + bi * k_re[..., None]
    same = (jnp.arange(16)[:, None, None, None] == jnp.arange(16)[None, None, :, None])[None]

    def b_blk(m):
        m5 = m.reshape(S5_KT, 16, 64, 16).transpose(0, 1, 3, 2)[:, :, :, None, :]
        return jnp.where(same, m5, 0.0).reshape(S5_KT, 256, 1024)

    def c_blk(m):
        m5 = m.reshape(S5_KT, 16, 16, 64).transpose(0, 1, 3, 2)[:, :, :, None, :]
        return jnp.where(same, m5, 0.0).reshape(S5_KT, 1024, 256)

    bb = jnp.concatenate([b_blk(bb_re), b_blk(bb_im)], axis=-1).astype(BF16)
    cc = jnp.concatenate([c_blk(c_re.astype(F32)), c_blk(-c_im.astype(F32))], axis=1).astype(BF16)
    ab = jnp.stack([ab_re.reshape(S5_KT, D), ab_im.reshape(S5_KT, D)], axis=1)
    return bb, cc, ab


def _s5_state_to_slab(s_re, s_im):
    re = s_re.astype(F32).reshape(NB, S5_KT, D).transpose(1, 0, 2)
    im = s_im.astype(F32).reshape(NB, S5_KT, D).transpose(1, 0, 2)
    return jnp.concatenate([re, im], axis=-1)


def _s5_slab_to_state(fin):
    re = fin[..., 0:D].transpose(0, 2, 1, 3).reshape(2 * NB, 64, 64)
    im = fin[..., D:2 * D].transpose(0, 2, 1, 3).reshape(2 * NB, 64, 64)
    return re, im


def kernel(x_prompt, x_sample, state_lru, state_s5_re, state_s5_im, c, c_ctx, ada_w, ada_b, norm_mix, norm_ffn, norm_final, lru_w_in, lru_conv_w, lru_conv_b, lru_w_a, lru_b_a, lru_w_i, lru_b_i, lru_lambda, lru_w_out, s5_a_re, s5_a_im, s5_log_dt, s5_b_re, s5_b_im, s5_c_re, s5_c_im, s5_d, s5_w_glu, ffn_w_up, ffn_conv_w, ffn_conv_b, ffn_w_down):
    c_all = jnp.concatenate([c.astype(F32), c_ctx.astype(F32)[None], jnp.zeros((7, D), F32)], axis=0)
    mods = _mod_vectors(c_all, ada_w.astype(F32), ada_b.astype(F32)[:, None, :])

    def mod_of(l):
        return jnp.stack([jnp.broadcast_to(mods[l, 8][None], (NB, N_MOD * D)), mods[l, 0:NB]], axis=0)

    xp4 = x_prompt.astype(F32).reshape(2, NB, P_LEN, D)
    gf = norm_final[None].astype(F32)

    mod0 = mod_of(0)
    zeros_h = jnp.zeros((NB, D), F32)
    sp = jax.nn.softplus(-lru_lambda[0].astype(F32)) * (-LRU_C * LOG2_E * 0.5)
    ba = 0.5 * lru_b_a[0].astype(F32)
    bi = 0.5 * lru_b_i[0].astype(F32)
    h0_f = jnp.stack([zeros_h, zeros_h, state_lru[:, 0, 0].astype(F32)], axis=0)
    h0_b = jnp.stack([zeros_h, zeros_h, state_lru[:, 0, 1].astype(F32)], axis=0)
    xs = x_sample.astype(F32)
    gate, xr, hsb, fin_b = _lru_a(xp4, xs, mod0, norm_mix[0][None].astype(F32),
                                  lru_w_in[0].astype(BF16), lru_conv_w[0].astype(F32),
                                  lru_conv_b[0][None].astype(F32),
                                  _pair_blockdiag(lru_w_a[0, 1], lru_w_i[0, 1]), ba[1][None], bi[1][None],
                                  sp[1][None], h0_b)
    x, fin_f = _lru_fwd(xr, gate, hsb, xp4, xs, mod0, _pair_blockdiag(lru_w_a[0, 0], lru_w_i[0, 0]),
                        ba[0][None], bi[0][None], sp[0][None], h0_f, lru_w_out[0].astype(BF16))
    ffn_args = (norm_ffn.astype(F32)[:, None, :], ffn_w_up.astype(BF16), ffn_conv_w.astype(F32),
                ffn_conv_b.astype(F32)[:, None, :], ffn_w_down.astype(BF16), gf)
    x = _ffn(x, mod0, 0, *ffn_args, False)
    new_lru = jnp.stack([fin_f[0:2].reshape(2 * NB, D), fin_b[0:2].reshape(2 * NB, D)], axis=1)[:, None]

    mod1 = mod_of(1)
    g1 = norm_mix[1][None].astype(F32)
    zeros_s = jnp.zeros((S5_KT, NB, SLAB), F32)
    dirs = []
    for d in range(2):
        bb, cc, ab = _s5_dir_params(s5_a_re[0, d], s5_a_im[0, d], s5_log_dt[0, d], s5_b_re[0, d], s5_b_im[0, d],
                                    s5_c_re[0, d], s5_c_im[0, d])
        h0 = jnp.stack([zeros_s, zeros_s, _s5_state_to_slab(state_s5_re[:, 0, d], state_s5_im[:, 0, d])], axis=0)
        dirs.append((bb, cc, ab, h0))
    yb, fin_sb = _s5_bwd(x, mod1, g1, *dirs[1])
    x, fin_sf = _s5_fwd(x, yb, mod1, g1, *dirs[0], s5_d[0][None].astype(F32), s5_w_glu[0].astype(BF16))
    y_p4, y_sample = _ffn(x, mod1, 1, *ffn_args, True)
    f_re, f_im = _s5_slab_to_state(fin_sf[0:2])
    b_re, b_im = _s5_slab_to_state(fin_sb[0:2])
    new_s5_re = jnp.stack([f_re, b_re], axis=1)[:, None]
    new_s5_im = jnp.stack([f_im, b_im], axis=1)[:, None]

    return (y_p4.reshape(2 * NB, P_LEN, D), y_sample, new_lru, new_s5_re, new_s5_im)
```

```python
import functools

import jax
import jax.numpy as jnp
from jax import lax
from jax.experimental import pallas as pl
from jax.experimental.pallas import tpu as pltpu

F32 = jnp.float32
BF16 = jnp.bfloat16

D = 1024
D_FF = 2816
N_MOD = 6
EPS = 1e-6
LRU_C = 8.0
LOG2_E = 1.4426950408889634
NB = 8
STEPS = 64
TM = STEPS * NB
HALO = 16
NORM_ROWS = 64
FF_CHUNK = 256
FFN_SUB = 2
COL_CHUNK = 256
P_LEN = 256
S_LEN = 4096
P_TILES_PER_SEQ = P_LEN // STEPS
N_P_TILES = 2 * P_TILES_PER_SEQ
N_S_TILES = S_LEN // STEPS
NT = N_P_TILES + N_S_TILES
N_ROWS = NT * TM
N_SEQ = 3
S5_KT = 4
SLAB = 2048
S5_BN = 256
S5_CK_BWD = 1024
S5_CK_FWD = 256
VMEM_LIMIT = 56 * 1024 * 1024


def _seq_id(j):
    return jnp.where(j >= N_P_TILES, 2, jnp.where(j >= P_TILES_PER_SEQ, 1, 0))


def _is_seq_first(j):
    return (j == 0) | (j == P_TILES_PER_SEQ) | (j == N_P_TILES)


def _is_seq_last(j):
    return (j == P_TILES_PER_SEQ - 1) | (j == N_P_TILES - 1) | (j == NT - 1)


def _params(sem):
    return pltpu.CompilerParams(dimension_semantics=sem, vmem_limit_bytes=VMEM_LIMIT)


def _const_spec(shape):
    n = len(shape)
    return pl.BlockSpec(shape, lambda *_: (0,) * n, pipeline_mode=pl.Buffered(1))


def _mod_spec(tile_of):
    return pl.BlockSpec((None, NB, N_MOD * D),
                        lambda i: (jnp.where(tile_of(i) >= N_P_TILES, 1, 0), 0, 0))


def _rms(xv, g):
    ms = jnp.mean(xv * xv, axis=-1, keepdims=True)
    return xv * lax.rsqrt(ms + EPS) * g


def _per_batch(y, vec):
    r, n = y.shape
    return (y.reshape(r // NB, NB, n) * vec[None]).reshape(r, n)


def _norm_mod(xv, g, scale1, shift):
    r = xv.shape[0]
    y = _rms(xv, g).reshape(r // NB, NB, D) * scale1[None] + shift[None]
    return y.reshape(r, D)


def _fill_norm(dst, dst_off, x_ref, g, scale1, shift, f32_dst=None):
    for k in range(TM // NORM_ROWS):
        r0 = k * NORM_ROWS
        y = _norm_mod(x_ref[r0:r0 + NORM_ROWS], g, scale1, shift)
        dst[dst_off + r0:dst_off + r0 + NORM_ROWS] = y.astype(BF16)
        if f32_dst is not None:
            f32_dst[r0:r0 + NORM_ROWS] = y


def _fill_hbuf(hbuf, x_ref, x_prev, x_next, g, scale1, shift):
    hbuf[0:HALO] = _norm_mod(x_prev, g, scale1, shift).astype(BF16)
    hbuf[HALO + TM:HALO + TM + HALO] = _norm_mod(x_next, g, scale1, shift).astype(BF16)
    _fill_norm(hbuf, HALO, x_ref, g, scale1, shift)


def _mod_kernel(c_ref, w_ref, b_ref, o_ref):
    cv = c_ref[...]
    s = (cv * jax.nn.sigmoid(cv)).astype(BF16)
    o_ref[...] = jnp.dot(s, w_ref[...].astype(BF16), preferred_element_type=F32) + b_ref[...]


def _mod_vectors(c_all, ada_w, ada_b):
    depth = ada_w.shape[0]
    return pl.pallas_call(
        _mod_kernel,
        out_shape=jax.ShapeDtypeStruct((depth, 16, N_MOD * D), F32),
        grid=(depth, N_MOD),
        in_specs=[
            pl.BlockSpec((16, D), lambda l, n: (0, 0)),
            pl.BlockSpec((None, D, D), lambda l, n: (l, 0, n)),
            pl.BlockSpec((None, 1, D), lambda l, n: (l, 0, n)),
        ],
        out_specs=pl.BlockSpec((None, 16, D), lambda l, n: (l, 0, n)),
        compiler_params=_params(("arbitrary", "arbitrary")),
        name="mod_vectors",
    )(c_all, ada_w, ada_b)


def _bt_specs(steps, offset, tile_steps=STEPS, tile_of=lambda i: i):
    per_tile = tile_steps // steps
    tiles_per_seq = P_LEN // tile_steps
    n_p = 2 * tiles_per_seq

    def prompt_idx(i):
        ip = jnp.minimum(tile_of(i), n_p - 1)
        blk = jnp.clip((ip % tiles_per_seq) * per_tile + offset, 0, P_LEN // steps - 1)
        return (ip // tiles_per_seq, blk, 0)

    def sample_idx(i):
        blk = jnp.clip((tile_of(i) - n_p) * per_tile + offset, 0, S_LEN // steps - 1)
        return (0, blk, 0)

    return (pl.BlockSpec((NB, steps, D), prompt_idx), pl.BlockSpec((NB, steps, D), sample_idx))


def _to_rows(x_bt):
    n = x_bt.shape[1]
    return jnp.swapaxes(x_bt, 0, 1).reshape(n * NB, D)


def _lru_gate_math(z, ba, bi, sp, xr):
    t_r = jnp.tanh(z[:, 0:128] + ba)
    t_i = jnp.tanh(z[:, 128:256] + bi)
    a = jnp.exp2(sp + sp * t_r)
    s = 1.0 - a * a
    root = jnp.where(s > 0.0, s * lax.rsqrt(s), 0.0)
    return a, root * ((0.5 + 0.5 * t_i) * xr)


def _lru_scan(a_s, bx_s, h_s, out_ref, reverse):
    h = h_s[...]
    for s in range(STEPS):
        t = (STEPS - 1 - s) if reverse else s
        rows = slice(8 * t, 8 * (t + 1))
        h = a_s[rows] * h + bx_s[rows]
        out_ref[rows] = h
    h_s[...] = h


def _lru_a_kernel(xa_ref, xap_ref, xan_ref, xb_ref, xbp_ref, xbn_ref, mod_ref, g_ref, wi_ref, cw_ref, cb_ref,
                  wp_ref, ba_ref, bi_ref, sp_ref, h0_ref,
                  gate_ref, xr_ref, hs_ref, fin_ref, hbuf, xc_s, a_s, bx_s, h_s):
    j = NT - 1 - pl.program_id(0)

    @pl.when(_is_seq_last(j))
    def _():
        h_s[...] = h0_ref[...]

    shift = mod_ref[:, 0:D]
    scale1 = 1.0 + mod_ref[:, D:2 * D]

    is_p = j < N_P_TILES
    xc_s[...] = _to_rows(jnp.where(is_p, xa_ref[...], xb_ref[...]))
    x_prev = _to_rows(jnp.where(is_p, xap_ref[...], xbp_ref[...]))[NB * NB - HALO:NB * NB]
    x_next = _to_rows(jnp.where(is_p, xan_ref[...], xbn_ref[...]))[0:HALO]
    _fill_hbuf(hbuf, xc_s, x_prev, x_next, g_ref[...], scale1, shift)

    first = _is_seq_first(j)
    last = _is_seq_last(j)

    for c in range(D // COL_CHUNK):
        cs = slice(COL_CHUNK * c, COL_CHUNK * (c + 1))
        cx = slice(D + COL_CHUNK * c, D + COL_CHUNK * (c + 1))
        gate_ref[:, cs] = jax.nn.gelu(
            jnp.dot(hbuf[HALO:HALO + TM], wi_ref[:, cs], preferred_element_type=F32)).astype(BF16)
        u = jnp.dot(hbuf[...], wi_ref[:, cx], preferred_element_type=F32)
        head = jnp.where(first, 0.0, u[0:16])
        tail = jnp.where(last, 0.0, u[TM + 16:TM + 24])
        m2 = jnp.concatenate([head, u[16:TM]], axis=0)
        m1 = jnp.concatenate([head[8:16], u[16:TM + 8]], axis=0)
        p1 = jnp.concatenate([u[24:TM + 16], tail], axis=0)
        xr = (cb_ref[:, cs] + m2 * cw_ref[0:1, cs] + m1 * cw_ref[1:2, cs]
              + u[16:TM + 16] * cw_ref[2:3, cs] + p1 * cw_ref[3:4, cs])
        xr16 = xr.astype(BF16)
        xr_ref[:, cs] = xr16
        for h in range(COL_CHUNK // 128):
            p = (COL_CHUNK // 128) * c + h
            cols = slice(128 * p, 128 * (p + 1))
            z = jnp.dot(xr16[:, 128 * h:128 * (h + 1)], wp_ref[p], preferred_element_type=F32)
            a, bx = _lru_gate_math(z, ba_ref[:, cols], bi_ref[:, cols], sp_ref[:, cols],
                                   xr[:, 128 * h:128 * (h + 1)])
            a_s[:, cols] = a
            bx_s[:, cols] = bx

    _lru_scan(a_s, bx_s, h_s, bx_s, True)
    hs_ref[...] = bx_s[...].astype(BF16)
    fin_ref[...] = h_s[...]


def _lru_a(xp, xs, mod, g, wi, cw, cb, wp, ba, bi, sp, h0):
    rev = lambda i: NT - 1 - i
    p_tile, s_tile = _bt_specs(STEPS, 0, tile_of=rev)
    p_prev, s_prev = _bt_specs(NB, -1, tile_of=rev)
    p_next, s_next = _bt_specs(NB, STEPS // NB, tile_of=rev)
    tile = pl.BlockSpec((TM, D), lambda i: (rev(i), 0))
    seq = pl.BlockSpec((None, NB, D), lambda i: (_seq_id(rev(i)), 0, 0))
    rows16 = jax.ShapeDtypeStruct((N_ROWS, D), BF16)
    return pl.pallas_call(
        _lru_a_kernel,
        out_shape=(rows16, rows16, rows16, jax.ShapeDtypeStruct((N_SEQ, NB, D), F32)),
        grid=(NT,),
        in_specs=[p_tile, p_prev, p_next, s_tile, s_prev, s_next,
                  _mod_spec(rev), _const_spec((1, D)),
                  _const_spec((D, 2 * D)), _const_spec((4, D)), _const_spec((1, D)),
                  _const_spec((8, 128, 256)), _const_spec((1, D)), _const_spec((1, D)), _const_spec((1, D)), seq],
        out_specs=(tile, tile, tile, seq),
        scratch_shapes=[pltpu.VMEM((TM + 2 * HALO, D), BF16), pltpu.VMEM((TM, D), F32), pltpu.VMEM((TM, D), F32),
                        pltpu.VMEM((TM, D), F32), pltpu.VMEM((NB, D), F32)],
        compiler_params=_params(("arbitrary",)),
        name="lru_a",
    )(xp, xp, xp, xs, xs, xs, mod, g, wi, cw, cb, wp, ba, bi, sp, h0)


def _lru_fwd_kernel(xr_ref, gate_ref, hsb_ref, xa_ref, xb_ref, mod_ref, wp_ref, ba_ref, bi_ref, sp_ref, h0_ref,
                    wo_ref, x1_ref, fin_ref, a_s, bx_s, y_s, h_s):
    j = pl.program_id(0)

    @pl.when(_is_seq_first(j))
    def _():
        h_s[...] = h0_ref[...]

    for p in range(8):
        cols = slice(128 * p, 128 * (p + 1))
        z = jnp.dot(xr_ref[:, cols], wp_ref[p], preferred_element_type=F32)
        a, bx = _lru_gate_math(z, ba_ref[:, cols], bi_ref[:, cols], sp_ref[:, cols],
                               xr_ref[:, cols].astype(F32))
        a_s[:, cols] = a
        bx_s[:, cols] = bx
    _lru_scan(a_s, bx_s, h_s, bx_s, False)
    fin_ref[...] = h_s[...]

    y_s[...] = ((bx_s[...] + hsb_ref[...].astype(F32)) * gate_ref[...].astype(F32)).astype(BF16)
    x_rows = _to_rows(jnp.where(j < N_P_TILES, xa_ref[...], xb_ref[...]))
    for c in range(D // COL_CHUNK):
        cs = slice(COL_CHUNK * c, COL_CHUNK * (c + 1))
        out = jnp.dot(y_s[...], wo_ref[:, cs], preferred_element_type=F32)
        x1_ref[:, cs] = x_rows[:, cs] + _per_batch(out, mod_ref[:, 2 * D + COL_CHUNK * c:2 * D + COL_CHUNK * (c + 1)])


def _lru_fwd(xr, gate, hsb, xp, xs, mod, wp, ba, bi, sp, h0, wo):
    tile = pl.BlockSpec((TM, D), lambda i: (i, 0))
    p_tile, s_tile = _bt_specs(STEPS, 0)
    seq = pl.BlockSpec((None, NB, D), lambda i: (_seq_id(i), 0, 0))
    return pl.pallas_call(
        _lru_fwd_kernel,
        out_shape=(jax.ShapeDtypeStruct((N_ROWS, D), F32), jax.ShapeDtypeStruct((N_SEQ, NB, D), F32)),
        grid=(NT,),
        in_specs=[tile, tile, tile, p_tile, s_tile, _mod_spec(lambda i: i), _const_spec((8, 128, 256)),
                  _const_spec((1, D)), _const_spec((1, D)), _const_spec((1, D)), seq, _const_spec((D, D))],
        out_specs=(tile, seq),
        scratch_shapes=[pltpu.VMEM((TM, D), F32), pltpu.VMEM((TM, D), F32), pltpu.VMEM((TM, D), BF16),
                        pltpu.VMEM((NB, D), F32)],
        compiler_params=_params(("arbitrary",)),
        name="lru_fwd",
    )(xr, gate, hsb, xp, xs, mod, wp, ba, bi, sp, h0, wo)


def _ffn_kernel(x_ref, xp_ref, xn_ref, mod_ref, g_ref, wu_ref, cw_ref, cb_ref, wd_ref, gf_ref,
                *rest, final_norm):
    if final_norm:
        op_ref, os_ref, hbuf, act = rest
    else:
        o_ref, hbuf, act = rest
    k = pl.program_id(0)
    g = g_ref[...]
    shift = mod_ref[:, 3 * D:4 * D]
    scale1 = 1.0 + mod_ref[:, 4 * D:5 * D]
    gt2 = mod_ref[:, 5 * D:6 * D]
    rows_all = FFN_SUB * TM

    def fill_piece(p):
        n_blocks = rows_all // NORM_ROWS
        if p == 0:
            hbuf[0:HALO] = _norm_mod(xp_ref[...], g, scale1, shift).astype(BF16)
        elif p <= n_blocks:
            r0 = (p - 1) * NORM_ROWS
            hbuf[HALO + r0:HALO + r0 + NORM_ROWS] = _norm_mod(
                x_ref[r0:r0 + NORM_ROWS], g, scale1, shift).astype(BF16)
        elif p == n_blocks + 1:
            hbuf[HALO + rows_all:HALO + rows_all + HALO] = _norm_mod(xn_ref[...], g, scale1, shift).astype(BF16)

    n_pieces = rows_all // NORM_ROWS + 2
    first_pieces = TM // NORM_ROWS + 2
    for p in range(first_pieces):
        fill_piece(p)
    next_piece = first_pieces
    finals = []

    for s in range(FFN_SUB):
        tile = FFN_SUB * k + s
        is_prompt = tile < N_P_TILES
        seg_first = jnp.logical_not(is_prompt & ((tile % P_TILES_PER_SEQ) != 0))
        seg_last = jnp.logical_not(is_prompt & ((tile % P_TILES_PER_SEQ) != P_TILES_PER_SEQ - 1))
        h_rows = slice(TM * s, TM * s + TM + 2 * HALO)

        def conv3(cs):
            u = jnp.dot(hbuf[h_rows], wu_ref[:, cs], preferred_element_type=F32)
            p0 = jnp.where(seg_first, 0.0, u[8:16])
            n0 = jnp.where(seg_last, 0.0, u[TM + 16:TM + 24])
            prev = jnp.concatenate([p0, u[16:TM + 8]], axis=0)
            nxt = jnp.concatenate([u[24:TM + 16], n0], axis=0)
            return (cb_ref[:, cs] + prev * cw_ref[0:1, cs] + u[16:TM + 16] * cw_ref[1:2, cs]
                    + nxt * cw_ref[2:3, cs])

        for c in range(D_FF // FF_CHUNK):
            v = conv3(slice(FF_CHUNK * c, FF_CHUNK * (c + 1)))
            gg = conv3(slice(D_FF + FF_CHUNK * c, D_FF + FF_CHUNK * (c + 1)))
            act[s, :, FF_CHUNK * c:FF_CHUNK * (c + 1)] = (v * (gg * jax.nn.sigmoid(gg))).astype(BF16)
            if next_piece < n_pieces:
                fill_piece(next_piece)
                next_piece += 1

        out = jnp.dot(act[s], wd_ref[...], preferred_element_type=F32)
        y = x_ref[TM * s:TM * (s + 1)] + _per_batch(out, gt2)
        if final_norm:
            finals.append(jnp.swapaxes(_rms(y, gf_ref[...]).reshape(STEPS, NB, D), 0, 1))
        else:
            o_ref[TM * s:TM * (s + 1)] = y

    if final_norm:
        y_bt = jnp.concatenate(finals, axis=1)

        @pl.when(k < N_P_TILES // FFN_SUB)
        def _():
            op_ref[...] = y_bt

        @pl.when(k >= N_P_TILES // FFN_SUB)
        def _():
            os_ref[...] = y_bt


def _ffn(x, mod, layer, g, wu, cw, cb, wd, gf, final_norm):
    rows = FFN_SUB * TM
    n_steps = NT // FFN_SUB
    blk = pl.BlockSpec((rows, D), lambda i: (i, 0))
    prev = pl.BlockSpec((HALO, D), lambda i: (jnp.maximum(i * (rows // HALO) - 1, 0), 0))
    nxt = pl.BlockSpec((HALO, D), lambda i: (jnp.minimum((i + 1) * (rows // HALO), N_ROWS // HALO - 1), 0))
    mod_spec = pl.BlockSpec((None, NB, N_MOD * D), lambda i: (jnp.where(i >= N_P_TILES // FFN_SUB, 1, 0), 0, 0))

    def layer_spec(shape):
        n = len(shape)
        return pl.BlockSpec((None,) + shape, lambda *_: (layer,) + (0,) * n, pipeline_mode=pl.Buffered(1))

    if final_norm:
        out_shape = (jax.ShapeDtypeStruct((2 * NB, P_LEN, D), F32), jax.ShapeDtypeStruct((NB, S_LEN, D), F32))
        out_specs = _bt_specs(FFN_SUB * STEPS, 0, FFN_SUB * STEPS)
    else:
        out_shape = jax.ShapeDtypeStruct((N_ROWS, D), F32)
        out_specs = blk
    return pl.pallas_call(
        functools.partial(_ffn_kernel, final_norm=final_norm),
        out_shape=out_shape,
        grid=(n_steps,),
        in_specs=[blk, prev, nxt, mod_spec, layer_spec((1, D)),
                  layer_spec((D, 2 * D_FF)), layer_spec((3, 2 * D_FF)), layer_spec((1, 2 * D_FF)),
                  layer_spec((D_FF, D)), _const_spec((1, D))],
        out_specs=out_specs,
        scratch_shapes=[pltpu.VMEM((rows + 2 * HALO, D), BF16), pltpu.VMEM((FFN_SUB, TM, D_FF), BF16)],
        compiler_params=_params(("arbitrary",)),
        name="conv_ffn",
    )(x, x, x, mod, g, wu, cw, cb, wd, gf)


def _s5_core(ubuf, bb_ref, cc_ref, ab_ref, slabs, hs16, hst, y_write, reverse, c_cols):
    def b_piece(kt, n):
        cs = slice(S5_BN * n, S5_BN * (n + 1))
        slabs[kt % 2, :, cs] = jnp.dot(ubuf[:, 256 * kt:256 * (kt + 1)], bb_ref[kt, :, cs],
                                       preferred_element_type=F32)

    def c_piece(kt, q):
        cs = slice(c_cols * q, c_cols * (q + 1))
        return jnp.dot(hs16[kt % 2, :, cs], cc_ref[kt, cs, :], preferred_element_type=F32)

    n_b = SLAB // S5_BN
    every = STEPS // n_b
    n_c = SLAB // c_cols
    c_every = STEPS // n_c
    for n in range(n_b):
        b_piece(0, n)
    for kt in range(S5_KT):
        slab = slabs.at[kt % 2]
        hs = hs16.at[kt % 2]
        a_re = jnp.broadcast_to(ab_ref[kt, 0:1, :], (NB, D))
        a_im = jnp.broadcast_to(ab_ref[kt, 1:2, :], (NB, D))
        h_re = hst[kt, :, 0:D]
        h_im = hst[kt, :, D:2 * D]
        part = None
        for s in range(STEPS):
            if s % every == 0 and kt + 1 < S5_KT:
                b_piece(kt + 1, s // every)
            if s % c_every == c_every // 2 and kt >= 1:
                q = s // c_every
                part = c_piece(kt - 1, q) if q == 0 else part + c_piece(kt - 1, q)
                if q == n_c - 1:
                    y_write(kt - 1, part)
            t = (STEPS - 1 - s) if reverse else s
            rows = slice(8 * t, 8 * (t + 1))
            n_re = a_re * h_re - a_im * h_im + slab[rows, 0:D]
            n_im = a_re * h_im + a_im * h_re + slab[rows, D:2 * D]
            if s % 2 == 1:
                lo = min(t, t + 1 if reverse else t - 1)
                pair = slice(8 * lo, 8 * lo + 16)
                first, second = ((n_re, h_re), (n_im, h_im)) if reverse else ((h_re, n_re), (h_im, n_im))
                hs[pair, 0:D] = jnp.concatenate(first, axis=0).astype(BF16)
                hs[pair, D:2 * D] = jnp.concatenate(second, axis=0).astype(BF16)
            h_re, h_im = n_re, n_im
        hst[kt, :, 0:D] = h_re
        hst[kt, :, D:2 * D] = h_im
    part = c_piece(S5_KT - 1, 0)
    for q in range(1, n_c):
        part = part + c_piece(S5_KT - 1, q)
    y_write(S5_KT - 1, part)


def _s5_bwd_kernel(x_ref, mod_ref, g_ref, bb_ref, cc_ref, ab_ref, h0_ref,
                   yb_ref, fin_ref, ubuf, slabs, hs16, hst):
    j = NT - 1 - pl.program_id(0)

    @pl.when(_is_seq_last(j))
    def _():
        hst[...] = h0_ref[...]

    _fill_norm(ubuf, 0, x_ref, g_ref[...], 1.0 + mod_ref[:, D:2 * D], mod_ref[:, 0:D])

    def y_write(kt, val):
        yb_ref[:, 256 * kt:256 * (kt + 1)] = val

    _s5_core(ubuf, bb_ref, cc_ref, ab_ref, slabs, hs16, hst, y_write, True, S5_CK_BWD)
    fin_ref[...] = hst[...]


def _s5_bwd(x, mod, g, bb, cc, ab, h0):
    tile = pl.BlockSpec((TM, D), lambda i: (NT - 1 - i, 0))
    seq = pl.BlockSpec((None, S5_KT, NB, SLAB), lambda i: (_seq_id(NT - 1 - i), 0, 0, 0))
    return pl.pallas_call(
        _s5_bwd_kernel,
        out_shape=(jax.ShapeDtypeStruct((N_ROWS, D), F32),
                   jax.ShapeDtypeStruct((N_SEQ, S5_KT, NB, SLAB), F32)),
        grid=(NT,),
        in_specs=[tile, _mod_spec(lambda i: NT - 1 - i), _const_spec((1, D)),
                  _const_spec((S5_KT, 256, SLAB)), _const_spec((S5_KT, SLAB, 256)),
                  _const_spec((S5_KT, 2, D)), seq],
        out_specs=(tile, seq),
        scratch_shapes=[pltpu.VMEM((TM, D), BF16), pltpu.VMEM((2, TM, SLAB), F32), pltpu.VMEM((2, TM, SLAB), BF16),
                        pltpu.VMEM((S5_KT, NB, SLAB), F32)],
        compiler_params=_params(("arbitrary",)),
        name="s5_bwd",
    )(x, mod, g, bb, cc, ab, h0)


def _s5_fwd_kernel(x_ref, yb_ref, mod_ref, g_ref, bb_ref, cc_ref, ab_ref, h0_ref, dsk_ref, wglu_ref,
                   x1_ref, fin_ref, ubuf, zbuf, u_s, slabs, hs16, hst):
    j = pl.program_id(0)

    @pl.when(_is_seq_first(j))
    def _():
        hst[...] = h0_ref[...]

    _fill_norm(ubuf, 0, x_ref, g_ref[...], 1.0 + mod_ref[:, D:2 * D], mod_ref[:, 0:D], u_s)

    def y_write(kt, val):
        cs = slice(256 * kt, 256 * (kt + 1))
        y = dsk_ref[:, cs] * u_s[:, cs] + yb_ref[:, cs] + val
        zbuf[:, cs] = jax.nn.gelu(y).astype(BF16)

    _s5_core(ubuf, bb_ref, cc_ref, ab_ref, slabs, hs16, hst, y_write, False, S5_CK_FWD)
    fin_ref[...] = hst[...]

    for c in range(D // COL_CHUNK):
        cs = slice(COL_CHUNK * c, COL_CHUNK * (c + 1))
        v = jnp.dot(zbuf[...], wglu_ref[:, cs], preferred_element_type=F32)
        gg = jnp.dot(zbuf[...], wglu_ref[:, D + COL_CHUNK * c:D + COL_CHUNK * (c + 1)], preferred_element_type=F32)
        out = v * jax.nn.sigmoid(gg)
        x1_ref[:, cs] = x_ref[:, cs] + _per_batch(out, mod_ref[:, 2 * D + COL_CHUNK * c:2 * D + COL_CHUNK * (c + 1)])


def _s5_fwd(x, yb, mod, g, bb, cc, ab, h0, dsk, wglu):
    tile = pl.BlockSpec((TM, D), lambda i: (i, 0))
    seq = pl.BlockSpec((None, S5_KT, NB, SLAB), lambda i: (_seq_id(i), 0, 0, 0))
    return pl.pallas_call(
        _s5_fwd_kernel,
        out_shape=(jax.ShapeDtypeStruct((N_ROWS, D), F32),
                   jax.ShapeDtypeStruct((N_SEQ, S5_KT, NB, SLAB), F32)),
        grid=(NT,),
        in_specs=[tile, tile, _mod_spec(lambda i: i), _const_spec((1, D)),
                  _const_spec((S5_KT, 256, SLAB)), _const_spec((S5_KT, SLAB, 256)),
                  _const_spec((S5_KT, 2, D)), seq, _const_spec((1, D)), _const_spec((D, 2 * D))],
        out_specs=(tile, seq),
        scratch_shapes=[pltpu.VMEM((TM, D), BF16), pltpu.VMEM((TM, D), BF16), pltpu.VMEM((TM, D), F32),
                        pltpu.VMEM((2, TM, SLAB), F32), pltpu.VMEM((2, TM, SLAB), BF16),
                        pltpu.VMEM((S5_KT, NB, SLAB), F32)],
        compiler_params=_params(("arbitrary",)),
        name="s5_fwd",
    )(x, yb, mod, g, bb, cc, ab, h0, dsk, wglu)


def _pair_blockdiag(w_a, w_i):
    same = (jnp.arange(2)[:, None, None, None] == jnp.arange(2)[None, None, :, None])

    def bd(w):
        w4 = w.astype(F32).reshape(8, 2, 64, 1, 64)
        return jnp.where(same[None], w4, 0.0).reshape(8, 128, 128)

    return (0.5 * jnp.concatenate([bd(w_a), bd(w_i)], axis=-1)).astype(BF16)


def _s5_dir_params(a_re, a_im, log_dt, b_re, b_im, c_re, c_im):
    l_re = a_re.astype(F32)
    l_im = a_im.astype(F32)
    dt = jnp.exp(log_dt.astype(F32))[:, None]
    mag = jnp.exp(l_re * dt)
    ab_re = mag * jnp.cos(l_im * dt)
    ab_im = mag * jnp.sin(l_im * dt)
    den = l_re * l_re + l_im * l_im
    k_re = ((ab_re - 1.0) * l_re + ab_im * l_im) / den
    k_im = (ab_im * l_re - (ab_re - 1.0) * l_im) / den
    br = b_re.astype(F32)
    bi = b_im.astype(F32)
    bb_re = br * k_re[..., None] - bi * k_im[..., None]
    bb_im = br * k_im[..., None] + bi * k_re[..., None]
    same = (jnp.arange(16)[:, None, None, None] == jnp.arange(16)[None, None, :, None])[None]

    def b_blk(m):
        m5 = m.reshape(S5_KT, 16, 64, 16).transpose(0, 1, 3, 2)[:, :, :, None, :]
        return jnp.where(same, m5, 0.0).reshape(S5_KT, 256, 1024)

    def c_blk(m):
        m5 = m.reshape(S5_KT, 16, 16, 64).transpose(0, 1, 3, 2)[:, :, :, None, :]
        return jnp.where(same, m5, 0.0).reshape(S5_KT, 1024, 256)

    bb = jnp.concatenate([b_blk(bb_re), b_blk(bb_im)], axis=-1).astype(BF16)
    cc = jnp.concatenate([c_blk(c_re.astype(F32)), c_blk(-c_im.astype(F32))], axis=1).astype(BF16)
    ab = jnp.stack([ab_re.reshape(S5_KT, D), ab_im.reshape(S5_KT, D)], axis=1)
    return bb, cc, ab


def _s5_state_to_slab(s_re, s_im):
    re = s_re.astype(F32).reshape(NB, S5_KT, D).transpose(1, 0, 2)
    im = s_im.astype(F32).reshape(NB, S5_KT, D).transpose(1, 0, 2)
    return jnp.concatenate([re, im], axis=-1)


def _s5_slab_to_state(fin):
    re = fin[..., 0:D].transpose(0, 2, 1, 3).reshape(2 * NB, 64, 64)
    im = fin[..., D:2 * D].transpose(0, 2, 1, 3).reshape(2 * NB, 64, 64)
    return re, im


def kernel(x_prompt, x_sample, state_lru, state_s5_re, state_s5_im, c, c_ctx, ada_w, ada_b, norm_mix, norm_ffn, norm_final, lru_w_in, lru_conv_w, lru_conv_b, lru_w_a, lru_b_a, lru_w_i, lru_b_i, lru_lambda, lru_w_out, s5_a_re, s5_a_im, s5_log_dt, s5_b_re, s5_b_im, s5_c_re, s5_c_im, s5_d, s5_w_glu, ffn_w_up, ffn_conv_w, ffn_conv_b, ffn_w_down):
    c_all = jnp.concatenate([c.astype(F32), c_ctx.astype(F32)[None], jnp.zeros((7, D), F32)], axis=0)
    mods = _mod_vectors(c_all, ada_w.astype(F32), ada_b.astype(F32)[:, None, :])

    def mod_of(l):
        return jnp.stack([jnp.broadcast_to(mods[l, 8][None], (NB, N_MOD * D)), mods[l, 0:NB]], axis=0)

    xp = x_prompt.astype(F32)
    gf = norm_final[None].astype(F32)

    mod0 = mod_of(0)
    zeros_h = jnp.zeros((NB, D), F32)
    sp = jax.nn.softplus(-lru_lambda[0].astype(F32)) * (-LRU_C * LOG2_E * 0.5)
    ba = 0.5 * lru_b_a[0].astype(F32)
    bi = 0.5 * lru_b_i[0].astype(F32)
    h0_f = jnp.stack([zeros_h, zeros_h, state_lru[:, 0, 0].astype(F32)], axis=0)
    h0_b = jnp.stack([zeros_h, zeros_h, state_lru[:, 0, 1].astype(F32)], axis=0)
    xs = x_sample.astype(F32)
    gate, xr, hsb, fin_b = _lru_a(xp, xs, mod0, norm_mix[0][None].astype(F32),
                                  lru_w_in[0].astype(BF16), lru_conv_w[0].astype(F32),
                                  lru_conv_b[0][None].astype(F32),
                                  _pair_blockdiag(lru_w_a[0, 1], lru_w_i[0, 1]), ba[1][None], bi[1][None],
                                  sp[1][None], h0_b)
    x, fin_f = _lru_fwd(xr, gate, hsb, xp, xs, mod0, _pair_blockdiag(lru_w_a[0, 0], lru_w_i[0, 0]),
                        ba[0][None], bi[0][None], sp[0][None], h0_f, lru_w_out[0].astype(BF16))
    ffn_args = (norm_ffn.astype(F32)[:, None, :], ffn_w_up.astype(BF16), ffn_conv_w.astype(F32),
                ffn_conv_b.astype(F32)[:, None, :], ffn_w_down.astype(BF16), gf)
    x = _ffn(x, mod0, 0, *ffn_args, False)
    new_lru = jnp.stack([fin_f[0:2].reshape(2 * NB, D), fin_b[0:2].reshape(2 * NB, D)], axis=1)[:, None]

    mod1 = mod_of(1)
    g1 = norm_mix[1][None].astype(F32)
    zeros_s = jnp.zeros((S5_KT, NB, SLAB), F32)
    dirs = []
    for d in range(2):
        bb, cc, ab = _s5_dir_params(s5_a_re[0, d], s5_a_im[0, d], s5_log_dt[0, d], s5_b_re[0, d], s5_b_im[0, d],
                                    s5_c_re[0, d], s5_c_im[0, d])
        h0 = jnp.stack([zeros_s, zeros_s, _s5_state_to_slab(state_s5_re[:, 0, d], state_s5_im[:, 0, d])], axis=0)
        dirs.append((bb, cc, ab, h0))
    yb, fin_sb = _s5_bwd(x, mod1, g1, *dirs[1])
    x, fin_sf = _s5_fwd(x, yb, mod1, g1, *dirs[0], s5_d[0][None].astype(F32), s5_w_glu[0].astype(BF16))
    y_prompt, y_sample = _ffn(x, mod1, 1, *ffn_args, True)
    f_re, f_im = _s5_slab_to_state(fin_sf[0:2])
    b_re, b_im = _s5_slab_to_state(fin_sb[0:2])
    new_s5_re = jnp.stack([f_re, b_re], axis=1)[:, None]
    new_s5_im = jnp.stack([f_im, b_im], axis=1)[:, None]

    return (y_prompt, y_sample, new_lru, new_s5_re, new_s5_im)
```

```python
import functools

import jax
import jax.numpy as jnp
from jax import lax
from jax.experimental import pallas as pl
from jax.experimental.pallas import tpu as pltpu

F32 = jnp.float32
BF16 = jnp.bfloat16

D = 1024
D_FF = 2816
N_MOD = 6
EPS = 1e-6
LRU_C = 8.0
LOG2_E = 1.4426950408889634
NB = 8
STEPS = 64
TM = STEPS * NB
HALO = 16
NORM_ROWS = 64
FF_CHUNK = 256
FFN_SUB = 2
COL_CHUNK = 256
P_LEN = 256
S_LEN = 4096
P_TILES_PER_SEQ = P_LEN // STEPS
N_P_TILES = 2 * P_TILES_PER_SEQ
N_S_TILES = S_LEN // STEPS
NT = N_P_TILES + N_S_TILES
N_ROWS = NT * TM
N_SEQ = 3
S5_KT = 4
SLAB = 2048
S5_BN = 256
S5_CK_BWD = 1024
S5_CK_FWD = 256
VMEM_LIMIT = 56 * 1024 * 1024


def _seq_id(j):
    return jnp.where(j >= N_P_TILES, 2, jnp.where(j >= P_TILES_PER_SEQ, 1, 0))


def _is_seq_first(j):
    return (j == 0) | (j == P_TILES_PER_SEQ) | (j == N_P_TILES)


def _is_seq_last(j):
    return (j == P_TILES_PER_SEQ - 1) | (j == N_P_TILES - 1) | (j == NT - 1)


def _params(sem):
    return pltpu.CompilerParams(dimension_semantics=sem, vmem_limit_bytes=VMEM_LIMIT)


def _const_spec(shape):
    n = len(shape)
    return pl.BlockSpec(shape, lambda *_: (0,) * n, pipeline_mode=pl.Buffered(1))


def _mod_spec(tile_of):
    return pl.BlockSpec((None, NB, N_MOD * D),
                        lambda i: (jnp.where(tile_of(i) >= N_P_TILES, 1, 0), 0, 0))


def _rms(xv, g):
    ms = jnp.mean(xv * xv, axis=-1, keepdims=True)
    return xv * lax.rsqrt(ms + EPS) * g


def _per_batch(y, vec):
    r, n = y.shape
    return (y.reshape(r // NB, NB, n) * vec[None]).reshape(r, n)


def _norm_mod(xv, g, scale1, shift):
    r = xv.shape[0]
    y = _rms(xv, g).reshape(r // NB, NB, D) * scale1[None] + shift[None]
    return y.reshape(r, D)


def _fill_norm(dst, dst_off, x_ref, g, scale1, shift, f32_dst=None):
    for k in range(TM // NORM_ROWS):
        r0 = k * NORM_ROWS
        y = _norm_mod(x_ref[r0:r0 + NORM_ROWS], g, scale1, shift)
        dst[dst_off + r0:dst_off + r0 + NORM_ROWS] = y.astype(BF16)
        if f32_dst is not None:
            f32_dst[r0:r0 + NORM_ROWS] = y


def _fill_hbuf(hbuf, x_ref, x_prev, x_next, g, scale1, shift):
    hbuf[0:HALO] = _norm_mod(x_prev, g, scale1, shift).astype(BF16)
    hbuf[HALO + TM:HALO + TM + HALO] = _norm_mod(x_next, g, scale1, shift).astype(BF16)
    _fill_norm(hbuf, HALO, x_ref, g, scale1, shift)


def _mod_kernel(c_ref, w_ref, b_ref, o_ref):
    cv = c_ref[...]
    s = (cv * jax.nn.sigmoid(cv)).astype(BF16)
    o_ref[...] = jnp.dot(s, w_ref[...].astype(BF16), preferred_element_type=F32) + b_ref[...]


def _mod_vectors(c_all, ada_w, ada_b):
    depth = ada_w.shape[0]
    return pl.pallas_call(
        _mod_kernel,
        out_shape=jax.ShapeDtypeStruct((depth, 16, N_MOD * D), F32),
        grid=(depth, N_MOD),
        in_specs=[
            pl.BlockSpec((16, D), lambda l, n: (0, 0)),
            pl.BlockSpec((None, D, D), lambda l, n: (l, 0, n)),
            pl.BlockSpec((None, 1, D), lambda l, n: (l, 0, n)),
        ],
        out_specs=pl.BlockSpec((None, 16, D), lambda l, n: (l, 0, n)),
        compiler_params=_params(("arbitrary", "arbitrary")),
        name="mod_vectors",
    )(c_all, ada_w, ada_b)


def _bt_specs(steps, offset, tile_steps=STEPS, tile_of=lambda i: i):
    per_tile = tile_steps // steps
    tiles_per_seq = P_LEN // tile_steps
    n_p = 2 * tiles_per_seq

    def prompt_idx(i):
        ip = jnp.minimum(tile_of(i), n_p - 1)
        blk = jnp.clip((ip % tiles_per_seq) * per_tile + offset, 0, P_LEN // steps - 1)
        return (ip // tiles_per_seq, blk, 0)

    def sample_idx(i):
        blk = jnp.clip((tile_of(i) - n_p) * per_tile + offset, 0, S_LEN // steps - 1)
        return (0, blk, 0)

    return (pl.BlockSpec((NB, steps, D), prompt_idx), pl.BlockSpec((NB, steps, D), sample_idx))


def _to_rows(x_bt):
    n = x_bt.shape[1]
    return jnp.swapaxes(x_bt, 0, 1).reshape(n * NB, D)


def _lru_gate_math(z, ba, bi, sp, xr):
    t_r = jnp.tanh(z[:, 0:128] + ba)
    t_i = jnp.tanh(z[:, 128:256] + bi)
    a = jnp.exp2(sp + sp * t_r)
    s = 1.0 - a * a
    root = jnp.where(s > 0.0, s * lax.rsqrt(s), 0.0)
    return a, root * ((0.5 + 0.5 * t_i) * xr)


def _lru_scan(a_s, bx_s, h_s, out_ref, reverse):
    h = h_s[...]
    for s in range(STEPS):
        t = (STEPS - 1 - s) if reverse else s
        rows = slice(8 * t, 8 * (t + 1))
        h = a_s[rows] * h + bx_s[rows]
        out_ref[rows] = h
    h_s[...] = h


def _lru_a_kernel(xa_ref, xap_ref, xan_ref, xb_ref, xbp_ref, xbn_ref, mod_ref, g_ref, wi_ref, cw_ref, cb_ref,
                  wp_ref, ba_ref, bi_ref, sp_ref, h0_ref,
                  gate_ref, xr_ref, hs_ref, fin_ref, hbuf, xc_s, a_s, bx_s, h_s):
    j = NT - 1 - pl.program_id(0)

    @pl.when(_is_seq_last(j))
    def _():
        h_s[...] = h0_ref[...]

    shift = mod_ref[:, 0:D]
    scale1 = 1.0 + mod_ref[:, D:2 * D]

    is_p = j < N_P_TILES
    xc_s[...] = _to_rows(jnp.where(is_p, xa_ref[...], xb_ref[...]))
    x_prev = _to_rows(jnp.where(is_p, xap_ref[...], xbp_ref[...]))[NB * NB - HALO:NB * NB]
    x_next = _to_rows(jnp.where(is_p, xan_ref[...], xbn_ref[...]))[0:HALO]
    _fill_hbuf(hbuf, xc_s, x_prev, x_next, g_ref[...], scale1, shift)

    first = _is_seq_first(j)
    last = _is_seq_last(j)

    for c in range(D // COL_CHUNK):
        cs = slice(COL_CHUNK * c, COL_CHUNK * (c + 1))
        cx = slice(D + COL_CHUNK * c, D + COL_CHUNK * (c + 1))
        gate_ref[:, cs] = jax.nn.gelu(
            jnp.dot(hbuf[HALO:HALO + TM], wi_ref[:, cs], preferred_element_type=F32)).astype(BF16)
        u = jnp.dot(hbuf[...], wi_ref[:, cx], preferred_element_type=F32)
        head = jnp.where(first, 0.0, u[0:16])
        tail = jnp.where(last, 0.0, u[TM + 16:TM + 24])
        m2 = jnp.concatenate([head, u[16:TM]], axis=0)
        m1 = jnp.concatenate([head[8:16], u[16:TM + 8]], axis=0)
        p1 = jnp.concatenate([u[24:TM + 16], tail], axis=0)
        xr = (cb_ref[:, cs] + m2 * cw_ref[0:1, cs] + m1 * cw_ref[1:2, cs]
              + u[16:TM + 16] * cw_ref[2:3, cs] + p1 * cw_ref[3:4, cs])
        xr16 = xr.astype(BF16)
        xr_ref[:, cs] = xr16
        for h in range(COL_CHUNK // 128):
            p = (COL_CHUNK // 128) * c + h
            cols = slice(128 * p, 128 * (p + 1))
            z = jnp.dot(xr16[:, 128 * h:128 * (h + 1)], wp_ref[p], preferred_element_type=F32)
            a, bx = _lru_gate_math(z, ba_ref[:, cols], bi_ref[:, cols], sp_ref[:, cols],
                                   xr[:, 128 * h:128 * (h + 1)])
            a_s[:, cols] = a
            bx_s[:, cols] = bx

    _lru_scan(a_s, bx_s, h_s, bx_s, True)
    hs_ref[...] = bx_s[...].astype(BF16)
    fin_ref[...] = h_s[...]


def _lru_a(xp, xs, mod, g, wi, cw, cb, wp, ba, bi, sp, h0):
    rev = lambda i: NT - 1 - i
    p_tile, s_tile = _bt_specs(STEPS, 0, tile_of=rev)
    p_prev, s_prev = _bt_specs(NB, -1, tile_of=rev)
    p_next, s_next = _bt_specs(NB, STEPS // NB, tile_of=rev)
    tile = pl.BlockSpec((TM, D), lambda i: (rev(i), 0))
    seq = pl.BlockSpec((None, NB, D), lambda i: (_seq_id(rev(i)), 0, 0))
    rows16 = jax.ShapeDtypeStruct((N_ROWS, D), BF16)
    return pl.pallas_call(
        _lru_a_kernel,
        out_shape=(rows16, rows16, rows16, jax.ShapeDtypeStruct((N_SEQ, NB, D), F32)),
        grid=(NT,),
        in_specs=[p_tile, p_prev, p_next, s_tile, s_prev, s_next,
                  _mod_spec(rev), _const_spec((1, D)),
                  _const_spec((D, 2 * D)), _const_spec((4, D)), _const_spec((1, D)),
                  _const_spec((8, 128, 256)), _const_spec((1, D)), _const_spec((1, D)), _const_spec((1, D)), seq],
        out_specs=(tile, tile, tile, seq),
        scratch_shapes=[pltpu.VMEM((TM + 2 * HALO, D), BF16), pltpu.VMEM((TM, D), F32), pltpu.VMEM((TM, D), F32),
                        pltpu.VMEM((TM, D), F32), pltpu.VMEM((NB, D), F32)],
        compiler_params=_params(("arbitrary",)),
        name="lru_a",
    )(xp, xp, xp, xs, xs, xs, mod, g, wi, cw, cb, wp, ba, bi, sp, h0)


def _lru_fwd_kernel(xr_ref, gate_ref, hsb_ref, xa_ref, xb_ref, mod_ref, wp_ref, ba_ref, bi_ref, sp_ref, h0_ref,
                    wo_ref, x1_ref, fin_ref, a_s, bx_s, y_s, h_s):
    j = pl.program_id(0)

    @pl.when(_is_seq_first(j))
    def _():
        h_s[...] = h0_ref[...]

    for p in range(8):
        cols = slice(128 * p, 128 * (p + 1))
        z = jnp.dot(xr_ref[:, cols], wp_ref[p], preferred_element_type=F32)
        a, bx = _lru_gate_math(z, ba_ref[:, cols], bi_ref[:, cols], sp_ref[:, cols],
                               xr_ref[:, cols].astype(F32))
        a_s[:, cols] = a
        bx_s[:, cols] = bx
    _lru_scan(a_s, bx_s, h_s, bx_s, False)
    fin_ref[...] = h_s[...]

    y_s[...] = ((bx_s[...] + hsb_ref[...].astype(F32)) * gate_ref[...].astype(F32)).astype(BF16)
    x_rows = _to_rows(jnp.where(j < N_P_TILES, xa_ref[...], xb_ref[...]))
    for c in range(D // COL_CHUNK):
        cs = slice(COL_CHUNK * c, COL_CHUNK * (c + 1))
        out = jnp.dot(y_s[...], wo_ref[:, cs], preferred_element_type=F32)
        x1_ref[:, cs] = x_rows[:, cs] + _per_batch(out, mod_ref[:, 2 * D + COL_CHUNK * c:2 * D + COL_CHUNK * (c + 1)])


def _lru_fwd(xr, gate, hsb, xp, xs, mod, wp, ba, bi, sp, h0, wo):
    tile = pl.BlockSpec((TM, D), lambda i: (i, 0))
    p_tile, s_tile = _bt_specs(STEPS, 0)
    seq = pl.BlockSpec((None, NB, D), lambda i: (_seq_id(i), 0, 0))
    return pl.pallas_call(
        _lru_fwd_kernel,
        out_shape=(jax.ShapeDtypeStruct((N_ROWS, D), F32), jax.ShapeDtypeStruct((N_SEQ, NB, D), F32)),
        grid=(NT,),
        in_specs=[tile, tile, tile, p_tile, s_tile, _mod_spec(lambda i: i), _const_spec((8, 128, 256)),
                  _const_spec((1, D)), _const_spec((1, D)), _const_spec((1, D)), seq, _const_spec((D, D))],
        out_specs=(tile, seq),
        scratch_shapes=[pltpu.VMEM((TM, D), F32), pltpu.VMEM((TM, D), F32), pltpu.VMEM((TM, D), BF16),
                        pltpu.VMEM((NB, D), F32)],
        compiler_params=_params(("arbitrary",)),
        name="lru_fwd",
    )(xr, gate, hsb, xp, xs, mod, wp, ba, bi, sp, h0, wo)


def _ffn_kernel(x_ref, xp_ref, xn_ref, mod_ref, g_ref, wu_ref, cw_ref, cb_ref, wd_ref, gf_ref,
                *rest, final_norm):
    if final_norm:
        op_ref, os_ref, hbuf, act = rest
    else:
        o_ref, hbuf, act = rest
    k = pl.program_id(0)
    g = g_ref[...]
    shift = mod_ref[:, 3 * D:4 * D]
    scale1 = 1.0 + mod_ref[:, 4 * D:5 * D]
    gt2 = mod_ref[:, 5 * D:6 * D]
    rows_all = FFN_SUB * TM

    def fill_piece(p):
        n_blocks = rows_all // NORM_ROWS
        if p == 0:
            hbuf[0:HALO] = _norm_mod(xp_ref[...], g, scale1, shift).astype(BF16)
        elif p <= n_blocks:
            r0 = (p - 1) * NORM_ROWS
            hbuf[HALO + r0:HALO + r0 + NORM_ROWS] = _norm_mod(
                x_ref[r0:r0 + NORM_ROWS], g, scale1, shift).astype(BF16)
        elif p == n_blocks + 1:
            hbuf[HALO + rows_all:HALO + rows_all + HALO] = _norm_mod(xn_ref[...], g, scale1, shift).astype(BF16)

    n_pieces = rows_all // NORM_ROWS + 2
    first_pieces = TM // NORM_ROWS + 2
    for p in range(first_pieces):
        fill_piece(p)
    next_piece = first_pieces
    finals = []

    for s in range(FFN_SUB):
        tile = FFN_SUB * k + s
        is_prompt = tile < N_P_TILES
        seg_first = jnp.logical_not(is_prompt & ((tile % P_TILES_PER_SEQ) != 0))
        seg_last = jnp.logical_not(is_prompt & ((tile % P_TILES_PER_SEQ) != P_TILES_PER_SEQ - 1))
        h_rows = slice(TM * s, TM * s + TM + 2 * HALO)

        def conv3(cs):
            u = jnp.dot(hbuf[h_rows], wu_ref[:, cs], preferred_element_type=F32)
            p0 = jnp.where(seg_first, 0.0, u[8:16])
            n0 = jnp.where(seg_last, 0.0, u[TM + 16:TM + 24])
            prev = jnp.concatenate([p0, u[16:TM + 8]], axis=0)
            nxt = jnp.concatenate([u[24:TM + 16], n0], axis=0)
            return (cb_ref[:, cs] + prev * cw_ref[0:1, cs] + u[16:TM + 16] * cw_ref[1:2, cs]
                    + nxt * cw_ref[2:3, cs])

        for c in range(D_FF // FF_CHUNK):
            v = conv3(slice(FF_CHUNK * c, FF_CHUNK * (c + 1)))
            gg = conv3(slice(D_FF + FF_CHUNK * c, D_FF + FF_CHUNK * (c + 1)))
            act[s, :, FF_CHUNK * c:FF_CHUNK * (c + 1)] = (v * (gg * jax.nn.sigmoid(gg))).astype(BF16)
            if next_piece < n_pieces:
                fill_piece(next_piece)
                next_piece += 1

        out = jnp.dot(act[s], wd_ref[...], preferred_element_type=F32)
        y = x_ref[TM * s:TM * (s + 1)] + _per_batch(out, gt2)
        if final_norm:
            finals.append(jnp.swapaxes(_rms(y, gf_ref[...]).reshape(STEPS, NB, D), 0, 1))
        else:
            o_ref[TM * s:TM * (s + 1)] = y

    if final_norm:
        y_bt = jnp.concatenate(finals, axis=1)

        @pl.when(k < N_P_TILES // FFN_SUB)
        def _():
            op_ref[...] = y_bt

        @pl.when(k >= N_P_TILES // FFN_SUB)
        def _():
            os_ref[...] = y_bt


def _ffn(x, mod, layer, g, wu, cw, cb, wd, gf, final_norm):
    rows = FFN_SUB * TM
    n_steps = NT // FFN_SUB
    blk = pl.BlockSpec((rows, D), lambda i: (i, 0))
    prev = pl.BlockSpec((HALO, D), lambda i: (jnp.maximum(i * (rows // HALO) - 1, 0), 0))
    nxt = pl.BlockSpec((HALO, D), lambda i: (jnp.minimum((i + 1) * (rows // HALO), N_ROWS // HALO - 1), 0))
    mod_spec = pl.BlockSpec((None, NB, N_MOD * D), lambda i: (jnp.where(i >= N_P_TILES // FFN_SUB, 1, 0), 0, 0))

    def layer_spec(shape):
        n = len(shape)
        return pl.BlockSpec((None,) + shape, lambda *_: (layer,) + (0,) * n, pipeline_mode=pl.Buffered(1))

    if final_norm:
        out_shape = (jax.ShapeDtypeStruct((2 * NB, P_LEN, D), F32), jax.ShapeDtypeStruct((NB, S_LEN, D), F32))
        out_specs = _bt_specs(FFN_SUB * STEPS, 0, FFN_SUB * STEPS)
    else:
        out_shape = jax.ShapeDtypeStruct((N_ROWS, D), F32)
        out_specs = blk
    return pl.pallas_call(
        functools.partial(_ffn_kernel, final_norm=final_norm),
        out_shape=out_shape,
        grid=(n_steps,),
        in_specs=[blk, prev, nxt, mod_spec, layer_spec((1, D)),
                  layer_spec((D, 2 * D_FF)), layer_spec((3, 2 * D_FF)), layer_spec((1, 2 * D_FF)),
                  layer_spec((D_FF, D)), _const_spec((1, D))],
        out_specs=out_specs,
        scratch_shapes=[pltpu.VMEM((rows + 2 * HALO, D), BF16), pltpu.VMEM((FFN_SUB, TM, D_FF), BF16)],
        compiler_params=_params(("arbitrary",)),
        name="conv_ffn",
    )(x, x, x, mod, g, wu, cw, cb, wd, gf)


def _s5_core(ubuf, bb_ref, cc_ref, ab_ref, slabs, hs16, hst, y_write, reverse, c_cols):
    def b_piece(kt, n):
        cs = slice(S5_BN * n, S5_BN * (n + 1))
        slabs[kt % 2, :, cs] = jnp.dot(ubuf[:, 256 * kt:256 * (kt + 1)], bb_ref[kt, :, cs],
                                       preferred_element_type=F32)

    def c_piece(kt, q):
        cs = slice(c_cols * q, c_cols * (q + 1))
        return jnp.dot(hs16[kt % 2, :, cs], cc_ref[kt, cs, :], preferred_element_type=F32)

    n_b = SLAB // S5_BN
    every = STEPS // n_b
    n_c = SLAB // c_cols
    c_every = STEPS // n_c
    for n in range(n_b):
        b_piece(0, n)
    for kt in range(S5_KT):
        slab = slabs.at[kt % 2]
        hs = hs16.at[kt % 2]
        a_re = jnp.broadcast_to(ab_ref[kt, 0:1, :], (NB, D))
        a_im = jnp.broadcast_to(ab_ref[kt, 1:2, :], (NB, D))
        h_re = hst[kt, :, 0:D]
        h_im = hst[kt, :, D:2 * D]
        part = None
        for s in range(STEPS):
            if s % every == 0 and kt + 1 < S5_KT:
                b_piece(kt + 1, s // every)
            if s % c_every == c_every // 2 and kt >= 1:
                q = s // c_every
                part = c_piece(kt - 1, q) if q == 0 else part + c_piece(kt - 1, q)
                if q == n_c - 1:
                    y_write(kt - 1, part)
            t = (STEPS - 1 - s) if reverse else s
            rows = slice(8 * t, 8 * (t + 1))
            n_re = a_re * h_re - a_im * h_im + slab[rows, 0:D]
            n_im = a_re * h_im + a_im * h_re + slab[rows, D:2 * D]
            if s % 2 == 1:
                lo = min(t, t + 1 if reverse else t - 1)
                pair = slice(8 * lo, 8 * lo + 16)
                first, second = ((n_re, h_re), (n_im, h_im)) if reverse else ((h_re, n_re), (h_im, n_im))
                hs[pair, 0:D] = jnp.concatenate(first, axis=0).astype(BF16)
                hs[pair, D:2 * D] = jnp.concatenate(second, axis=0).astype(BF16)
            h_re, h_im = n_re, n_im
        hst[kt, :, 0:D] = h_re
        hst[kt, :, D:2 * D] = h_im
    part = c_piece(S5_KT - 1, 0)
    for q in range(1, n_c):
        part = part + c_piece(S5_KT - 1, q)
    y_write(S5_KT - 1, part)


def _s5_bwd_kernel(x_ref, mod_ref, g_ref, bb_ref, cc_ref, ab_ref, h0_ref,
                   yb_ref, fin_ref, ubuf, slabs, hs16, hst):
    j = NT - 1 - pl.program_id(0)

    @pl.when(_is_seq_last(j))
    def _():
        hst[...] = h0_ref[...]

    _fill_norm(ubuf, 0, x_ref, g_ref[...], 1.0 + mod_ref[:, D:2 * D], mod_ref[:, 0:D])

    def y_write(kt, val):
        yb_ref[:, 256 * kt:256 * (kt + 1)] = val

    _s5_core(ubuf, bb_ref, cc_ref, ab_ref, slabs, hs16, hst, y_write, True, S5_CK_BWD)
    fin_ref[...] = hst[...]


def _s5_bwd(x, mod, g, bb, cc, ab, h0):
    tile = pl.BlockSpec((TM, D), lambda i: (NT - 1 - i, 0))
    seq = pl.BlockSpec((None, S5_KT, NB, SLAB), lambda i: (_seq_id(NT - 1 - i), 0, 0, 0))
    return pl.pallas_call(
        _s5_bwd_kernel,
        out_shape=(jax.ShapeDtypeStruct((N_ROWS, D), F32),
                   jax.ShapeDtypeStruct((N_SEQ, S5_KT, NB, SLAB), F32)),
        grid=(NT,),
        in_specs=[tile, _mod_spec(lambda i: NT - 1 - i), _const_spec((1, D)),
                  _const_spec((S5_KT, 256, SLAB)), _const_spec((S5_KT, SLAB, 256)),
                  _const_spec((S5_KT, 2, D)), seq],
        out_specs=(tile, seq),
        scratch_shapes=[pltpu.VMEM((TM, D), BF16), pltpu.VMEM((2, TM, SLAB), F32), pltpu.VMEM((2, TM, SLAB), BF16),
                        pltpu.VMEM((S5_KT, NB, SLAB), F32)],
        compiler_params=_params(("arbitrary",)),
        name="s5_bwd",
    )(x, mod, g, bb, cc, ab, h0)


def _s5_fwd_kernel(x_ref, yb_ref, mod_ref, g_ref, bb_ref, cc_ref, ab_ref, h0_ref, dsk_ref, wglu_ref,
                   x1_ref, fin_ref, ubuf, zbuf, u_s, slabs, hs16, hst):
    j = pl.program_id(0)

    @pl.when(_is_seq_first(j))
    def _():
        hst[...] = h0_ref[...]

    _fill_norm(ubuf, 0, x_ref, g_ref[...], 1.0 + mod_ref[:, D:2 * D], mod_ref[:, 0:D], u_s)

    def y_write(kt, val):
        cs = slice(256 * kt, 256 * (kt + 1))
        y = dsk_ref[:, cs] * u_s[:, cs] + yb_ref[:, cs] + val
        zbuf[:, cs] = jax.nn.gelu(y).astype(BF16)

    _s5_core(ubuf, bb_ref, cc_ref, ab_ref, slabs, hs16, hst, y_write, False, S5_CK_FWD)
    fin_ref[...] = hst[...]

    for c in range(D // COL_CHUNK):
        cs = slice(COL_CHUNK * c, COL_CHUNK * (c + 1))
        v = jnp.dot(zbuf[...], wglu_ref[:, cs], preferred_element_type=F32)
        gg = jnp.dot(zbuf[...], wglu_ref[:, D + COL_CHUNK * c:D + COL_CHUNK * (c + 1)], preferred_element_type=F32)
        out = v * jax.nn.sigmoid(gg)
        x1_ref[:, cs] = x_ref[:, cs] + _per_batch(out, mod_ref[:, 2 * D + COL_CHUNK * c:2 * D + COL_CHUNK * (c + 1)])


def _s5_fwd(x, yb, mod, g, bb, cc, ab, h0, dsk, wglu):
    tile = pl.BlockSpec((TM, D), lambda i: (i, 0))
    seq = pl.BlockSpec((None, S5_KT, NB, SLAB), lambda i: (_seq_id(i), 0, 0, 0))
    return pl.pallas_call(
        _s5_fwd_kernel,
        out_shape=(jax.ShapeDtypeStruct((N_ROWS, D), F32),
                   jax.ShapeDtypeStruct((N_SEQ, S5_KT, NB, SLAB), F32)),
        grid=(NT,),
        in_specs=[tile, tile, _mod_spec(lambda i: i), _const_spec((1, D)),
                  _const_spec((S5_KT, 256, SLAB)), _const_spec((S5_KT, SLAB, 256)),
                  _const_spec((S5_KT, 2, D)), seq, _const_spec((1, D)), _const_spec((D, 2 * D))],
        out_specs=(tile, seq),
        scratch_shapes=[pltpu.VMEM((TM, D), BF16), pltpu.VMEM((TM, D), BF16), pltpu.VMEM((TM, D), F32),
                        pltpu.VMEM((2, TM, SLAB), F32), pltpu.VMEM((2, TM, SLAB), BF16),
                        pltpu.VMEM((S5_KT, NB, SLAB), F32)],
        compiler_params=_params(("arbitrary",)),
        name="s5_fwd",
    )(x, yb, mod, g, bb, cc, ab, h0, dsk, wglu)


def _pair_blockdiag(w_a, w_i):
    same = (jnp.arange(2)[:, None, None, None] == jnp.arange(2)[None, None, :, None])

    def bd(w):
        w4 = w.astype(F32).reshape(8, 2, 64, 1, 64)
        return jnp.where(same[None], w4, 0.0).reshape(8, 128, 128)

    return (0.5 * jnp.concatenate([bd(w_a), bd(w_i)], axis=-1)).astype(BF16)


def _s5_dir_params(a_re, a_im, log_dt, b_re, b_im, c_re, c_im):
    l_re = a_re.astype(F32)
    l_im = a_im.astype(F32)
    dt = jnp.exp(log_dt.astype(F32))[:, None]
    mag = jnp.exp(l_re * dt)
    ab_re = mag * jnp.cos(l_im * dt)
    ab_im = mag * jnp.sin(l_im * dt)
    den = l_re * l_re + l_im * l_im
    k_re = ((ab_re - 1.0) * l_re + ab_im * l_im) / den
    k_im = (ab_im * l_re - (ab_re - 1.0) * l_im) / den
    br = b_re.astype(F32)
    bi = b_im.astype(F32)
    bb_re = br * k_re[..., None] - bi * k_im[..., None]
    bb_im = br * k_im[..., None] + bi * k_re[..., None]
    same = (jnp.arange(16)[:, None, None, None] == jnp.arange(16)[None, None, :, None])[None]

    def b_blk(m):
        m5 = m.reshape(S5_KT, 16, 64, 16).transpose(0, 1, 3, 2)[:, :, :, None, :]
        return jnp.where(same, m5, 0.0).reshape(S5_KT, 256, 1024)

    def c_blk(m):
        rows = m.reshape(S5_KT, 16, 16, 64).transpose(0, 1, 3, 2).reshape(S5_KT, 1024, 16)
        spread = (jnp.arange(256)[None, :] % 16 == jnp.arange(16)[:, None]).astype(F32)
        tiled = jnp.einsum('krc,cq->krq', rows, spread, precision=lax.Precision.HIGHEST)
        keep = (jnp.arange(1024)[:, None] // 64) == (jnp.arange(256)[None, :] // 16)
        return jnp.where(keep[None], tiled, 0.0)

    bb = jnp.concatenate([b_blk(bb_re), b_blk(bb_im)], axis=-1).astype(BF16)
    cc = jnp.concatenate([c_blk(c_re.astype(F32)), c_blk(-c_im.astype(F32))], axis=1).astype(BF16)
    ab = jnp.stack([ab_re.reshape(S5_KT, D), ab_im.reshape(S5_KT, D)], axis=1)
    return bb, cc, ab


def _s5_state_to_slab(s_re, s_im):
    re = s_re.astype(F32).reshape(NB, S5_KT, D).transpose(1, 0, 2)
    im = s_im.astype(F32).reshape(NB, S5_KT, D).transpose(1, 0, 2)
    return jnp.concatenate([re, im], axis=-1)


def _s5_slab_to_state(fin):
    re = fin[..., 0:D].transpose(0, 2, 1, 3).reshape(2 * NB, 64, 64)
    im = fin[..., D:2 * D].transpose(0, 2, 1, 3).reshape(2 * NB, 64, 64)
    return re, im


def kernel(x_prompt, x_sample, state_lru, state_s5_re, state_s5_im, c, c_ctx, ada_w, ada_b, norm_mix, norm_ffn, norm_final, lru_w_in, lru_conv_w, lru_conv_b, lru_w_a, lru_b_a, lru_w_i, lru_b_i, lru_lambda, lru_w_out, s5_a_re, s5_a_im, s5_log_dt, s5_b_re, s5_b_im, s5_c_re, s5_c_im, s5_d, s5_w_glu, ffn_w_up, ffn_conv_w, ffn_conv_b, ffn_w_down):
    c_all = jnp.concatenate([c.astype(F32), c_ctx.astype(F32)[None], jnp.zeros((7, D), F32)], axis=0)
    mods = _mod_vectors(c_all, ada_w.astype(F32), ada_b.astype(F32)[:, None, :])

    def mod_of(l):
        return jnp.stack([jnp.broadcast_to(mods[l, 8][None], (NB, N_MOD * D)), mods[l, 0:NB]], axis=0)

    xp = x_prompt.astype(F32)
    gf = norm_final[None].astype(F32)

    mod0 = mod_of(0)
    zeros_h = jnp.zeros((NB, D), F32)
    sp = jax.nn.softplus(-lru_lambda[0].astype(F32)) * (-LRU_C * LOG2_E * 0.5)
    ba = 0.5 * lru_b_a[0].astype(F32)
    bi = 0.5 * lru_b_i[0].astype(F32)
    h0_f = jnp.stack([zeros_h, zeros_h, state_lru[:, 0, 0].astype(F32)], axis=0)
    h0_b = jnp.stack([zeros_h, zeros_h, state_lru[:, 0, 1].astype(F32)], axis=0)
    xs = x_sample.astype(F32)
    gate, xr, hsb, fin_b = _lru_a(xp, xs, mod0, norm_mix[0][None].astype(F32),
                                  lru_w_in[0].astype(BF16), lru_conv_w[0].astype(F32),
                                  lru_conv_b[0][None].astype(F32),
                                  _pair_blockdiag(lru_w_a[0, 1], lru_w_i[0, 1]), ba[1][None], bi[1][None],
                                  sp[1][None], h0_b)
    x, fin_f = _lru_fwd(xr, gate, hsb, xp, xs, mod0, _pair_blockdiag(lru_w_a[0, 0], lru_w_i[0, 0]),
                        ba[0][None], bi[0][None], sp[0][None], h0_f, lru_w_out[0].astype(BF16))
    ffn_args = (norm_ffn.astype(F32)[:, None, :], ffn_w_up.astype(BF16), ffn_conv_w.astype(F32),
                ffn_conv_b.astype(F32)[:, None, :], ffn_w_down.astype(BF16), gf)
    x = _ffn(x, mod0, 0, *ffn_args, False)
    new_lru = jnp.stack([fin_f[0:2].reshape(2 * NB, D), fin_b[0:2].reshape(2 * NB, D)], axis=1)[:, None]

    mod1 = mod_of(1)
    g1 = norm_mix[1][None].astype(F32)
    zeros_s = jnp.zeros((S5_KT, NB, SLAB), F32)
    dirs = []
    for d in range(2):
        bb, cc, ab = _s5_dir_params(s5_a_re[0, d], s5_a_im[0, d], s5_log_dt[0, d], s5_b_re[0, d], s5_b_im[0, d],
                                    s5_c_re[0, d], s5_c_im[0, d])
        h0 = jnp.stack([zeros_s, zeros_s, _s5_state_to_slab(state_s5_re[:, 0, d], state_s5_im[:, 0, d])], axis=0)
        dirs.append((bb, cc, ab, h0))
    yb, fin_sb = _s5_bwd(x, mod1, g1, *dirs[1])
    x, fin_sf = _s5_fwd(x, yb, mod1, g1, *dirs[0], s5_d[0][None].astype(F32), s5_w_glu[0].astype(BF16))
    y_prompt, y_sample = _ffn(x, mod1, 1, *ffn_args, True)
    f_re, f_im = _s5_slab_to_state(fin_sf[0:2])
    b_re, b_im = _s5_slab_to_state(fin_sb[0:2])
    new_s5_re = jnp.stack([f_re, b_re], axis=1)[:, None]
    new_s5_im = jnp.stack([f_im, b_im], axis=1)[:, None]

    return (y_prompt, y_sample, new_lru, new_s5_re, new_s5_im)
```

```python
import functools

import jax
import jax.numpy as jnp
from jax import lax
from jax.experimental import pallas as pl
from jax.experimental.pallas import tpu as pltpu

F32 = jnp.float32
BF16 = jnp.bfloat16

D = 1024
D_FF = 2816
N_MOD = 6
EPS = 1e-6
LRU_C = 8.0
LOG2_E = 1.4426950408889634
NB = 8
STEPS = 64
TM = STEPS * NB
HALO = 16
NORM_ROWS = 64
FF_CHUNK = 256
FFN_SUB = 2
COL_CHUNK = 256
P_LEN = 256
S_LEN = 4096
P_TILES_PER_SEQ = P_LEN // STEPS
N_P_TILES = 2 * P_TILES_PER_SEQ
N_S_TILES = S_LEN // STEPS
NT = N_P_TILES + N_S_TILES
N_ROWS = NT * TM
N_SEQ = 3
S5_KT = 4
SLAB = 2048
S5_BN = 256
S5_CK_BWD = 1024
S5_CK_FWD = 256
VMEM_LIMIT = 56 * 1024 * 1024


def _seq_id(j):
    return jnp.where(j >= N_P_TILES, 2, jnp.where(j >= P_TILES_PER_SEQ, 1, 0))


def _is_seq_first(j):
    return (j == 0) | (j == P_TILES_PER_SEQ) | (j == N_P_TILES)


def _is_seq_last(j):
    return (j == P_TILES_PER_SEQ - 1) | (j == N_P_TILES - 1) | (j == NT - 1)


def _params(sem):
    return pltpu.CompilerParams(dimension_semantics=sem, vmem_limit_bytes=VMEM_LIMIT)


def _const_spec(shape):
    n = len(shape)
    return pl.BlockSpec(shape, lambda *_: (0,) * n, pipeline_mode=pl.Buffered(1))


def _mod_spec(tile_of):
    return pl.BlockSpec((None, NB, N_MOD * D),
                        lambda i: (jnp.where(tile_of(i) >= N_P_TILES, 1, 0), 0, 0))


def _rms(xv, g):
    ms = jnp.mean(xv * xv, axis=-1, keepdims=True)
    return xv * lax.rsqrt(ms + EPS) * g


def _per_batch(y, vec):
    r, n = y.shape
    return (y.reshape(r // NB, NB, n) * vec[None]).reshape(r, n)


def _norm_mod(xv, g, scale1, shift):
    r = xv.shape[0]
    y = _rms(xv, g).reshape(r // NB, NB, D) * scale1[None] + shift[None]
    return y.reshape(r, D)


def _fill_norm(dst, dst_off, x_ref, g, scale1, shift, f32_dst=None):
    for k in range(TM // NORM_ROWS):
        r0 = k * NORM_ROWS
        y = _norm_mod(x_ref[r0:r0 + NORM_ROWS], g, scale1, shift)
        dst[dst_off + r0:dst_off + r0 + NORM_ROWS] = y.astype(BF16)
        if f32_dst is not None:
            f32_dst[r0:r0 + NORM_ROWS] = y


def _fill_hbuf(hbuf, x_ref, x_prev, x_next, g, scale1, shift):
    hbuf[0:HALO] = _norm_mod(x_prev, g, scale1, shift).astype(BF16)
    hbuf[HALO + TM:HALO + TM + HALO] = _norm_mod(x_next, g, scale1, shift).astype(BF16)
    _fill_norm(hbuf, HALO, x_ref, g, scale1, shift)


def _mod_kernel(c_ref, w_ref, b_ref, o_ref):
    cv = c_ref[...]
    s = (cv * jax.nn.sigmoid(cv)).astype(BF16)
    o_ref[...] = jnp.dot(s, w_ref[...].astype(BF16), preferred_element_type=F32) + b_ref[...]


def _mod_vectors(c_all, ada_w, ada_b):
    depth = ada_w.shape[0]
    return pl.pallas_call(
        _mod_kernel,
        out_shape=jax.ShapeDtypeStruct((depth, 16, N_MOD * D), F32),
        grid=(depth, N_MOD),
        in_specs=[
            pl.BlockSpec((16, D), lambda l, n: (0, 0)),
            pl.BlockSpec((None, D, D), lambda l, n: (l, 0, n)),
            pl.BlockSpec((None, 1, D), lambda l, n: (l, 0, n)),
        ],
        out_specs=pl.BlockSpec((None, 16, D), lambda l, n: (l, 0, n)),
        compiler_params=_params(("arbitrary", "arbitrary")),
        name="mod_vectors",
    )(c_all, ada_w, ada_b)


def _bt_specs(steps, offset, tile_steps=STEPS, tile_of=lambda i: i):
    per_tile = tile_steps // steps
    tiles_per_seq = P_LEN // tile_steps
    n_p = 2 * tiles_per_seq

    def prompt_idx(i):
        ip = jnp.minimum(tile_of(i), n_p - 1)
        blk = jnp.clip((ip % tiles_per_seq) * per_tile + offset, 0, P_LEN // steps - 1)
        return (ip // tiles_per_seq, blk, 0)

    def sample_idx(i):
        blk = jnp.clip((tile_of(i) - n_p) * per_tile + offset, 0, S_LEN // steps - 1)
        return (0, blk, 0)

    return (pl.BlockSpec((NB, steps, D), prompt_idx), pl.BlockSpec((NB, steps, D), sample_idx))


def _to_rows(x_bt):
    n = x_bt.shape[1]
    return jnp.swapaxes(x_bt, 0, 1).reshape(n * NB, D)


def _lru_gate_math(z, ba, bi, sp, xr):
    t_r = jnp.tanh(z[:, 0:128] + ba)
    t_i = jnp.tanh(z[:, 128:256] + bi)
    a = jnp.exp2(sp + sp * t_r)
    s = 1.0 - a * a
    root = jnp.where(s > 0.0, s * lax.rsqrt(s), 0.0)
    return a, root * ((0.5 + 0.5 * t_i) * xr)


def _lru_scan(a_s, bx_s, h_s, out_ref, reverse):
    h = h_s[...]
    for s in range(STEPS):
        t = (STEPS - 1 - s) if reverse else s
        rows = slice(8 * t, 8 * (t + 1))
        h = a_s[rows] * h + bx_s[rows]
        out_ref[rows] = h
    h_s[...] = h


def _lru_a_kernel(xa_ref, xap_ref, xan_ref, xb_ref, xbp_ref, xbn_ref, mod_ref, g_ref, wi_ref, cw_ref, cb_ref,
                  wp_ref, ba_ref, bi_ref, sp_ref, h0_ref,
                  gate_ref, xr_ref, hs_ref, fin_ref, hbuf, xc_s, a_s, bx_s, h_s):
    j = NT - 1 - pl.program_id(0)

    @pl.when(_is_seq_last(j))
    def _():
        h_s[...] = h0_ref[...]

    shift = mod_ref[:, 0:D]
    scale1 = 1.0 + mod_ref[:, D:2 * D]

    is_p = j < N_P_TILES
    xc_s[...] = _to_rows(jnp.where(is_p, xa_ref[...], xb_ref[...]))
    x_prev = _to_rows(jnp.where(is_p, xap_ref[...], xbp_ref[...]))[NB * NB - HALO:NB * NB]
    x_next = _to_rows(jnp.where(is_p, xan_ref[...], xbn_ref[...]))[0:HALO]
    _fill_hbuf(hbuf, xc_s, x_prev, x_next, g_ref[...], scale1, shift)

    first = _is_seq_first(j)
    last = _is_seq_last(j)

    for c in range(D // COL_CHUNK):
        cs = slice(COL_CHUNK * c, COL_CHUNK * (c + 1))
        cx = slice(D + COL_CHUNK * c, D + COL_CHUNK * (c + 1))
        gate_ref[:, cs] = jax.nn.gelu(
            jnp.dot(hbuf[HALO:HALO + TM], wi_ref[:, cs], preferred_element_type=F32)).astype(BF16)
        u = jnp.dot(hbuf[...], wi_ref[:, cx], preferred_element_type=F32)
        head = jnp.where(first, 0.0, u[0:16])
        tail = jnp.where(last, 0.0, u[TM + 16:TM + 24])
        m2 = jnp.concatenate([head, u[16:TM]], axis=0)
        m1 = jnp.concatenate([head[8:16], u[16:TM + 8]], axis=0)
        p1 = jnp.concatenate([u[24:TM + 16], tail], axis=0)
        xr = (cb_ref[:, cs] + m2 * cw_ref[0:1, cs] + m1 * cw_ref[1:2, cs]
              + u[16:TM + 16] * cw_ref[2:3, cs] + p1 * cw_ref[3:4, cs])
        xr16 = xr.astype(BF16)
        xr_ref[:, cs] = xr16
        for h in range(COL_CHUNK // 128):
            p = (COL_CHUNK // 128) * c + h
            cols = slice(128 * p, 128 * (p + 1))
            z = jnp.dot(xr16[:, 128 * h:128 * (h + 1)], wp_ref[p], preferred_element_type=F32)
            a, bx = _lru_gate_math(z, ba_ref[:, cols], bi_ref[:, cols], sp_ref[:, cols],
                                   xr[:, 128 * h:128 * (h + 1)])
            a_s[:, cols] = a
            bx_s[:, cols] = bx

    _lru_scan(a_s, bx_s, h_s, bx_s, True)
    hs_ref[...] = bx_s[...].astype(BF16)
    fin_ref[...] = h_s[...]


def _lru_a(xp, xs, mod, g, wi, cw, cb, wp, ba, bi, sp, h0):
    rev = lambda i: NT - 1 - i
    p_tile, s_tile = _bt_specs(STEPS, 0, tile_of=rev)
    p_prev, s_prev = _bt_specs(NB, -1, tile_of=rev)
    p_next, s_next = _bt_specs(NB, STEPS // NB, tile_of=rev)
    tile = pl.BlockSpec((TM, D), lambda i: (rev(i), 0))
    seq = pl.BlockSpec((None, NB, D), lambda i: (_seq_id(rev(i)), 0, 0))
    rows16 = jax.ShapeDtypeStruct((N_ROWS, D), BF16)
    return pl.pallas_call(
        _lru_a_kernel,
        out_shape=(rows16, rows16, rows16, jax.ShapeDtypeStruct((N_SEQ, NB, D), F32)),
        grid=(NT,),
        in_specs=[p_tile, p_prev, p_next, s_tile, s_prev, s_next,
                  _mod_spec(rev), _const_spec((1, D)),
                  _const_spec((D, 2 * D)), _const_spec((4, D)), _const_spec((1, D)),
                  _const_spec((8, 128, 256)), _const_spec((1, D)), _const_spec((1, D)), _const_spec((1, D)), seq],
        out_specs=(tile, tile, tile, seq),
        scratch_shapes=[pltpu.VMEM((TM + 2 * HALO, D), BF16), pltpu.VMEM((TM, D), F32), pltpu.VMEM((TM, D), F32),
                        pltpu.VMEM((TM, D), F32), pltpu.VMEM((NB, D), F32)],
        compiler_params=_params(("arbitrary",)),
        name="lru_a",
    )(xp, xp, xp, xs, xs, xs, mod, g, wi, cw, cb, wp, ba, bi, sp, h0)


def _lru_fwd_kernel(xr_ref, gate_ref, hsb_ref, xa_ref, xb_ref, mod_ref, wp_ref, ba_ref, bi_ref, sp_ref, h0_ref,
                    wo_ref, x1_ref, fin_ref, a_s, bx_s, y_s, h_s):
    j = pl.program_id(0)

    @pl.when(_is_seq_first(j))
    def _():
        h_s[...] = h0_ref[...]

    for p in range(8):
        cols = slice(128 * p, 128 * (p + 1))
        z = jnp.dot(xr_ref[:, cols], wp_ref[p], preferred_element_type=F32)
        a, bx = _lru_gate_math(z, ba_ref[:, cols], bi_ref[:, cols], sp_ref[:, cols],
                               xr_ref[:, cols].astype(F32))
        a_s[:, cols] = a
        bx_s[:, cols] = bx
    _lru_scan(a_s, bx_s, h_s, bx_s, False)
    fin_ref[...] = h_s[...]

    y_s[...] = ((bx_s[...] + hsb_ref[...].astype(F32)) * gate_ref[...].astype(F32)).astype(BF16)
    x_rows = _to_rows(jnp.where(j < N_P_TILES, xa_ref[...], xb_ref[...]))
    for c in range(D // COL_CHUNK):
        cs = slice(COL_CHUNK * c, COL_CHUNK * (c + 1))
        out = jnp.dot(y_s[...], wo_ref[:, cs], preferred_element_type=F32)
        x1_ref[:, cs] = x_rows[:, cs] + _per_batch(out, mod_ref[:, 2 * D + COL_CHUNK * c:2 * D + COL_CHUNK * (c + 1)])


def _lru_fwd(xr, gate, hsb, xp, xs, mod, wp, ba, bi, sp, h0, wo):
    tile = pl.BlockSpec((TM, D), lambda i: (i, 0))
    p_tile, s_tile = _bt_specs(STEPS, 0)
    seq = pl.BlockSpec((None, NB, D), lambda i: (_seq_id(i), 0, 0))
    return pl.pallas_call(
        _lru_fwd_kernel,
        out_shape=(jax.ShapeDtypeStruct((N_ROWS, D), F32), jax.ShapeDtypeStruct((N_SEQ, NB, D), F32)),
        grid=(NT,),
        in_specs=[tile, tile, tile, p_tile, s_tile, _mod_spec(lambda i: i), _const_spec((8, 128, 256)),
                  _const_spec((1, D)), _const_spec((1, D)), _const_spec((1, D)), seq, _const_spec((D, D))],
        out_specs=(tile, seq),
        scratch_shapes=[pltpu.VMEM((TM, D), F32), pltpu.VMEM((TM, D), F32), pltpu.VMEM((TM, D), BF16),
                        pltpu.VMEM((NB, D), F32)],
        compiler_params=_params(("arbitrary",)),
        name="lru_fwd",
    )(xr, gate, hsb, xp, xs, mod, wp, ba, bi, sp, h0, wo)


def _ffn_kernel(x_ref, xp_ref, xn_ref, mod_ref, g_ref, wu_ref, cw_ref, cb_ref, wd_ref, gf_ref,
                *rest, final_norm):
    if final_norm:
        op_ref, os_ref, hbuf, act = rest
    else:
        o_ref, hbuf, act = rest
    k = pl.program_id(0)
    g = g_ref[...]
    shift = mod_ref[:, 3 * D:4 * D]
    scale1 = 1.0 + mod_ref[:, 4 * D:5 * D]
    gt2 = mod_ref[:, 5 * D:6 * D]
    rows_all = FFN_SUB * TM

    def fill_piece(p):
        n_blocks = rows_all // NORM_ROWS
        if p == 0:
            hbuf[0:HALO] = _norm_mod(xp_ref[...], g, scale1, shift).astype(BF16)
        elif p <= n_blocks:
            r0 = (p - 1) * NORM_ROWS
            hbuf[HALO + r0:HALO + r0 + NORM_ROWS] = _norm_mod(
                x_ref[r0:r0 + NORM_ROWS], g, scale1, shift).astype(BF16)
        elif p == n_blocks + 1:
            hbuf[HALO + rows_all:HALO + rows_all + HALO] = _norm_mod(xn_ref[...], g, scale1, shift).astype(BF16)

    n_pieces = rows_all // NORM_ROWS + 2
    first_pieces = TM // NORM_ROWS + 2
    for p in range(first_pieces):
        fill_piece(p)
    next_piece = first_pieces
    finals = []

    for s in range(FFN_SUB):
        tile = FFN_SUB * k + s
        is_prompt = tile < N_P_TILES
        seg_first = jnp.logical_not(is_prompt & ((tile % P_TILES_PER_SEQ) != 0))
        seg_last = jnp.logical_not(is_prompt & ((tile % P_TILES_PER_SEQ) != P_TILES_PER_SEQ - 1))
        h_rows = slice(TM * s, TM * s + TM + 2 * HALO)

        def conv3(cs):
            u = jnp.dot(hbuf[h_rows], wu_ref[:, cs], preferred_element_type=F32)
            p0 = jnp.where(seg_first, 0.0, u[8:16])
            n0 = jnp.where(seg_last, 0.0, u[TM + 16:TM + 24])
            prev = jnp.concatenate([p0, u[16:TM + 8]], axis=0)
            nxt = jnp.concatenate([u[24:TM + 16], n0], axis=0)
            return (cb_ref[:, cs] + prev * cw_ref[0:1, cs] + u[16:TM + 16] * cw_ref[1:2, cs]
                    + nxt * cw_ref[2:3, cs])

        for c in range(D_FF // FF_CHUNK):
            v = conv3(slice(FF_CHUNK * c, FF_CHUNK * (c + 1)))
            gg = conv3(slice(D_FF + FF_CHUNK * c, D_FF + FF_CHUNK * (c + 1)))
            act[s, :, FF_CHUNK * c:FF_CHUNK * (c + 1)] = (v * (gg * jax.nn.sigmoid(gg))).astype(BF16)
            if next_piece < n_pieces:
                fill_piece(next_piece)
                next_piece += 1

        out = jnp.dot(act[s], wd_ref[...], preferred_element_type=F32)
        y = x_ref[TM * s:TM * (s + 1)] + _per_batch(out, gt2)
        if final_norm:
            finals.append(jnp.swapaxes(_rms(y, gf_ref[...]).reshape(STEPS, NB, D), 0, 1))
        else:
            o_ref[TM * s:TM * (s + 1)] = y

    if final_norm:
        y_bt = jnp.concatenate(finals, axis=1)

        @pl.when(k < N_P_TILES // FFN_SUB)
        def _():
            op_ref[...] = y_bt

        @pl.when(k >= N_P_TILES // FFN_SUB)
        def _():
            os_ref[...] = y_bt


def _ffn(x, mod, layer, g, wu, cw, cb, wd, gf, final_norm):
    rows = FFN_SUB * TM
    n_steps = NT // FFN_SUB
    blk = pl.BlockSpec((rows, D), lambda i: (i, 0))
    prev = pl.BlockSpec((HALO, D), lambda i: (jnp.maximum(i * (rows // HALO) - 1, 0), 0))
    nxt = pl.BlockSpec((HALO, D), lambda i: (jnp.minimum((i + 1) * (rows // HALO), N_ROWS // HALO - 1), 0))
    mod_spec = pl.BlockSpec((None, NB, N_MOD * D), lambda i: (jnp.where(i >= N_P_TILES // FFN_SUB, 1, 0), 0, 0))

    def layer_spec(shape):
        n = len(shape)
        return pl.BlockSpec((None,) + shape, lambda *_: (layer,) + (0,) * n, pipeline_mode=pl.Buffered(1))

    if final_norm:
        out_shape = (jax.ShapeDtypeStruct((2 * NB, P_LEN, D), F32), jax.ShapeDtypeStruct((NB, S_LEN, D), F32))
        out_specs = _bt_specs(FFN_SUB * STEPS, 0, FFN_SUB * STEPS)
    else:
        out_shape = jax.ShapeDtypeStruct((N_ROWS, D), F32)
        out_specs = blk
    return pl.pallas_call(
        functools.partial(_ffn_kernel, final_norm=final_norm),
        out_shape=out_shape,
        grid=(n_steps,),
        in_specs=[blk, prev, nxt, mod_spec, layer_spec((1, D)),
                  layer_spec((D, 2 * D_FF)), layer_spec((3, 2 * D_FF)), layer_spec((1, 2 * D_FF)),
                  layer_spec((D_FF, D)), _const_spec((1, D))],
        out_specs=out_specs,
        scratch_shapes=[pltpu.VMEM((rows + 2 * HALO, D), BF16), pltpu.VMEM((FFN_SUB, TM, D_FF), BF16)],
        compiler_params=_params(("arbitrary",)),
        name="conv_ffn",
    )(x, x, x, mod, g, wu, cw, cb, wd, gf)


def _s5_core(ubuf, bb_ref, cc_ref, ab_ref, slabs, hs16, hst, y_write, reverse, c_cols):
    def b_piece(kt, n):
        cs = slice(S5_BN * n, S5_BN * (n + 1))
        slabs[kt % 2, :, cs] = jnp.dot(ubuf[:, 256 * kt:256 * (kt + 1)], bb_ref[kt, :, cs],
                                       preferred_element_type=F32)

    def c_piece(kt, q):
        cs = slice(c_cols * q, c_cols * (q + 1))
        return jnp.dot(hs16[kt % 2, :, cs], cc_ref[kt, cs, :], preferred_element_type=F32)

    n_b = SLAB // S5_BN
    every = STEPS // n_b
    n_c = SLAB // c_cols
    c_every = STEPS // n_c
    for n in range(n_b):
        b_piece(0, n)
    for kt in range(S5_KT):
        slab = slabs.at[kt % 2]
        hs = hs16.at[kt % 2]
        a_re = jnp.broadcast_to(ab_ref[kt, 0:1, :], (NB, D))
        a_im = jnp.broadcast_to(ab_ref[kt, 1:2, :], (NB, D))
        h_re = hst[kt, :, 0:D]
        h_im = hst[kt, :, D:2 * D]
        part = None
        for s in range(STEPS):
            if s % every == 0 and kt + 1 < S5_KT:
                b_piece(kt + 1, s // every)
            if s % c_every == c_every // 2 and kt >= 1:
                q = s // c_every
                part = c_piece(kt - 1, q) if q == 0 else part + c_piece(kt - 1, q)
                if q == n_c - 1:
                    y_write(kt - 1, part)
            t = (STEPS - 1 - s) if reverse else s
            rows = slice(8 * t, 8 * (t + 1))
            n_re = a_re * h_re - a_im * h_im + slab[rows, 0:D]
            n_im = a_re * h_im + a_im * h_re + slab[rows, D:2 * D]
            if s % 2 == 1:
                lo = min(t, t + 1 if reverse else t - 1)
                pair = slice(8 * lo, 8 * lo + 16)
                first, second = ((n_re, h_re), (n_im, h_im)) if reverse else ((h_re, n_re), (h_im, n_im))
                hs[pair, 0:D] = jnp.concatenate(first, axis=0).astype(BF16)
                hs[pair, D:2 * D] = jnp.concatenate(second, axis=0).astype(BF16)
            h_re, h_im = n_re, n_im
        hst[kt, :, 0:D] = h_re
        hst[kt, :, D:2 * D] = h_im
    part = c_piece(S5_KT - 1, 0)
    for q in range(1, n_c):
        part = part + c_piece(S5_KT - 1, q)
    y_write(S5_KT - 1, part)


def _s5_bwd_kernel(x_ref, mod_ref, g_ref, bb_ref, cc_ref, ab_ref, h0_ref,
                   yb_ref, fin_ref, ubuf, slabs, hs16, hst):
    j = NT - 1 - pl.program_id(0)

    @pl.when(_is_seq_last(j))
    def _():
        hst[...] = h0_ref[...]

    _fill_norm(ubuf, 0, x_ref, g_ref[...], 1.0 + mod_ref[:, D:2 * D], mod_ref[:, 0:D])

    def y_write(kt, val):
        yb_ref[:, 256 * kt:256 * (kt + 1)] = val

    _s5_core(ubuf, bb_ref, cc_ref, ab_ref, slabs, hs16, hst, y_write, True, S5_CK_BWD)
    fin_ref[...] = hst[...]


def _s5_bwd(x, mod, g, bb, cc, ab, h0):
    tile = pl.BlockSpec((TM, D), lambda i: (NT - 1 - i, 0))
    seq = pl.BlockSpec((None, S5_KT, NB, SLAB), lambda i: (_seq_id(NT - 1 - i), 0, 0, 0))
    return pl.pallas_call(
        _s5_bwd_kernel,
        out_shape=(jax.ShapeDtypeStruct((N_ROWS, D), F32),
                   jax.ShapeDtypeStruct((N_SEQ, S5_KT, NB, SLAB), F32)),
        grid=(NT,),
        in_specs=[tile, _mod_spec(lambda i: NT - 1 - i), _const_spec((1, D)),
                  _const_spec((S5_KT, 256, SLAB)), _const_spec((S5_KT, SLAB, 256)),
                  _const_spec((S5_KT, 2, D)), seq],
        out_specs=(tile, seq),
        scratch_shapes=[pltpu.VMEM((TM, D), BF16), pltpu.VMEM((2, TM, SLAB), F32), pltpu.VMEM((2, TM, SLAB), BF16),
                        pltpu.VMEM((S5_KT, NB, SLAB), F32)],
        compiler_params=_params(("arbitrary",)),
        name="s5_bwd",
    )(x, mod, g, bb, cc, ab, h0)


def _s5_fwd_kernel(x_ref, yb_ref, mod_ref, g_ref, bb_ref, cc_ref, ab_ref, h0_ref, dsk_ref, wglu_ref,
                   x1_ref, fin_ref, ubuf, zbuf, u_s, slabs, hs16, hst):
    j = pl.program_id(0)

    @pl.when(_is_seq_first(j))
    def _():
        hst[...] = h0_ref[...]

    _fill_norm(ubuf, 0, x_ref, g_ref[...], 1.0 + mod_ref[:, D:2 * D], mod_ref[:, 0:D], u_s)

    def y_write(kt, val):
        cs = slice(256 * kt, 256 * (kt + 1))
        y = dsk_ref[:, cs] * u_s[:, cs] + yb_ref[:, cs] + val
        zbuf[:, cs] = jax.nn.gelu(y).astype(BF16)

    _s5_core(ubuf, bb_ref, cc_ref, ab_ref, slabs, hs16, hst, y_write, False, S5_CK_FWD)
    fin_ref[...] = hst[...]

    for c in range(D // COL_CHUNK):
        cs = slice(COL_CHUNK * c, COL_CHUNK * (c + 1))
        v = jnp.dot(zbuf[...], wglu_ref[:, cs], preferred_element_type=F32)
        gg = jnp.dot(zbuf[...], wglu_ref[:, D + COL_CHUNK * c:D + COL_CHUNK * (c + 1)], preferred_element_type=F32)
        out = v * jax.nn.sigmoid(gg)
        x1_ref[:, cs] = x_ref[:, cs] + _per_batch(out, mod_ref[:, 2 * D + COL_CHUNK * c:2 * D + COL_CHUNK * (c + 1)])


def _s5_fwd(x, yb, mod, g, bb, cc, ab, h0, dsk, wglu):
    tile = pl.BlockSpec((TM, D), lambda i: (i, 0))
    seq = pl.BlockSpec((None, S5_KT, NB, SLAB), lambda i: (_seq_id(i), 0, 0, 0))
    return pl.pallas_call(
        _s5_fwd_kernel,
        out_shape=(jax.ShapeDtypeStruct((N_ROWS, D), F32),
                   jax.ShapeDtypeStruct((N_SEQ, S5_KT, NB, SLAB), F32)),
        grid=(NT,),
        in_specs=[tile, tile, _mod_spec(lambda i: i), _const_spec((1, D)),
                  _const_spec((S5_KT, 256, SLAB)), _const_spec((S5_KT, SLAB, 256)),
                  _const_spec((S5_KT, 2, D)), seq, _const_spec((1, D)), _const_spec((D, 2 * D))],
        out_specs=(tile, seq),
        scratch_shapes=[pltpu.VMEM((TM, D), BF16), pltpu.VMEM((TM, D), BF16), pltpu.VMEM((TM, D), F32),
                        pltpu.VMEM((2, TM, SLAB), F32), pltpu.VMEM((2, TM, SLAB), BF16),
                        pltpu.VMEM((S5_KT, NB, SLAB), F32)],
        compiler_params=_params(("arbitrary",)),
        name="s5_fwd",
    )(x, yb, mod, g, bb, cc, ab, h0, dsk, wglu)


def _pair_blockdiag(w_a, w_i):
    same = (jnp.arange(2)[:, None, None, None] == jnp.arange(2)[None, None, :, None])

    def bd(w):
        w4 = w.astype(F32).reshape(8, 2, 64, 1, 64)
        return jnp.where(same[None], w4, 0.0).reshape(8, 128, 128)

    return (0.5 * jnp.concatenate([bd(w_a), bd(w_i)], axis=-1)).astype(BF16)


def _s5_dir_params(a_re, a_im, log_dt, b_re, b_im, c_re, c_im):
    l_re = a_re.astype(F32)
    l_im = a_im.astype(F32)
    dt = jnp.exp(log_dt.astype(F32))[:, None]
    mag = jnp.exp(l_re * dt)
    ab_re = mag * jnp.cos(l_im * dt)
    ab_im = mag * jnp.sin(l_im * dt)
    den = l_re * l_re + l_im * l_im
    k_re = ((ab_re - 1.0) * l_re + ab_im * l_im) / den
    k_im = (ab_im * l_re - (ab_re - 1.0) * l_im) / den
    br = b_re.astype(F32)
    bi = b_im.astype(F32)
    bb_re = br * k_re[..., None] - bi * k_im[..., None]
    bb_im = br * k_im[..., None] + bi * k_re[..., None]
    def expand(rows, spread, keep):
        tiled = jnp.einsum('krc,cq->krq', rows, spread, precision=lax.Precision.HIGHEST)
        return jnp.where(keep[None], tiled, 0.0).astype(BF16)

    def b_rows(m):
        return m.reshape(S5_KT, 16, 64, 16).transpose(0, 1, 3, 2).reshape(S5_KT, 256, 64)

    def c_rows(m):
        return m.reshape(S5_KT, 16, 16, 64).transpose(0, 1, 3, 2).reshape(S5_KT, 1024, 16)

    q = jnp.arange(SLAB)[None, :]
    h = jnp.arange(128)[:, None]
    spread_b = ((q // D == h // 64) & (q % 64 == h % 64)).astype(F32)
    keep_b = (jnp.arange(256)[:, None] // 16) == ((q % D) // 64)
    bb = expand(jnp.concatenate([b_rows(bb_re), b_rows(bb_im)], axis=-1), spread_b, keep_b)

    r = jnp.arange(SLAB)[:, None]
    o = jnp.arange(256)[None, :]
    spread_c = (o % 16 == jnp.arange(16)[:, None]).astype(F32)
    keep_c = ((r % D) // 64) == (o // 16)
    cc = expand(jnp.concatenate([c_rows(c_re.astype(F32)), c_rows(-c_im.astype(F32))], axis=1), spread_c, keep_c)
    ab = jnp.stack([ab_re.reshape(S5_KT, D), ab_im.reshape(S5_KT, D)], axis=1)
    return bb, cc, ab


def _s5_state_to_slab(s_re, s_im):
    re = s_re.astype(F32).reshape(NB, S5_KT, D).transpose(1, 0, 2)
    im = s_im.astype(F32).reshape(NB, S5_KT, D).transpose(1, 0, 2)
    return jnp.concatenate([re, im], axis=-1)


def _s5_slab_to_state(fin):
    re = fin[..., 0:D].transpose(0, 2, 1, 3).reshape(2 * NB, 64, 64)
    im = fin[..., D:2 * D].transpose(0, 2, 1, 3).reshape(2 * NB, 64, 64)
    return re, im


def kernel(x_prompt, x_sample, state_lru, state_s5_re, state_s5_im, c, c_ctx, ada_w, ada_b, norm_mix, norm_ffn, norm_final, lru_w_in, lru_conv_w, lru_conv_b, lru_w_a, lru_b_a, lru_w_i, lru_b_i, lru_lambda, lru_w_out, s5_a_re, s5_a_im, s5_log_dt, s5_b_re, s5_b_im, s5_c_re, s5_c_im, s5_d, s5_w_glu, ffn_w_up, ffn_conv_w, ffn_conv_b, ffn_w_down):
    c_all = jnp.concatenate([c.astype(F32), c_ctx.astype(F32)[None], jnp.zeros((7, D), F32)], axis=0)
    mods = _mod_vectors(c_all, ada_w.astype(F32), ada_b.astype(F32)[:, None, :])

    def mod_of(l):
        return jnp.stack([jnp.broadcast_to(mods[l, 8][None], (NB, N_MOD * D)), mods[l, 0:NB]], axis=0)

    xp = x_prompt.astype(F32)
    gf = norm_final[None].astype(F32)

    mod0 = mod_of(0)
    zeros_h = jnp.zeros((NB, D), F32)
    sp = jax.nn.softplus(-lru_lambda[0].astype(F32)) * (-LRU_C * LOG2_E * 0.5)
    ba = 0.5 * lru_b_a[0].astype(F32)
    bi = 0.5 * lru_b_i[0].astype(F32)
    h0_f = jnp.stack([zeros_h, zeros_h, state_lru[:, 0, 0].astype(F32)], axis=0)
    h0_b = jnp.stack([zeros_h, zeros_h, state_lru[:, 0, 1].astype(F32)], axis=0)
    xs = x_sample.astype(F32)
    gate, xr, hsb, fin_b = _lru_a(xp, xs, mod0, norm_mix[0][None].astype(F32),
                                  lru_w_in[0].astype(BF16), lru_conv_w[0].astype(F32),
                                  lru_conv_b[0][None].astype(F32),
                                  _pair_blockdiag(lru_w_a[0, 1], lru_w_i[0, 1]), ba[1][None], bi[1][None],
                                  sp[1][None], h0_b)
    x, fin_f = _lru_fwd(xr, gate, hsb, xp, xs, mod0, _pair_blockdiag(lru_w_a[0, 0], lru_w_i[0, 0]),
                        ba[0][None], bi[0][None], sp[0][None], h0_f, lru_w_out[0].astype(BF16))
    ffn_args = (norm_ffn.astype(F32)[:, None, :], ffn_w_up.astype(BF16), ffn_conv_w.astype(F32),
                ffn_conv_b.astype(F32)[:, None, :], ffn_w_down.astype(BF16), gf)
    x = _ffn(x, mod0, 0, *ffn_args, False)
    new_lru = jnp.stack([fin_f[0:2].reshape(2 * NB, D), fin_b[0:2].reshape(2 * NB, D)], axis=1)[:, None]

    mod1 = mod_of(1)
    g1 = norm_mix[1][None].astype(F32)
    zeros_s = jnp.zeros((S5_KT, NB, SLAB), F32)
    dirs = []
    for d in range(2):
        bb, cc, ab = _s5_dir_params(s5_a_re[0, d], s5_a_im[0, d], s5_log_dt[0, d], s5_b_re[0, d], s5_b_im[0, d],
                                    s5_c_re[0, d], s5_c_im[0, d])
        h0 = jnp.stack([zeros_s, zeros_s, _s5_state_to_slab(state_s5_re[:, 0, d], state_s5_im[:, 0, d])], axis=0)
        dirs.append((bb, cc, ab, h0))
    yb, fin_sb = _s5_bwd(x, mod1, g1, *dirs[1])
    x, fin_sf = _s5_fwd(x, yb, mod1, g1, *dirs[0], s5_d[0][None].astype(F32), s5_w_glu[0].astype(BF16))
    y_prompt, y_sample = _ffn(x, mod1, 1, *ffn_args, True)
    f_re, f_im = _s5_slab_to_state(fin_sf[0:2])
    b_re, b_im = _s5_slab_to_state(fin_sb[0:2])
    new_s5_re = jnp.stack([f_re, b_re], axis=1)[:, None]
    new_s5_im = jnp.stack([f_im, b_im], axis=1)[:, None]

    return (y_prompt, y_sample, new_lru, new_s5_re, new_s5_im)
```

```python
import functools

import jax
import jax.numpy as jnp
from jax import lax
from jax.experimental import pallas as pl
from jax.experimental.pallas import tpu as pltpu

F32 = jnp.float32
BF16 = jnp.bfloat16

D = 1024
D_FF = 2816
N_MOD = 6
EPS = 1e-6
LRU_C = 8.0
LOG2_E = 1.4426950408889634
NB = 8
STEPS = 64
TM = STEPS * NB
HALO = 16
NORM_ROWS = 64
FF_CHUNK = 256
FFN_SUB = 2
COL_CHUNK = 256
P_LEN = 256
S_LEN = 4096
P_TILES_PER_SEQ = P_LEN // STEPS
N_P_TILES = 2 * P_TILES_PER_SEQ
N_S_TILES = S_LEN // STEPS
NT = N_P_TILES + N_S_TILES
N_ROWS = NT * TM
N_SEQ = 3
S5_KT = 4
SLAB = 2048
S5_BN = 256
S5_CK_BWD = 1024
S5_CK_FWD = 256
VMEM_LIMIT = 56 * 1024 * 1024


def _seq_id(j):
    return jnp.where(j >= N_P_TILES, 2, jnp.where(j >= P_TILES_PER_SEQ, 1, 0))


def _is_seq_first(j):
    return (j == 0) | (j == P_TILES_PER_SEQ) | (j == N_P_TILES)


def _is_seq_last(j):
    return (j == P_TILES_PER_SEQ - 1) | (j == N_P_TILES - 1) | (j == NT - 1)


def _params(sem):
    return pltpu.CompilerParams(dimension_semantics=sem, vmem_limit_bytes=VMEM_LIMIT)


def _const_spec(shape):
    n = len(shape)
    return pl.BlockSpec(shape, lambda *_: (0,) * n, pipeline_mode=pl.Buffered(1))


def _mod_spec(tile_of):
    return pl.BlockSpec((None, NB, N_MOD * D),
                        lambda i: (jnp.where(tile_of(i) >= N_P_TILES, 1, 0), 0, 0))


def _rms(xv, g):
    ms = jnp.mean(xv * xv, axis=-1, keepdims=True)
    return xv * lax.rsqrt(ms + EPS) * g


def _per_batch(y, vec):
    r, n = y.shape
    return (y.reshape(r // NB, NB, n) * vec[None]).reshape(r, n)


def _norm_mod(xv, g, scale1, shift):
    r = xv.shape[0]
    y = _rms(xv, g).reshape(r // NB, NB, D) * scale1[None] + shift[None]
    return y.reshape(r, D)


def _fill_norm(dst, dst_off, x_ref, g, scale1, shift, f32_dst=None):
    for k in range(TM // NORM_ROWS):
        r0 = k * NORM_ROWS
        y = _norm_mod(x_ref[r0:r0 + NORM_ROWS], g, scale1, shift)
        dst[dst_off + r0:dst_off + r0 + NORM_ROWS] = y.astype(BF16)
        if f32_dst is not None:
            f32_dst[r0:r0 + NORM_ROWS] = y


def _fill_hbuf(hbuf, x_ref, x_prev, x_next, g, scale1, shift):
    hbuf[0:HALO] = _norm_mod(x_prev, g, scale1, shift).astype(BF16)
    hbuf[HALO + TM:HALO + TM + HALO] = _norm_mod(x_next, g, scale1, shift).astype(BF16)
    _fill_norm(hbuf, HALO, x_ref, g, scale1, shift)


def _mod_kernel(c_ref, w_ref, b_ref, o_ref):
    cv = c_ref[...]
    s = (cv * jax.nn.sigmoid(cv)).astype(BF16)
    o_ref[...] = jnp.dot(s, w_ref[...].astype(BF16), preferred_element_type=F32) + b_ref[...]


def _mod_vectors(c_all, ada_w, ada_b):
    depth = ada_w.shape[0]
    return pl.pallas_call(
        _mod_kernel,
        out_shape=jax.ShapeDtypeStruct((depth, 16, N_MOD * D), F32),
        grid=(depth, N_MOD),
        in_specs=[
            pl.BlockSpec((16, D), lambda l, n: (0, 0)),
            pl.BlockSpec((None, D, D), lambda l, n: (l, 0, n)),
            pl.BlockSpec((None, 1, D), lambda l, n: (l, 0, n)),
        ],
        out_specs=pl.BlockSpec((None, 16, D), lambda l, n: (l, 0, n)),
        compiler_params=_params(("arbitrary", "arbitrary")),
        name="mod_vectors",
    )(c_all, ada_w, ada_b)


def _bt_specs(steps, offset, tile_steps=STEPS, tile_of=lambda i: i):
    per_tile = tile_steps // steps
    tiles_per_seq = P_LEN // tile_steps
    n_p = 2 * tiles_per_seq

    def prompt_idx(i):
        ip = jnp.minimum(tile_of(i), n_p - 1)
        blk = jnp.clip((ip % tiles_per_seq) * per_tile + offset, 0, P_LEN // steps - 1)
        return (ip // tiles_per_seq, blk, 0)

    def sample_idx(i):
        blk = jnp.clip((tile_of(i) - n_p) * per_tile + offset, 0, S_LEN // steps - 1)
        return (0, blk, 0)

    return (pl.BlockSpec((NB, steps, D), prompt_idx), pl.BlockSpec((NB, steps, D), sample_idx))


def _to_rows(x_bt):
    n = x_bt.shape[1]
    return jnp.swapaxes(x_bt, 0, 1).reshape(n * NB, D)


def _lru_gate_math(z, ba, bi, sp, xr):
    t_r = jnp.tanh(z[:, 0:128] + ba)
    t_i = jnp.tanh(z[:, 128:256] + bi)
    a = jnp.exp2(sp + sp * t_r)
    s = 1.0 - a * a
    root = jnp.where(s > 0.0, s * lax.rsqrt(s), 0.0)
    return a, root * ((0.5 + 0.5 * t_i) * xr)


def _lru_scan(a_s, bx_s, h_s, out_ref, reverse):
    h = h_s[...]
    for s in range(STEPS):
        t = (STEPS - 1 - s) if reverse else s
        rows = slice(8 * t, 8 * (t + 1))
        h = a_s[rows] * h + bx_s[rows]
        out_ref[rows] = h
    h_s[...] = h


def _lru_a_kernel(xa_ref, xap_ref, xan_ref, xb_ref, xbp_ref, xbn_ref, mod_ref, g_ref, wi_ref, cw_ref, cb_ref,
                  wp_ref, ba_ref, bi_ref, sp_ref, h0_ref,
                  gate_ref, xr_ref, hs_ref, fin_ref, hbuf, xc_s, a_s, bx_s, h_s):
    j = NT - 1 - pl.program_id(0)

    @pl.when(_is_seq_last(j))
    def _():
        h_s[...] = h0_ref[...]

    shift = mod_ref[:, 0:D]
    scale1 = 1.0 + mod_ref[:, D:2 * D]

    is_p = j < N_P_TILES
    xc_s[...] = _to_rows(jnp.where(is_p, xa_ref[...], xb_ref[...]))
    x_prev = _to_rows(jnp.where(is_p, xap_ref[...], xbp_ref[...]))[NB * NB - HALO:NB * NB]
    x_next = _to_rows(jnp.where(is_p, xan_ref[...], xbn_ref[...]))[0:HALO]
    _fill_hbuf(hbuf, xc_s, x_prev, x_next, g_ref[...], scale1, shift)

    first = _is_seq_first(j)
    last = _is_seq_last(j)

    for c in range(D // COL_CHUNK):
        cs = slice(COL_CHUNK * c, COL_CHUNK * (c + 1))
        cx = slice(D + COL_CHUNK * c, D + COL_CHUNK * (c + 1))
        u = jnp.dot(hbuf[...], wi_ref[:, cx], preferred_element_type=F32)
        head = jnp.where(first, 0.0, u[0:16])
        tail = jnp.where(last, 0.0, u[TM + 16:TM + 24])
        m2 = jnp.concatenate([head, u[16:TM]], axis=0)
        m1 = jnp.concatenate([head[8:16], u[16:TM + 8]], axis=0)
        p1 = jnp.concatenate([u[24:TM + 16], tail], axis=0)
        xr = (cb_ref[:, cs] + m2 * cw_ref[0:1, cs] + m1 * cw_ref[1:2, cs]
              + u[16:TM + 16] * cw_ref[2:3, cs] + p1 * cw_ref[3:4, cs])
        xr16 = xr.astype(BF16)
        xr_ref[:, cs] = xr16
        for h in range(COL_CHUNK // 128):
            p = (COL_CHUNK // 128) * c + h
            cols = slice(128 * p, 128 * (p + 1))
            z = jnp.dot(xr16[:, 128 * h:128 * (h + 1)], wp_ref[p], preferred_element_type=F32)
            a, bx = _lru_gate_math(z, ba_ref[:, cols], bi_ref[:, cols], sp_ref[:, cols],
                                   xr[:, 128 * h:128 * (h + 1)])
            a_s[:, cols] = a
            bx_s[:, cols] = bx
        gate_ref[:, cs] = jax.nn.gelu(
            jnp.dot(hbuf[HALO:HALO + TM], wi_ref[:, cs], preferred_element_type=F32)).astype(BF16)

    _lru_scan(a_s, bx_s, h_s, bx_s, True)
    hs_ref[...] = bx_s[...].astype(BF16)
    fin_ref[...] = h_s[...]


def _lru_a(xp, xs, mod, g, wi, cw, cb, wp, ba, bi, sp, h0):
    rev = lambda i: NT - 1 - i
    p_tile, s_tile = _bt_specs(STEPS, 0, tile_of=rev)
    p_prev, s_prev = _bt_specs(NB, -1, tile_of=rev)
    p_next, s_next = _bt_specs(NB, STEPS // NB, tile_of=rev)
    tile = pl.BlockSpec((TM, D), lambda i: (rev(i), 0))
    seq = pl.BlockSpec((None, NB, D), lambda i: (_seq_id(rev(i)), 0, 0))
    rows16 = jax.ShapeDtypeStruct((N_ROWS, D), BF16)
    return pl.pallas_call(
        _lru_a_kernel,
        out_shape=(rows16, rows16, rows16, jax.ShapeDtypeStruct((N_SEQ, NB, D), F32)),
        grid=(NT,),
        in_specs=[p_tile, p_prev, p_next, s_tile, s_prev, s_next,
                  _mod_spec(rev), _const_spec((1, D)),
                  _const_spec((D, 2 * D)), _const_spec((4, D)), _const_spec((1, D)),
                  _const_spec((8, 128, 256)), _const_spec((1, D)), _const_spec((1, D)), _const_spec((1, D)), seq],
        out_specs=(tile, tile, tile, seq),
        scratch_shapes=[pltpu.VMEM((TM + 2 * HALO, D), BF16), pltpu.VMEM((TM, D), F32), pltpu.VMEM((TM, D), F32),
                        pltpu.VMEM((TM, D), F32), pltpu.VMEM((NB, D), F32)],
        compiler_params=_params(("arbitrary",)),
        name="lru_a",
    )(xp, xp, xp, xs, xs, xs, mod, g, wi, cw, cb, wp, ba, bi, sp, h0)


def _lru_fwd_kernel(xr_ref, gate_ref, hsb_ref, xa_ref, xb_ref, mod_ref, wp_ref, ba_ref, bi_ref, sp_ref, h0_ref,
                    wo_ref, x1_ref, fin_ref, a_s, bx_s, y_s, h_s):
    j = pl.program_id(0)

    @pl.when(_is_seq_first(j))
    def _():
        h_s[...] = h0_ref[...]

    for p in range(8):
        cols = slice(128 * p, 128 * (p + 1))
        z = jnp.dot(xr_ref[:, cols], wp_ref[p], preferred_element_type=F32)
        a, bx = _lru_gate_math(z, ba_ref[:, cols], bi_ref[:, cols], sp_ref[:, cols],
                               xr_ref[:, cols].astype(F32))
        a_s[:, cols] = a
        bx_s[:, cols] = bx
    _lru_scan(a_s, bx_s, h_s, bx_s, False)
    fin_ref[...] = h_s[...]

    y_s[...] = ((bx_s[...] + hsb_ref[...].astype(F32)) * gate_ref[...].astype(F32)).astype(BF16)
    x_rows = _to_rows(jnp.where(j < N_P_TILES, xa_ref[...], xb_ref[...]))
    for c in range(D // COL_CHUNK):
        cs = slice(COL_CHUNK * c, COL_CHUNK * (c + 1))
        out = jnp.dot(y_s[...], wo_ref[:, cs], preferred_element_type=F32)
        x1_ref[:, cs] = x_rows[:, cs] + _per_batch(out, mod_ref[:, 2 * D + COL_CHUNK * c:2 * D + COL_CHUNK * (c + 1)])


def _lru_fwd(xr, gate, hsb, xp, xs, mod, wp, ba, bi, sp, h0, wo):
    tile = pl.BlockSpec((TM, D), lambda i: (i, 0))
    p_tile, s_tile = _bt_specs(STEPS, 0)
    seq = pl.BlockSpec((None, NB, D), lambda i: (_seq_id(i), 0, 0))
    return pl.pallas_call(
        _lru_fwd_kernel,
        out_shape=(jax.ShapeDtypeStruct((N_ROWS, D), F32), jax.ShapeDtypeStruct((N_SEQ, NB, D), F32)),
        grid=(NT,),
        in_specs=[tile, tile, tile, p_tile, s_tile, _mod_spec(lambda i: i), _const_spec((8, 128, 256)),
                  _const_spec((1, D)), _const_spec((1, D)), _const_spec((1, D)), seq, _const_spec((D, D))],
        out_specs=(tile, seq),
        scratch_shapes=[pltpu.VMEM((TM, D), F32), pltpu.VMEM((TM, D), F32), pltpu.VMEM((TM, D), BF16),
                        pltpu.VMEM((NB, D), F32)],
        compiler_params=_params(("arbitrary",)),
        name="lru_fwd",
    )(xr, gate, hsb, xp, xs, mod, wp, ba, bi, sp, h0, wo)


def _ffn_kernel(x_ref, xp_ref, xn_ref, mod_ref, g_ref, wu_ref, cw_ref, cb_ref, wd_ref, gf_ref,
                *rest, final_norm):
    if final_norm:
        op_ref, os_ref, hbuf, act = rest
    else:
        o_ref, hbuf, act = rest
    k = pl.program_id(0)
    g = g_ref[...]
    shift = mod_ref[:, 3 * D:4 * D]
    scale1 = 1.0 + mod_ref[:, 4 * D:5 * D]
    gt2 = mod_ref[:, 5 * D:6 * D]
    rows_all = FFN_SUB * TM

    def fill_piece(p):
        n_blocks = rows_all // NORM_ROWS
        if p == 0:
            hbuf[0:HALO] = _norm_mod(xp_ref[...], g, scale1, shift).astype(BF16)
        elif p <= n_blocks:
            r0 = (p - 1) * NORM_ROWS
            hbuf[HALO + r0:HALO + r0 + NORM_ROWS] = _norm_mod(
                x_ref[r0:r0 + NORM_ROWS], g, scale1, shift).astype(BF16)
        elif p == n_blocks + 1:
            hbuf[HALO + rows_all:HALO + rows_all + HALO] = _norm_mod(xn_ref[...], g, scale1, shift).astype(BF16)

    n_pieces = rows_all // NORM_ROWS + 2
    first_pieces = TM // NORM_ROWS + 2
    for p in range(first_pieces):
        fill_piece(p)
    next_piece = first_pieces
    finals = []

    for s in range(FFN_SUB):
        tile = FFN_SUB * k + s
        is_prompt = tile < N_P_TILES
        seg_first = jnp.logical_not(is_prompt & ((tile % P_TILES_PER_SEQ) != 0))
        seg_last = jnp.logical_not(is_prompt & ((tile % P_TILES_PER_SEQ) != P_TILES_PER_SEQ - 1))
        h_rows = slice(TM * s, TM * s + TM + 2 * HALO)

        def conv3(cs):
            u = jnp.dot(hbuf[h_rows], wu_ref[:, cs], preferred_element_type=F32)
            p0 = jnp.where(seg_first, 0.0, u[8:16])
            n0 = jnp.where(seg_last, 0.0, u[TM + 16:TM + 24])
            prev = jnp.concatenate([p0, u[16:TM + 8]], axis=0)
            nxt = jnp.concatenate([u[24:TM + 16], n0], axis=0)
            return (cb_ref[:, cs] + prev * cw_ref[0:1, cs] + u[16:TM + 16] * cw_ref[1:2, cs]
                    + nxt * cw_ref[2:3, cs])

        for c in range(D_FF // FF_CHUNK):
            v = conv3(slice(FF_CHUNK * c, FF_CHUNK * (c + 1)))
            gg = conv3(slice(D_FF + FF_CHUNK * c, D_FF + FF_CHUNK * (c + 1)))
            act[s, :, FF_CHUNK * c:FF_CHUNK * (c + 1)] = (v * (gg * jax.nn.sigmoid(gg))).astype(BF16)
            if next_piece < n_pieces:
                fill_piece(next_piece)
                next_piece += 1

        out = jnp.dot(act[s], wd_ref[...], preferred_element_type=F32)
        y = x_ref[TM * s:TM * (s + 1)] + _per_batch(out, gt2)
        if final_norm:
            finals.append(jnp.swapaxes(_rms(y, gf_ref[...]).reshape(STEPS, NB, D), 0, 1))
        else:
            o_ref[TM * s:TM * (s + 1)] = y

    if final_norm:
        y_bt = jnp.concatenate(finals, axis=1)

        @pl.when(k < N_P_TILES // FFN_SUB)
        def _():
            op_ref[...] = y_bt

        @pl.when(k >= N_P_TILES // FFN_SUB)
        def _():
            os_ref[...] = y_bt


def _ffn(x, mod, layer, g, wu, cw, cb, wd, gf, final_norm):
    rows = FFN_SUB * TM
    n_steps = NT // FFN_SUB
    blk = pl.BlockSpec((rows, D), lambda i: (i, 0))
    prev = pl.BlockSpec((HALO, D), lambda i: (jnp.maximum(i * (rows // HALO) - 1, 0), 0))
    nxt = pl.BlockSpec((HALO, D), lambda i: (jnp.minimum((i + 1) * (rows // HALO), N_ROWS // HALO - 1), 0))
    mod_spec = pl.BlockSpec((None, NB, N_MOD * D), lambda i: (jnp.where(i >= N_P_TILES // FFN_SUB, 1, 0), 0, 0))

    def layer_spec(shape):
        n = len(shape)
        return pl.BlockSpec((None,) + shape, lambda *_: (layer,) + (0,) * n, pipeline_mode=pl.Buffered(1))

    if final_norm:
        out_shape = (jax.ShapeDtypeStruct((2 * NB, P_LEN, D), F32), jax.ShapeDtypeStruct((NB, S_LEN, D), F32))
        out_specs = _bt_specs(FFN_SUB * STEPS, 0, FFN_SUB * STEPS)
    else:
        out_shape = jax.ShapeDtypeStruct((N_ROWS, D), F32)
        out_specs = blk
    return pl.pallas_call(
        functools.partial(_ffn_kernel, final_norm=final_norm),
        out_shape=out_shape,
        grid=(n_steps,),
        in_specs=[blk, prev, nxt, mod_spec, layer_spec((1, D)),
                  layer_spec((D, 2 * D_FF)), layer_spec((3, 2 * D_FF)), layer_spec((1, 2 * D_FF)),
                  layer_spec((D_FF, D)), _const_spec((1, D))],
        out_specs=out_specs,
        scratch_shapes=[pltpu.VMEM((rows + 2 * HALO, D), BF16), pltpu.VMEM((FFN_SUB, TM, D_FF), BF16)],
        compiler_params=_params(("arbitrary",)),
        name="conv_ffn",
    )(x, x, x, mod, g, wu, cw, cb, wd, gf)


def _s5_core(ubuf, bb_ref, cc_ref, ab_ref, slabs, hs16, hst, y_write, reverse, c_cols):
    def b_piece(kt, n):
        cs = slice(S5_BN * n, S5_BN * (n + 1))
        slabs[kt % 2, :, cs] = jnp.dot(ubuf[:, 256 * kt:256 * (kt + 1)], bb_ref[kt, :, cs],
                                       preferred_element_type=F32)

    def c_piece(kt, q):
        cs = slice(c_cols * q, c_cols * (q + 1))
        return jnp.dot(hs16[kt % 2, :, cs], cc_ref[kt, cs, :], preferred_element_type=F32)

    n_b = SLAB // S5_BN
    every = STEPS // n_b
    n_c = SLAB // c_cols
    c_every = STEPS // n_c
    for n in range(n_b):
        b_piece(0, n)
    for kt in range(S5_KT):
        slab = slabs.at[kt % 2]
        hs = hs16.at[kt % 2]
        a_re = jnp.broadcast_to(ab_ref[kt, 0:1, :], (NB, D))
        a_im = jnp.broadcast_to(ab_ref[kt, 1:2, :], (NB, D))
        h_re = hst[kt, :, 0:D]
        h_im = hst[kt, :, D:2 * D]
        part = None
        for s in range(STEPS):
            if s % every == 0 and kt + 1 < S5_KT:
                b_piece(kt + 1, s // every)
            if s % c_every == c_every // 2 and kt >= 1:
                q = s // c_every
                part = c_piece(kt - 1, q) if q == 0 else part + c_piece(kt - 1, q)
                if q == n_c - 1:
                    y_write(kt - 1, part)
            t = (STEPS - 1 - s) if reverse else s
            rows = slice(8 * t, 8 * (t + 1))
            n_re = a_re * h_re - a_im * h_im + slab[rows, 0:D]
            n_im = a_re * h_im + a_im * h_re + slab[rows, D:2 * D]
            if s % 2 == 1:
                lo = min(t, t + 1 if reverse else t - 1)
                pair = slice(8 * lo, 8 * lo + 16)
                first, second = ((n_re, h_re), (n_im, h_im)) if reverse else ((h_re, n_re), (h_im, n_im))
                hs[pair, 0:D] = jnp.concatenate(first, axis=0).astype(BF16)
                hs[pair, D:2 * D] = jnp.concatenate(second, axis=0).astype(BF16)
            h_re, h_im = n_re, n_im
        hst[kt, :, 0:D] = h_re
        hst[kt, :, D:2 * D] = h_im
    part = c_piece(S5_KT - 1, 0)
    for q in range(1, n_c):
        part = part + c_piece(S5_KT - 1, q)
    y_write(S5_KT - 1, part)


def _s5_bwd_kernel(x_ref, mod_ref, g_ref, bb_ref, cc_ref, ab_ref, h0_ref,
                   yb_ref, fin_ref, ubuf, slabs, hs16, hst):
    j = NT - 1 - pl.program_id(0)

    @pl.when(_is_seq_last(j))
    def _():
        hst[...] = h0_ref[...]

    _fill_norm(ubuf, 0, x_ref, g_ref[...], 1.0 + mod_ref[:, D:2 * D], mod_ref[:, 0:D])

    def y_write(kt, val):
        yb_ref[:, 256 * kt:256 * (kt + 1)] = val

    _s5_core(ubuf, bb_ref, cc_ref, ab_ref, slabs, hs16, hst, y_write, True, S5_CK_BWD)
    fin_ref[...] = hst[...]


def _s5_bwd(x, mod, g, bb, cc, ab, h0):
    tile = pl.BlockSpec((TM, D), lambda i: (NT - 1 - i, 0))
    seq = pl.BlockSpec((None, S5_KT, NB, SLAB), lambda i: (_seq_id(NT - 1 - i), 0, 0, 0))
    return pl.pallas_call(
        _s5_bwd_kernel,
        out_shape=(jax.ShapeDtypeStruct((N_ROWS, D), F32),
                   jax.ShapeDtypeStruct((N_SEQ, S5_KT, NB, SLAB), F32)),
        grid=(NT,),
        in_specs=[tile, _mod_spec(lambda i: NT - 1 - i), _const_spec((1, D)),
                  _const_spec((S5_KT, 256, SLAB)), _const_spec((S5_KT, SLAB, 256)),
                  _const_spec((S5_KT, 2, D)), seq],
        out_specs=(tile, seq),
        scratch_shapes=[pltpu.VMEM((TM, D), BF16), pltpu.VMEM((2, TM, SLAB), F32), pltpu.VMEM((2, TM, SLAB), BF16),
                        pltpu.VMEM((S5_KT, NB, SLAB), F32)],
        compiler_params=_params(("arbitrary",)),
        name="s5_bwd",
    )(x, mod, g, bb, cc, ab, h0)


def _s5_fwd_kernel(x_ref, yb_ref, mod_ref, g_ref, bb_ref, cc_ref, ab_ref, h0_ref, dsk_ref, wglu_ref,
                   x1_ref, fin_ref, ubuf, zbuf, u_s, slabs, hs16, hst):
    j = pl.program_id(0)

    @pl.when(_is_seq_first(j))
    def _():
        hst[...] = h0_ref[...]

    _fill_norm(ubuf, 0, x_ref, g_ref[...], 1.0 + mod_ref[:, D:2 * D], mod_ref[:, 0:D], u_s)

    def y_write(kt, val):
        cs = slice(256 * kt, 256 * (kt + 1))
        y = dsk_ref[:, cs] * u_s[:, cs] + yb_ref[:, cs] + val
        zbuf[:, cs] = jax.nn.gelu(y).astype(BF16)

    _s5_core(ubuf, bb_ref, cc_ref, ab_ref, slabs, hs16, hst, y_write, False, S5_CK_FWD)
    fin_ref[...] = hst[...]

    for c in range(D // COL_CHUNK):
        cs = slice(COL_CHUNK * c, COL_CHUNK * (c + 1))
        v = jnp.dot(zbuf[...], wglu_ref[:, cs], preferred_element_type=F32)
        gg = jnp.dot(zbuf[...], wglu_ref[:, D + COL_CHUNK * c:D + COL_CHUNK * (c + 1)], preferred_element_type=F32)
        out = v * jax.nn.sigmoid(gg)
        x1_ref[:, cs] = x_ref[:, cs] + _per_batch(out, mod_ref[:, 2 * D + COL_CHUNK * c:2 * D + COL_CHUNK * (c + 1)])


def _s5_fwd(x, yb, mod, g, bb, cc, ab, h0, dsk, wglu):
    tile = pl.BlockSpec((TM, D), lambda i: (i, 0))
    seq = pl.BlockSpec((None, S5_KT, NB, SLAB), lambda i: (_seq_id(i), 0, 0, 0))
    return pl.pallas_call(
        _s5_fwd_kernel,
        out_shape=(jax.ShapeDtypeStruct((N_ROWS, D), F32),
                   jax.ShapeDtypeStruct((N_SEQ, S5_KT, NB, SLAB), F32)),
        grid=(NT,),
        in_specs=[tile, tile, _mod_spec(lambda i: i), _const_spec((1, D)),
                  _const_spec((S5_KT, 256, SLAB)), _const_spec((S5_KT, SLAB, 256)),
                  _const_spec((S5_KT, 2, D)), seq, _const_spec((1, D)), _const_spec((D, 2 * D))],
        out_specs=(tile, seq),
        scratch_shapes=[pltpu.VMEM((TM, D), BF16), pltpu.VMEM((TM, D), BF16), pltpu.VMEM((TM, D), F32),
                        pltpu.VMEM((2, TM, SLAB), F32), pltpu.VMEM((2, TM, SLAB), BF16),
                        pltpu.VMEM((S5_KT, NB, SLAB), F32)],
        compiler_params=_params(("arbitrary",)),
        name="s5_fwd",
    )(x, yb, mod, g, bb, cc, ab, h0, dsk, wglu)


def _pair_blockdiag(w_a, w_i):
    same = (jnp.arange(2)[:, None, None, None] == jnp.arange(2)[None, None, :, None])

    def bd(w):
        w4 = w.astype(F32).reshape(8, 2, 64, 1, 64)
        return jnp.where(same[None], w4, 0.0).reshape(8, 128, 128)

    return (0.5 * jnp.concatenate([bd(w_a), bd(w_i)], axis=-1)).astype(BF16)


def _s5_dir_params(a_re, a_im, log_dt, b_re, b_im, c_re, c_im):
    l_re = a_re.astype(F32)
    l_im = a_im.astype(F32)
    dt = jnp.exp(log_dt.astype(F32))[:, None]
    mag = jnp.exp(l_re * dt)
    ab_re = mag * jnp.cos(l_im * dt)
    ab_im = mag * jnp.sin(l_im * dt)
    den = l_re * l_re + l_im * l_im
    k_re = ((ab_re - 1.0) * l_re + ab_im * l_im) / den
    k_im = (ab_im * l_re - (ab_re - 1.0) * l_im) / den
    br = b_re.astype(F32)
    bi = b_im.astype(F32)
    bb_re = br * k_re[..., None] - bi * k_im[..., None]
    bb_im = br * k_im[..., None] + bi * k_re[..., None]
    def expand(rows, spread, keep):
        tiled = jnp.einsum('krc,cq->krq', rows, spread, precision=lax.Precision.HIGHEST)
        return jnp.where(keep[None], tiled, 0.0).astype(BF16)

    def b_rows(m):
        return m.reshape(S5_KT, 16, 64, 16).transpose(0, 1, 3, 2).reshape(S5_KT, 256, 64)

    def c_rows(m):
        return m.reshape(S5_KT, 16, 16, 64).transpose(0, 1, 3, 2).reshape(S5_KT, 1024, 16)

    q = jnp.arange(SLAB)[None, :]
    h = jnp.arange(128)[:, None]
    spread_b = ((q // D == h // 64) & (q % 64 == h % 64)).astype(F32)
    keep_b = (jnp.arange(256)[:, None] // 16) == ((q % D) // 64)
    bb = expand(jnp.concatenate([b_rows(bb_re), b_rows(bb_im)], axis=-1), spread_b, keep_b)

    r = jnp.arange(SLAB)[:, None]
    o = jnp.arange(256)[None, :]
    spread_c = (o % 16 == jnp.arange(16)[:, None]).astype(F32)
    keep_c = ((r % D) // 64) == (o // 16)
    cc = expand(jnp.concatenate([c_rows(c_re.astype(F32)), c_rows(-c_im.astype(F32))], axis=1), spread_c, keep_c)
    ab = jnp.stack([ab_re.reshape(S5_KT, D), ab_im.reshape(S5_KT, D)], axis=1)
    return bb, cc, ab


def _s5_state_to_slab(s_re, s_im):
    re = s_re.astype(F32).reshape(NB, S5_KT, D).transpose(1, 0, 2)
    im = s_im.astype(F32).reshape(NB, S5_KT, D).transpose(1, 0, 2)
    return jnp.concatenate([re, im], axis=-1)


def _s5_slab_to_state(fin):
    re = fin[..., 0:D].transpose(0, 2, 1, 3).reshape(2 * NB, 64, 64)
    im = fin[..., D:2 * D].transpose(0, 2, 1, 3).reshape(2 * NB, 64, 64)
    return re, im


def kernel(x_prompt, x_sample, state_lru, state_s5_re, state_s5_im, c, c_ctx, ada_w, ada_b, norm_mix, norm_ffn, norm_final, lru_w_in, lru_conv_w, lru_conv_b, lru_w_a, lru_b_a, lru_w_i, lru_b_i, lru_lambda, lru_w_out, s5_a_re, s5_a_im, s5_log_dt, s5_b_re, s5_b_im, s5_c_re, s5_c_im, s5_d, s5_w_glu, ffn_w_up, ffn_conv_w, ffn_conv_b, ffn_w_down):
    c_all = jnp.concatenate([c.astype(F32), c_ctx.astype(F32)[None], jnp.zeros((7, D), F32)], axis=0)
    mods = _mod_vectors(c_all, ada_w.astype(F32), ada_b.astype(F32)[:, None, :])

    def mod_of(l):
        return jnp.stack([jnp.broadcast_to(mods[l, 8][None], (NB, N_MOD * D)), mods[l, 0:NB]], axis=0)

    xp = x_prompt.astype(F32)
    gf = norm_final[None].astype(F32)

    mod0 = mod_of(0)
    zeros_h = jnp.zeros((NB, D), F32)
    sp = jax.nn.softplus(-lru_lambda[0].astype(F32)) * (-LRU_C * LOG2_E * 0.5)
    ba = 0.5 * lru_b_a[0].astype(F32)
    bi = 0.5 * lru_b_i[0].astype(F32)
    h0_f = jnp.stack([zeros_h, zeros_h, state_lru[:, 0, 0].astype(F32)], axis=0)
    h0_b = jnp.stack([zeros_h, zeros_h, state_lru[:, 0, 1].astype(F32)], axis=0)
    xs = x_sample.astype(F32)
    gate, xr, hsb, fin_b = _lru_a(xp, xs, mod0, norm_mix[0][None].astype(F32),
                                  lru_w_in[0].astype(BF16), lru_conv_w[0].astype(F32),
                                  lru_conv_b[0][None].astype(F32),
                                  _pair_blockdiag(lru_w_a[0, 1], lru_w_i[0, 1]), ba[1][None], bi[1][None],
                                  sp[1][None], h0_b)
    x, fin_f = _lru_fwd(xr, gate, hsb, xp, xs, mod0, _pair_blockdiag(lru_w_a[0, 0], lru_w_i[0, 0]),
                        ba[0][None], bi[0][None], sp[0][None], h0_f, lru_w_out[0].astype(BF16))
    ffn_args = (norm_ffn.astype(F32)[:, None, :], ffn_w_up.astype(BF16), ffn_conv_w.astype(F32),
                ffn_conv_b.astype(F32)[:, None, :], ffn_w_down.astype(BF16), gf)
    x = _ffn(x, mod0, 0, *ffn_args, False)
    new_lru = jnp.stack([fin_f[0:2].reshape(2 * NB, D), fin_b[0:2].reshape(2 * NB, D)], axis=1)[:, None]

    mod1 = mod_of(1)
    g1 = norm_mix[1][None].astype(F32)
    zeros_s = jnp.zeros((S5_KT, NB, SLAB), F32)
    dirs = []
    for d in range(2):
        bb, cc, ab = _s5_dir_params(s5_a_re[0, d], s5_a_im[0, d], s5_log_dt[0, d], s5_b_re[0, d], s5_b_im[0, d],
                                    s5_c_re[0, d], s5_c_im[0, d])
        h0 = jnp.stack([zeros_s, zeros_s, _s5_state_to_slab(state_s5_re[:, 0, d], state_s5_im[:, 0, d])], axis=0)
        dirs.append((bb, cc, ab, h0))
    yb, fin_sb = _s5_bwd(x, mod1, g1, *dirs[1])
    x, fin_sf = _s5_fwd(x, yb, mod1, g1, *dirs[0], s5_d[0][None].astype(F32), s5_w_glu[0].astype(BF16))
    y_prompt, y_sample = _ffn(x, mod1, 1, *ffn_args, True)
    f_re, f_im = _s5_slab_to_state(fin_sf[0:2])
    b_re, b_im = _s5_slab_to_state(fin_sb[0:2])
    new_s5_re = jnp.stack([f_re, b_re], axis=1)[:, None]
    new_s5_im = jnp.stack([f_im, b_im], axis=1)[:, None]

    return (y_prompt, y_sample, new_lru, new_s5_re, new_s5_im)
```
